```python
import math
import jax
import jax.numpy as jnp
from jax import lax
import numpy as np


D_MODEL = 2048
BATCH = 8
SEQ = 4096
DEPTH = 4

GRID_W = 64
CTX_LEN = 256
N_MIXERS = 4
EPS = 1e-6
DA_HEADS = 16
DA_HALF = D_MODEL // DA_HEADS // 2
DA_VDIM = 2 * DA_HALF
Q_BLOCK = 128
ROPE_BASE = 10000.0
GDN_HEADS = 16
GDN_DK = D_MODEL // GDN_HEADS
GDN_DV = D_MODEL // GDN_HEADS
GDN_CONV = 5
GDN_CHUNK = 64
NA_HEADS = 16
NA_DIM = D_MODEL // NA_HEADS
WIN_R = 8
WIN_C = 16
HG_HEADS = 16
HG_DK = D_MODEL // HG_HEADS
HG_DV = D_MODEL // HG_HEADS
HG_CHUNK = 32
N_EXPERTS = 16
N_GROUPS = 4
EXPERTS_PER_GROUP = N_EXPERTS // N_GROUPS
GROUP_SCORE_TOPK = 2
TOP_K = 2
D_EXPERT = D_MODEL // 2

kernel_name = 'hybrid_flow_backbone'


def rmsnorm(x, g, eps=EPS):
    xf = x.astype(jnp.float32)
    y = xf * lax.rsqrt(jnp.mean(xf * xf, axis=-1, keepdims=True) + eps)
    return y.astype(x.dtype) * g


def l2norm(x):
    return x * lax.rsqrt(jnp.sum(x * x, axis=-1, keepdims=True) + EPS)


def flip(t):
    return None if t is None else jnp.flip(t, axis=1)


def axial_rope_tables(length, dim):
    n_freq = dim // 4
    t = jnp.arange(length)
    pos = jnp.stack([t // GRID_W, t % GRID_W], axis=-1).astype(jnp.float32)
    inv = ROPE_BASE ** (-jnp.arange(n_freq, dtype=jnp.float32) / n_freq)
    ang = pos[:, :, None] * inv
    return jnp.cos(ang), jnp.sin(ang)


def apply_rope(x, cos, sin):
    n_freq = cos.shape[-1]
    xr = x.reshape(*x.shape[:-1], 2, 2, n_freq)
    x1, x2 = xr[..., 0, :], xr[..., 1, :]
    c = cos[:, None].astype(x.dtype)
    s = sin[:, None].astype(x.dtype)
    out = jnp.stack([x1 * c - x2 * s, x1 * s + x2 * c], axis=-2)
    return out.reshape(x.shape)


def diff_core(q1, q2, k1, k2, v, lam):
    scale = DA_HALF ** -0.5
    s1 = jnp.einsum('bqhd,bkhd->bhqk', q1, k1).astype(jnp.float32) * scale
    s2 = jnp.einsum('bqhd,bkhd->bhqk', q2, k2).astype(jnp.float32) * scale
    p = jax.nn.softmax(s1, axis=-1) - lam * jax.nn.softmax(s2, axis=-1)
    return jnp.einsum('bhqk,bkhe->bqhe', p.astype(v.dtype), v)


def diff_attention(h, hc, w_qkv, w_o, lam_vec, subln_g, layer_idx, ctx_out):
    B, L, _ = h.shape
    tc = hc.shape[1]
    dq = 2 * DA_HEADS * DA_HALF
    lam_init = 0.8 - 0.6 * math.exp(-0.3 * layer_idx)
    lv = lam_vec.astype(jnp.float32)
    lam = jnp.exp(jnp.sum(lv[0] * lv[1])) - jnp.exp(jnp.sum(lv[2] * lv[3])) + lam_init

    def qk_heads(t):
        return t.reshape(t.shape[0], t.shape[1], 2 * DA_HEADS, DA_HALF)

    def split_maps(t):
        t = t.reshape(t.shape[0], t.shape[1], DA_HEADS, 2, DA_HALF)
        return t[:, :, :, 0], t[:, :, :, 1]

    def finish(o):
        o = rmsnorm(o, subln_g, 1e-5) * (1.0 - lam_init)
        return o.reshape(o.shape[0], o.shape[1], -1) @ w_o

    qkv = h @ w_qkv
    cos, sin = axial_rope_tables(L, DA_HALF)
    q1, q2 = split_maps(apply_rope(qk_heads(qkv[..., :dq]), cos, sin))
    k1, k2 = split_maps(apply_rope(qk_heads(qkv[..., dq:2 * dq]), cos, sin))
    v = qkv[..., 2 * dq:].reshape(B, L, DA_HEADS, DA_VDIM)
    if ctx_out:
        qkv_c = hc @ w_qkv
        qc1, qc2 = split_maps(qk_heads(qkv_c[..., :dq]))
        kv_c = qkv_c[..., dq:]
    else:
        kv_c = hc @ w_qkv[:, dq:]
    kc1, kc2 = split_maps(qk_heads(kv_c[..., :dq]))
    vc = kv_c[..., dq:].reshape(B, tc, DA_HEADS, DA_VDIM)

    K1 = jnp.concatenate([kc1, k1], axis=1)
    K2 = jnp.concatenate([kc2, k2], axis=1)
    V = jnp.concatenate([vc, v], axis=1)
    nb = L // Q_BLOCK

    def blocks(t):
        return t.reshape(B, nb, Q_BLOCK, DA_HEADS, DA_HALF).swapaxes(0, 1)

    o = lax.map(lambda qs: diff_core(qs[0], qs[1], K1, K2, V, lam), (blocks(q1), blocks(q2)))
    y = finish(o.swapaxes(0, 1).reshape(B, L, DA_HEADS, DA_VDIM))
    yc = finish(diff_core(qc1, qc2, kc1, kc2, vc, lam)) if ctx_out else None
    return y, yc


def centred_dwconv(x, w):
    k = w.shape[0]
    pad = k // 2
    t = x.shape[1]
    xp = jnp.pad(x, ((0, 0), (pad, pad), (0, 0)))
    out = xp[:, 0:t] * w[0]
    for j in range(1, k):
        out = out + xp[:, j:j + t] * w[j]
    return out


def gated_delta_chunked(q, k, v, log_a, beta, s0, with_output):
    B, T, H, DK = k.shape
    C = GDN_CHUNK
    n = T // C

    def chunks(t):
        return t.reshape(B, n, C, H, -1).transpose(1, 0, 3, 2, 4)

    def chunks_s(t):
        return t.reshape(B, n, C, H).transpose(1, 0, 3, 2)

    kc, vc = chunks(k), chunks(v)
    bc = chunks_s(beta)[..., None]
    g = jnp.cumsum(chunks_s(log_a), axis=-1)
    idx = jnp.arange(C)
    incl = idx[:, None] >= idx[None, :]
    strict = idx[:, None] > idx[None, :]
    decay = jnp.exp(jnp.where(incl, g[..., :, None] - g[..., None, :], -jnp.inf))
    kb = kc * bc
    a_mat = jnp.where(strict, jnp.einsum('nbhid,nbhjd->nbhij', kb, kc) * decay, 0.0)
    eye = jnp.eye(C, dtype=jnp.float32)
    t_mat = lax.linalg.triangular_solve(eye + a_mat, jnp.broadcast_to(eye, a_mat.shape),
                                        left_side=True, lower=True)
    u = t_mat @ (vc * bc)
    w = t_mat @ (kb * jnp.exp(g)[..., None])
    g_last = g[..., -1:]
    k_dec = kc * jnp.exp(g_last - g)[..., None]
    s_dec = jnp.exp(g_last)[..., None]
    if not with_output:
        def step_state(s, xs):
            u_i, w_i, kd_i, sd_i = xs
            v_new = u_i - w_i @ s
            return s * sd_i + jnp.swapaxes(kd_i, -1, -2) @ v_new, None
        s, _ = lax.scan(step_state, s0, (u, w, k_dec, s_dec))
        return None, s
    qc = chunks(q) * DK ** -0.5
    q_dec = qc * jnp.exp(g)[..., None]
    qk = jnp.where(incl, jnp.einsum('nbhid,nbhjd->nbhij', qc, kc) * decay, 0.0)

    def step(s, xs):
        u_i, w_i, kd_i, sd_i, qd_i, qk_i = xs
        v_new = u_i - w_i @ s
        o = qd_i @ s + qk_i @ v_new
        return s * sd_i + jnp.swapaxes(kd_i, -1, -2) @ v_new, o

    s, o = lax.scan(step, s0, (u, w, k_dec, s_dec, q_dec, qk))
    return o.transpose(1, 0, 3, 2, 4).reshape(B, T, H, -1), s


def gdn_features(t, w_in, conv_w, w_ab, dt_bias, a_log, with_q):
    B, T, _ = t.shape
    wq = GDN_HEADS * GDN_DK
    wv = GDN_HEADS * GDN_DV
    lo = 0 if with_q else wq
    hi = 2 * wq + wv
    z = jax.nn.silu(centred_dwconv(t @ w_in[:, lo:hi], conv_w[:, lo:hi])).astype(jnp.float32)
    if with_q:
        q = l2norm(z[..., :wq].reshape(B, T, GDN_HEADS, GDN_DK))
        z = z[..., wq:]
    else:
        q = None
    k = l2norm(z[..., :wq].reshape(B, T, GDN_HEADS, GDN_DK))
    v = z[..., wq:].reshape(B, T, GDN_HEADS, GDN_DV)
    ab = (t @ w_ab).astype(jnp.float32).reshape(B, T, 2, 2, GDN_HEADS)
    log_a = -jnp.exp(a_log.astype(jnp.float32)) * jax.nn.softplus(ab[:, :, 0] + dt_bias.astype(jnp.float32))
    beta = jax.nn.sigmoid(ab[:, :, 1])
    return q, k, v, log_a, beta


def gated_deltanet(h, hc, w_in, conv_w, w_ab, dt_bias, a_log, norm_g, w_o, ctx_out):
    B = h.shape[0]
    gate_lo = 2 * GDN_HEADS * GDN_DK + GDN_HEADS * GDN_DV
    s0 = jnp.zeros((B, GDN_HEADS, GDN_DK, GDN_DV), jnp.float32)
    qc, kc, vc, lac, bc = gdn_features(hc, w_in, conv_w, w_ab, dt_bias, a_log, ctx_out)
    q, k, v, la, be = gdn_features(h, w_in, conv_w, w_ab, dt_bias, a_log, True)
    oc_f, sc_f = gated_delta_chunked(qc, kc, vc, lac[:, :, 0], bc[:, :, 0], s0, ctx_out)
    o_f, _ = gated_delta_chunked(q, k, v, la[:, :, 0], be[:, :, 0], sc_f, True)
    oc_b, sc_b = gated_delta_chunked(flip(qc), flip(kc), flip(vc), flip(lac[:, :, 1]), flip(bc[:, :, 1]), s0, ctx_out)
    o_b, _ = gated_delta_chunked(flip(q), flip(k), flip(v), flip(la[:, :, 1]), flip(be[:, :, 1]), sc_b, True)

    def finish(t, o):
        T = t.shape[1]
        gate = jax.nn.silu(t @ w_in[:, gate_lo:]).reshape(B, T, GDN_HEADS, GDN_DV)
        o = rmsnorm(o, norm_g).astype(t.dtype) * gate
        return o.reshape(B, T, -1) @ w_o

    y = finish(h, o_f + flip(o_b))
    yc = finish(hc, oc_f + flip(oc_b)) if ctx_out else None
    return y, yc


def neighbourhood_attention(h, hc, w_qkv, rpb, w_o, ctx_out):
    B, L, _ = h.shape
    tc = hc.shape[1]
    rows = L // GRID_W
    kr = min(WIN_R, rows)
    hd = NA_HEADS * NA_DIM
    scale = NA_DIM ** -0.5
    qkv = (h @ w_qkv).reshape(B, L, 3, NA_HEADS, NA_DIM)
    q, k, v = qkv[:, :, 0], qkv[:, :, 1], qkv[:, :, 2]
    if ctx_out:
        qkv_c = (hc @ w_qkv).reshape(B, tc, 3, NA_HEADS, NA_DIM)
        qc, kc, vc = qkv_c[:, :, 0], qkv_c[:, :, 1], qkv_c[:, :, 2]
    else:
        kv_c = (hc @ w_qkv[:, hd:]).reshape(B, tc, 2, NA_HEADS, NA_DIM)
        kc, vc = kv_c[:, :, 0], kv_c[:, :, 1]
    k_g = k.reshape(B, rows, GRID_W, NA_HEADS, NA_DIM)
    v_g = v.reshape(B, rows, GRID_W, NA_HEADS, NA_DIM)
    col = jnp.arange(GRID_W)
    c0 = jnp.clip(col - WIN_C // 2, 0, GRID_W - WIN_C)
    col_mask = (col[None, :] >= c0[:, None]) & (col[None, :] < c0[:, None] + WIN_C)
    dc_idx = jnp.clip(col[None, :] - col[:, None], -(WIN_C - 1), WIN_C - 1) + (WIN_C - 1)

    def row_block(args):
        r, q_r = args
        r0 = jnp.clip(r - kr // 2, 0, rows - kr)
        k_b = lax.dynamic_slice_in_dim(k_g, r0, kr, axis=1)
        v_b = lax.dynamic_slice_in_dim(v_g, r0, kr, axis=1)
        dr_idx = r0 + jnp.arange(kr) - r + (WIN_R - 1)
        bias = rpb[:, dr_idx[None, :, None], dc_idx[:, None, :]].astype(jnp.float32)
        s_win = jnp.einsum('bqhd,brwhd->bhqrw', q_r, k_b).astype(jnp.float32) * scale + bias
        s_win = jnp.where(col_mask[:, None, :], s_win, -jnp.inf).reshape(B, NA_HEADS, GRID_W, kr * GRID_W)
        s_ctx = jnp.einsum('bqhd,bkhd->bhqk', q_r, kc).astype(jnp.float32) * scale
        p = jax.nn.softmax(jnp.concatenate([s_win, s_ctx], axis=-1), axis=-1).astype(v.dtype)
        p_win = p[..., :kr * GRID_W].reshape(B, NA_HEADS, GRID_W, kr, GRID_W)
        return (jnp.einsum('bhqrw,brwhd->bqhd', p_win, v_b)
                + jnp.einsum('bhqk,bkhd->bqhd', p[..., kr * GRID_W:], vc))

    o = lax.map(row_block, (jnp.arange(rows), q.reshape(B, rows, GRID_W, NA_HEADS, NA_DIM).swapaxes(0, 1)))
    y = o.swapaxes(0, 1).reshape(B, L, hd) @ w_o
    if ctx_out:
        s = jnp.einsum('bqhd,bkhd->bhqk', qc, kc).astype(jnp.float32) * scale
        p = jax.nn.softmax(s, axis=-1).astype(vc.dtype)
        yc = jnp.einsum('bhqk,bkhd->bqhd', p, vc).reshape(B, tc, hd) @ w_o
    else:
        yc = None
    return y, yc


def gla_chunked(q, k, v, log_f, s0, with_output):
    B, T, H, DK = k.shape
    C = HG_CHUNK
    n = T // C

    def chunks(t):
        return t.reshape(B, n, C, H, -1).transpose(1, 0, 3, 2, 4)

    kc, vc = chunks(k), chunks(v)
    b = jnp.cumsum(chunks(log_f), axis=-2)
    b_last = b[..., -1:, :]
    k_dec = kc * jnp.exp(b_last - b)
    s_dec = jnp.swapaxes(jnp.exp(b_last), -1, -2)
    if not with_output:
        def step_state(s, xs):
            kd_i, v_i, sd_i = xs
            return s * sd_i + jnp.swapaxes(kd_i, -1, -2) @ v_i, None
        s, _ = lax.scan(step_state, s0, (k_dec, vc, s_dec))
        return None, s
    qc = chunks(q) * DK ** -0.5
    idx = jnp.arange(C)
    incl = (idx[:, None] >= idx[None, :])[:, :, None]

    def step(s, xs):
        q_i, k_i, v_i, b_i, kd_i, sd_i = xs
        rel = jnp.exp(jnp.where(incl, b_i[..., :, None, :] - b_i[..., None, :, :], -jnp.inf))
        att = jnp.sum(q_i[..., :, None, :] * k_i[..., None, :, :] * rel, axis=-1)
        o = (q_i * jnp.exp(b_i)) @ s + att @ v_i
        return s * sd_i + jnp.swapaxes(kd_i, -1, -2) @ v_i, o

    s, o = lax.scan(step, s0, (qc, kc, vc, b, k_dec, s_dec))
    return o.transpose(1, 0, 3, 2, 4).reshape(B, T, H, -1), s


def hgrn2(h, hc, w_q, w_i, w_f, b_f, w_g, norm_g, w_o, lb, ctx_out):
    B = h.shape[0]
    lbf = lb.astype(jnp.float32)

    def feats(t, with_q):
        T = t.shape[1]
        z = (jnp.einsum('btd,rdf->btrf', t, w_f) + b_f).astype(jnp.float32)
        log_f = jnp.log(lbf + (1.0 - lbf) * jax.nn.sigmoid(z))
        k = (1.0 - lbf) * jax.nn.sigmoid(-z)
        shp = (B, T, 2, HG_HEADS, HG_DK)
        v = (t @ w_i).astype(jnp.float32).reshape(B, T, HG_HEADS, HG_DV)
        q = jax.nn.silu((t @ w_q).astype(jnp.float32)).reshape(B, T, HG_HEADS, HG_DK) if with_q else None
        return q, k.reshape(shp), v, log_f.reshape(shp)

    s0 = jnp.zeros((B, HG_HEADS, HG_DK, HG_DV), jnp.float32)
    qc, kc, vc, lfc = feats(hc, ctx_out)
    q, k, v, lf = feats(h, True)
    oc_f, sc_f = gla_chunked(qc, kc[:, :, 0], vc, lfc[:, :, 0], s0, ctx_out)
    oc_b, sc_b = gla_chunked(flip(qc), flip(kc[:, :, 1]), flip(vc), flip(lfc[:, :, 1]), s0, ctx_out)
    o_f, _ = gla_chunked(q, k[:, :, 0], v, lf[:, :, 0], sc_f, True)
    o_b, _ = gla_chunked(flip(q), flip(k[:, :, 1]), flip(v), flip(lf[:, :, 1]), sc_b, True)

    def finish(t, o):
        T = t.shape[1]
        gate = jax.nn.sigmoid(t @ w_g).reshape(B, T, HG_HEADS, HG_DV)
        o = rmsnorm(o, norm_g).astype(t.dtype) * gate
        return o.reshape(B, T, -1) @ w_o

    y = finish(h, o_f + flip(o_b))
    yc = finish(hc, oc_f + flip(oc_b)) if ctx_out else None
    return y, yc


def grouped_moe(t, w_router, b_router, w_gate, w_up, w_down):
    n = t.shape[0]
    aff = jax.nn.sigmoid((t @ w_router).astype(jnp.float32))
    sel = (aff + b_router.astype(jnp.float32)).reshape(n, N_GROUPS, EXPERTS_PER_GROUP)
    group_score = jnp.sum(lax.top_k(sel, GROUP_SCORE_TOPK)[0], axis=-1)
    g_best = jnp.argmax(group_score, axis=-1)
    in_group = jnp.take_along_axis(sel, g_best[:, None, None], axis=1)[:, 0]
    _, local = lax.top_k(in_group, TOP_K)
    expert_idx = g_best[:, None] * EXPERTS_PER_GROUP + local
    w = jnp.take_along_axis(aff, expert_idx, axis=-1)
    w = w / jnp.sum(w, axis=-1, keepdims=True)
    combine = jnp.sum(jax.nn.one_hot(expert_idx, N_EXPERTS, dtype=jnp.float32) * w[..., None], axis=1).astype(t.dtype)
    out = None
    for e in range(N_EXPERTS):
        act = jax.nn.silu(t @ w_gate[e]) * (t @ w_up[e])
        contrib = combine[:, e:e + 1] * (act @ w_down[e])
        out = contrib if out is None else out + contrib
    return out


def setup_inputs(seed: int = 0) -> dict:
    key = jax.random.key(seed)
    ks = jax.random.split(key, 40)
    D = D_MODEL
    f32 = jnp.float32
    na, nb, nc, nd = [len(range(m, DEPTH, N_MIXERS)) for m in range(N_MIXERS)]

    def nrm(i, shape, scale):
        return jax.random.normal(ks[i], shape, f32) * scale

    def gain(i, shape):
        return 1.0 + 0.01 * jax.random.normal(ks[i], shape, f32)

    dt = jnp.exp(jax.random.uniform(ks[15], (nb, 2, GDN_HEADS), f32, math.log(1e-3), math.log(1e-1)))
    hgw = HG_HEADS * HG_DK
    return {
        'x': nrm(0, (BATCH, SEQ, D), 1.0),
        'c': nrm(1, (BATCH, D), 1.0),
        'ctx': nrm(2, (BATCH, CTX_LEN, D), 1.0),
        'c_ctx': nrm(3, (D,), 1.0),
        'w_mod': nrm(4, (DEPTH, D, 6 * D), 0.5 * D ** -0.5),
        'b_mod': nrm(5, (DEPTH, 6 * D), 0.01),
        'norm_mix_g': gain(6, (DEPTH, D)),
        'norm_ffn_g': gain(7, (DEPTH, D)),
        'da_w_qkv': nrm(8, (na, D, 3 * D), D ** -0.5),
        'da_w_o': nrm(9, (na, D, D), D ** -0.5),
        'da_lam': nrm(10, (na, 4, DA_HALF), 0.1),
        'da_subln_g': gain(11, (na, DA_VDIM)),
        'gdn_w_in': nrm(12, (nb, D, 2 * GDN_HEADS * GDN_DK + 2 * GDN_HEADS * GDN_DV), D ** -0.5),
        'gdn_conv': nrm(13, (nb, GDN_CONV, GDN_HEADS * (2 * GDN_DK + GDN_DV)), GDN_CONV ** -0.5),
        'gdn_w_ab': nrm(14, (nb, D, 4 * GDN_HEADS), D ** -0.5),
        'gdn_dt_bias': dt + jnp.log(-jnp.expm1(-dt)),
        'gdn_a_log': jnp.log(jax.random.uniform(ks[16], (nb, 2, GDN_HEADS), f32, 1.0, 16.0)),
        'gdn_norm_g': gain(17, (nb, GDN_DV)),
        'gdn_w_o': nrm(18, (nb, GDN_HEADS * GDN_DV, D), (GDN_HEADS * GDN_DV) ** -0.5),
        'na_w_qkv': nrm(19, (nc, D, 3 * NA_HEADS * NA_DIM), D ** -0.5),
        'na_rpb': nrm(20, (nc, NA_HEADS, 2 * WIN_R - 1, 2 * WIN_C - 1), 0.1),
        'na_w_o': nrm(21, (nc, NA_HEADS * NA_DIM, D), (NA_HEADS * NA_DIM) ** -0.5),
        'hg_w_q': nrm(22, (nd, D, hgw), D ** -0.5),
        'hg_w_i': nrm(23, (nd, D, HG_HEADS * HG_DV), D ** -0.5),
        'hg_w_f': nrm(24, (nd, 2, D, hgw), D ** -0.5),
        'hg_b_f': nrm(25, (nd, 2, hgw), 0.1),
        'hg_w_g': nrm(26, (nd, D, HG_HEADS * HG_DV), D ** -0.5),
        'hg_norm_g': gain(27, (nd, HG_DV)),
        'hg_w_o': nrm(28, (nd, HG_HEADS * HG_DV, D), (HG_HEADS * HG_DV) ** -0.5),
        'hg_lb_logits': nrm(29, (DEPTH, hgw), 0.1),
        'w_router': nrm(30, (D, N_EXPERTS), D ** -0.5),
        'b_router': nrm(31, (N_EXPERTS,), 0.01),
        'e_w_gate': nrm(32, (DEPTH, N_EXPERTS, D, D_EXPERT), D ** -0.5),
        'e_w_up': nrm(33, (DEPTH, N_EXPERTS, D, D_EXPERT), D ** -0.5),
        'e_w_down': nrm(34, (DEPTH, N_EXPERTS, D_EXPERT, D), D_EXPERT ** -0.5),
        'final_norm_g': gain(35, (D,)),
    }


def reference(x, c, ctx, c_ctx, w_mod, b_mod, norm_mix_g, norm_ffn_g,
              da_w_qkv, da_w_o, da_lam, da_subln_g,
              gdn_w_in, gdn_conv, gdn_w_ab, gdn_dt_bias, gdn_a_log, gdn_norm_g, gdn_w_o,
              na_w_qkv, na_rpb, na_w_o,
              hg_w_q, hg_w_i, hg_w_f, hg_b_f, hg_w_g, hg_norm_g, hg_w_o, hg_lb_logits,
              w_router, b_router, e_w_gate, e_w_up, e_w_down, final_norm_g):
    B, L, D = x.shape
    tc = ctx.shape[1]
    xc = ctx
    cond = jax.nn.silu(c)
    cond_ctx = jax.nn.silu(c_ctx)
    p_lb = jax.nn.softmax(hg_lb_logits.astype(jnp.float32), axis=0)
    lb_all = jnp.cumsum(p_lb, axis=0) - p_lb[0]
    for i in range(DEPTH):
        last = i == DEPTH - 1
        kind, j = i % N_MIXERS, i // N_MIXERS
        sh1, sc1, g1, sh2, sc2, g2 = jnp.split((cond @ w_mod[i] + b_mod[i])[:, None, :], 6, axis=-1)
        sh1c, sc1c, g1c, sh2c, sc2c, g2c = jnp.split(cond_ctx @ w_mod[i] + b_mod[i], 6, axis=-1)
        h = rmsnorm(x, norm_mix_g[i]) * (1.0 + sc1) + sh1
        hc = rmsnorm(xc, norm_mix_g[i]) * (1.0 + sc1c) + sh1c
        if kind == 0:
            y, yc = diff_attention(h, hc, da_w_qkv[j], da_w_o[j], da_lam[j], da_subln_g[j], i, not last)
        elif kind == 1:
            y, yc = gated_deltanet(h, hc, gdn_w_in[j], gdn_conv[j], gdn_w_ab[j], gdn_dt_bias[j],
                                   gdn_a_log[j], gdn_norm_g[j], gdn_w_o[j], not last)
        elif kind == 2:
            y, yc = neighbourhood_attention(h, hc, na_w_qkv[j], na_rpb[j], na_w_o[j], not last)
        else:
            y, yc = hgrn2(h, hc, hg_w_q[j], hg_w_i[j], hg_w_f[j], hg_b_f[j], hg_w_g[j],
                          hg_norm_g[j], hg_w_o[j], lb_all[i], not last)
        x = x + g1 * y
        h2 = rmsnorm(x, norm_ffn_g[i]) * (1.0 + sc2) + sh2
        if last:
            f = grouped_moe(h2.reshape(B * L, D), w_router, b_router, e_w_gate[i], e_w_up[i], e_w_down[i])
            x = x + g2 * f.reshape(B, L, D)
        else:
            xc = xc + g1c * yc
            h2c = rmsnorm(xc, norm_ffn_g[i]) * (1.0 + sc2c) + sh2c
            toks = jnp.concatenate([h2c, h2], axis=1).reshape(-1, D)
            f = grouped_moe(toks, w_router, b_router, e_w_gate[i], e_w_up[i], e_w_down[i]).reshape(B, tc + L, D)
            xc = xc + g2c * f[:, :tc]
            x = x + g2 * f[:, tc:]
    return rmsnorm(x, final_norm_g)
```

```python
import functools
import math

import jax
import jax.numpy as jnp
from jax import lax
from jax.experimental import pallas as pl
from jax.experimental.pallas import tpu as pltpu

F32 = jnp.float32
BF16 = jnp.bfloat16

D_MODEL = 2048
DEPTH = 4
GRID_W = 64
N_MIXERS = 4
EPS = 1e-6
DA_HEADS = 16
DA_HALF = D_MODEL // DA_HEADS // 2
DA_VDIM = 2 * DA_HALF
ROPE_BASE = 10000.0
GDN_HEADS = 16
GDN_DK = D_MODEL // GDN_HEADS
GDN_DV = D_MODEL // GDN_HEADS
GDN_CHUNK = 64
NA_HEADS = 16
NA_DIM = D_MODEL // NA_HEADS
WIN_R = 8
WIN_C = 16
HG_HEADS = 16
HG_DK = D_MODEL // HG_HEADS
HG_DV = D_MODEL // HG_HEADS
HG_CHUNK = 32
N_EXPERTS = 16
N_GROUPS = 4
EXPERTS_PER_GROUP = N_EXPERTS // N_GROUPS
GROUP_SCORE_TOPK = 2
TOP_K = 2
D_EXPERT = D_MODEL // 2

V7X_VMEM_LIMIT_BYTES = 56 * 1024 * 1024
LANES = 128
MOE_TILE = 512
ROUTER_PAD = LANES


def _largest_divisor(n, candidates):
    for c in candidates:
        if n % c == 0:
            return c
    raise ValueError(f"no tile in {candidates} divides {n}")


def _cparams(*sem):
    return pltpu.CompilerParams(dimension_semantics=sem, vmem_limit_bytes=V7X_VMEM_LIMIT_BYTES)


def _mm_kernel(a_ref, w_ref, o_ref):
    o_ref[...] = jnp.dot(a_ref[...], w_ref[...], preferred_element_type=F32).astype(o_ref.dtype)


def mm(a, w, out_dtype=F32):
    m, k = a.shape
    n = w.shape[1]
    n_pad = -n % LANES
    if n_pad:
        w = jnp.pad(w, ((0, 0), (0, n_pad)))
    a = a.astype(BF16)
    w = w.astype(BF16)
    np_ = n + n_pad
    tm = _largest_divisor(m, (1024, 512, 256, 128))
    tn = _largest_divisor(np_, (512, 256, 128))
    out = pl.pallas_call(
        _mm_kernel,
        out_shape=jax.ShapeDtypeStruct((m, np_), out_dtype),
        grid=(m // tm, np_ // tn),
        in_specs=[pl.BlockSpec((tm, k), lambda i, j: (i, 0)),
                  pl.BlockSpec((k, tn), lambda i, j: (0, j))],
        out_specs=pl.BlockSpec((tm, tn), lambda i, j: (i, j)),
        compiler_params=_cparams("parallel", "arbitrary"),
        name="proj_mm",
    )(a, w)
    return out[:, :n] if n_pad else out


def _normmod_kernel(x_ref, g_ref, sc_ref, sh_ref, o_ref):
    x = x_ref[0].astype(F32)
    y = x * lax.rsqrt(jnp.mean(x * x, axis=-1, keepdims=True) + EPS)
    o_ref[0] = ((y * g_ref[...]) * (1.0 + sc_ref[0]) + sh_ref[0]).astype(o_ref.dtype)


def _normmod_router_kernel(x_ref, g_ref, sc_ref, sh_ref, wr_ref, o_ref, aff_ref):
    x = x_ref[0].astype(F32)
    y = x * lax.rsqrt(jnp.mean(x * x, axis=-1, keepdims=True) + EPS)
    h = (y * g_ref[...]) * (1.0 + sc_ref[0]) + sh_ref[0]
    o_ref[0] = h.astype(o_ref.dtype)
    logits = jnp.dot(h, wr_ref[...], preferred_element_type=F32, precision=lax.Precision.HIGHEST)
    aff_ref[0] = jax.nn.sigmoid(logits)


def norm_modulate(x, g, sc, sh, w_router=None):
    b, t, d = x.shape
    tm = _largest_divisor(t, (512, 256, 128))
    x_spec = pl.BlockSpec((1, tm, d), lambda i, j: (i, j, 0))
    g_spec = pl.BlockSpec((1, d), lambda i, j: (0, 0))
    mod_spec = pl.BlockSpec((1, 1, d), lambda i, j: (i, 0, 0))
    g2 = g.reshape(1, d).astype(F32)
    if w_router is None:
        return pl.pallas_call(
            _normmod_kernel,
            out_shape=jax.ShapeDtypeStruct((b, t, d), BF16),
            grid=(b, t // tm),
            in_specs=[x_spec, g_spec, mod_spec, mod_spec],
            out_specs=x_spec,
            compiler_params=_cparams("parallel", "parallel"),
            name="norm_modulate",
        )(x, g2, sc, sh)
    wr = jnp.pad(w_router.astype(F32), ((0, 0), (0, ROUTER_PAD - w_router.shape[1])))
    return pl.pallas_call(
        _normmod_router_kernel,
        out_shape=(jax.ShapeDtypeStruct((b, t, d), BF16),
                   jax.ShapeDtypeStruct((b, t, ROUTER_PAD), F32)),
        grid=(b, t // tm),
        in_specs=[x_spec, g_spec, mod_spec, mod_spec,
                  pl.BlockSpec((d, ROUTER_PAD), lambda i, j: (0, 0))],
        out_specs=(x_spec, pl.BlockSpec((1, tm, ROUTER_PAD), lambda i, j: (i, j, 0))),
        compiler_params=_cparams("parallel", "parallel"),
        name="norm_modulate_router",
    )(x, g2, sc, sh, wr)


def _rmsnorm_kernel(x_ref, g_ref, o_ref):
    x = x_ref[0].astype(F32)
    y = x * lax.rsqrt(jnp.mean(x * x, axis=-1, keepdims=True) + EPS)
    o_ref[0] = y * g_ref[...]


def final_rmsnorm(x, g):
    b, t, d = x.shape
    tm = _largest_divisor(t, (512, 256, 128))
    spec = pl.BlockSpec((1, tm, d), lambda i, j: (i, j, 0))
    return pl.pallas_call(
        _rmsnorm_kernel,
        out_shape=jax.ShapeDtypeStruct((b, t, d), F32),
        grid=(b, t // tm),
        in_specs=[spec, pl.BlockSpec((1, d), lambda i, j: (0, 0))],
        out_specs=spec,
        compiler_params=_cparams("parallel", "parallel"),
        name="final_rmsnorm",
    )(x, g.reshape(1, d).astype(F32))


def _moe_kernel(te_ref, tv_ref, x_ref, wg_ref, wu_ref, wd_ref, o_ref):
    del te_ref
    i = pl.program_id(0)

    @pl.when(tv_ref[i] > 0)
    def _():
        x = x_ref[...]
        g = jnp.dot(x, wg_ref[0], preferred_element_type=F32)
        u = jnp.dot(x, wu_ref[0], preferred_element_type=F32)
        act = (g * jax.nn.sigmoid(g) * u).astype(BF16)
        o_ref[...] = jnp.dot(act, wd_ref[0], preferred_element_type=F32)

    @pl.when(tv_ref[i] == 0)
    def _():
        o_ref[...] = jnp.zeros_like(o_ref)


def _route(aff, b_router):
    n = aff.shape[0]
    sel = (aff + b_router.astype(F32)).reshape(n, N_GROUPS, EXPERTS_PER_GROUP)
    group_score = jnp.sum(lax.top_k(sel, GROUP_SCORE_TOPK)[0], axis=-1)
    g_best = jnp.argmax(group_score, axis=-1)
    in_group = jnp.take_along_axis(sel, g_best[:, None, None], axis=1)[:, 0]
    _, local = lax.top_k(in_group, TOP_K)
    expert_idx = g_best[:, None] * EXPERTS_PER_GROUP + local
    w = jnp.take_along_axis(aff, expert_idx, axis=-1)
    w = w / jnp.sum(w, axis=-1, keepdims=True)
    return expert_idx.astype(jnp.int32), w


def grouped_moe(h2, aff, b_router, w_gate, w_up, w_down):
    n, d = h2.shape
    expert_idx, w = _route(aff, b_router)
    n_slots = TOP_K * n
    e_flat = expert_idx.reshape(-1)
    onehot = (e_flat[:, None] == jnp.arange(N_EXPERTS, dtype=jnp.int32)[None, :]).astype(jnp.int32)
    csum = jnp.cumsum(onehot, axis=0)
    rank = jnp.take_along_axis(csum, e_flat[:, None], axis=1)[:, 0] - 1
    counts = csum[-1]
    padded = ((counts + MOE_TILE - 1) // MOE_TILE) * MOE_TILE
    pad_end = jnp.cumsum(padded)
    pad_off = pad_end - padded
    cnt_off = jnp.cumsum(counts) - counts
    dest = (pad_off[e_flat] + rank).reshape(n, TOP_K)

    n_tiles = -(-n_slots // MOE_TILE) + N_EXPERTS
    n_pad_slots = n_tiles * MOE_TILE
    tile_start = jnp.arange(n_tiles, dtype=jnp.int32) * MOE_TILE
    tile_expert = jnp.minimum(jnp.searchsorted(pad_end, tile_start, side="right"), N_EXPERTS - 1).astype(jnp.int32)
    tile_valid = (tile_start < pad_end[-1]).astype(jnp.int32)

    order = jnp.argsort(e_flat, stable=True).astype(jnp.int32)
    slot = jnp.arange(n_pad_slots, dtype=jnp.int32)
    slot_e = jnp.repeat(tile_expert, MOE_TILE)
    r = slot - pad_off[slot_e]
    slot_ok = (r < counts[slot_e]) & (jnp.repeat(tile_valid, MOE_TILE) > 0)
    src = order[jnp.clip(cnt_off[slot_e] + r, 0, n_slots - 1)] // TOP_K
    slot_token = jnp.where(slot_ok, src, 0)

    xs = jnp.take(h2, slot_token, axis=0)
    f = D_EXPERT
    ys = pl.pallas_call(
        _moe_kernel,
        out_shape=jax.ShapeDtypeStruct((n_pad_slots, d), F32),
        grid_spec=pltpu.PrefetchScalarGridSpec(
            num_scalar_prefetch=2,
            grid=(n_tiles,),
            in_specs=[pl.BlockSpec((MOE_TILE, d), lambda i, te, tv: (i, 0)),
                      pl.BlockSpec((1, d, f), lambda i, te, tv: (te[i], 0, 0)),
                      pl.BlockSpec((1, d, f), lambda i, te, tv: (te[i], 0, 0)),
                      pl.BlockSpec((1, f, d), lambda i, te, tv: (te[i], 0, 0))],
            out_specs=pl.BlockSpec((MOE_TILE, d), lambda i, te, tv: (i, 0)),
        ),
        compiler_params=_cparams("arbitrary"),
        name="moe_experts",
    )(tile_expert, tile_valid, xs, w_gate.astype(BF16), w_up.astype(BF16), w_down.astype(BF16))
    out = jnp.take(ys, dest[:, 0], axis=0) * w[:, 0:1] + jnp.take(ys, dest[:, 1], axis=0) * w[:, 1:2]
    return out


def _diff_attn_kernel(lam_ref, q_ref, k_ref, v_ref, g_ref, o_ref, *, n_ctx, lam_init):
    i = pl.program_id(2)
    lam = lam_ref[0]
    lane = lax.broadcasted_iota(jnp.int32, (1, 2 * DA_HALF), 1)
    q = q_ref[0]
    q1 = jnp.where(lane < DA_HALF, q, jnp.zeros_like(q))
    q2 = jnp.where(lane >= DA_HALF, q, jnp.zeros_like(q))
    nt = (((1,), (1,)), ((), ()))

    def attend(k, v):
        def softmax(s):
            m = jnp.max(s, axis=-1, keepdims=True)
            p = jnp.exp(s - m)
            return p * (1.0 / jnp.sum(p, axis=-1, keepdims=True))

        p1 = softmax(lax.dot_general(q1, k, nt, preferred_element_type=F32))
        p2 = softmax(lax.dot_general(q2, k, nt, preferred_element_type=F32))
        p = (p1 - lam * p2).astype(BF16)
        o = jnp.dot(p, v, preferred_element_type=F32)
        o = o * lax.rsqrt(jnp.mean(o * o, axis=-1, keepdims=True) + 1e-5)
        o_ref[0] = ((o * g_ref[...]) * (1.0 - lam_init)).astype(o_ref.dtype)

    @pl.when(i == 0)
    def _():
        attend(k_ref[0, :n_ctx, :], v_ref[0, :n_ctx, :])

    @pl.when(i > 0)
    def _():
        attend(k_ref[0], v_ref[0])


def diff_attn_core(qkv, lam, subln_g, n_ctx, lam_init):
    b, t, _ = qkv.shape
    tq = n_ctx
    nh = DA_HEADS
    hw = 2 * DA_HALF
    kern = functools.partial(_diff_attn_kernel, n_ctx=n_ctx, lam_init=lam_init)
    return pl.pallas_call(
        kern,
        out_shape=jax.ShapeDtypeStruct((b, t, nh * hw), BF16),
        grid=(b, nh, t // tq),
        in_specs=[pl.BlockSpec(memory_space=pltpu.SMEM),
                  pl.BlockSpec((1, tq, hw), lambda bi, h, i: (bi, i, h)),
                  pl.BlockSpec((1, t, hw), lambda bi, h, i: (bi, 0, nh + h)),
                  pl.BlockSpec((1, t, hw), lambda bi, h, i: (bi, 0, 2 * nh + h)),
                  pl.BlockSpec((1, hw), lambda bi, h, i: (0, 0))],
        out_specs=pl.BlockSpec((1, tq, hw), lambda bi, h, i: (bi, i, h)),
        compiler_params=_cparams("parallel", "parallel", "arbitrary"),
        name="diff_attn",
    )(lam.reshape(1).astype(F32), qkv, qkv, qkv, subln_g.reshape(1, hw).astype(F32))


def rmsnorm(x, g, eps=EPS):
    xf = x.astype(F32)
    y = xf * lax.rsqrt(jnp.mean(xf * xf, axis=-1, keepdims=True) + eps)
    return y.astype(x.dtype) * g


def l2norm(x):
    return x * lax.rsqrt(jnp.sum(x * x, axis=-1, keepdims=True) + EPS)


def flip(t):
    return None if t is None else jnp.flip(t, axis=1)


def proj(t, w, out_dtype=F32):
    b, tt, k = t.shape
    return mm(t.reshape(b * tt, k), w, out_dtype).reshape(b, tt, w.shape[1])


def axial_rope_tables(length, dim):
    n_freq = dim // 4
    t = jnp.arange(length)
    pos = jnp.stack([t // GRID_W, t % GRID_W], axis=-1).astype(F32)
    inv = ROPE_BASE ** (-jnp.arange(n_freq, dtype=F32) / n_freq)
    ang = pos[:, :, None] * inv
    return jnp.cos(ang), jnp.sin(ang)


def apply_rope(x, cos, sin):
    n_freq = cos.shape[-1]
    xr = x.reshape(*x.shape[:-1], 2, 2, n_freq)
    x1, x2 = xr[..., 0, :], xr[..., 1, :]
    c = cos[:, None].astype(x.dtype)
    s = sin[:, None].astype(x.dtype)
    out = jnp.stack([x1 * c - x2 * s, x1 * s + x2 * c], axis=-2)
    return out.reshape(x.shape)


def diff_attention(h, hc, w_qkv, w_o, lam_vec, subln_g, layer_idx):
    b, l, d = h.shape
    tc = hc.shape[1]
    dq = 2 * DA_HEADS * DA_HALF
    lam_init = 0.8 - 0.6 * math.exp(-0.3 * layer_idx)
    lv = lam_vec.astype(F32)
    lam = jnp.exp(jnp.sum(lv[0] * lv[1])) - jnp.exp(jnp.sum(lv[2] * lv[3])) + lam_init
    scale = DA_HALF ** -0.5

    qkv = proj(h, w_qkv)
    qkv_c = proj(hc, w_qkv)
    cos, sin = axial_rope_tables(l, DA_HALF)
    q = apply_rope(qkv[..., :dq].reshape(b, l, 2 * DA_HEADS, DA_HALF), cos, sin).reshape(b, l, dq) * scale
    k = apply_rope(qkv[..., dq:2 * dq].reshape(b, l, 2 * DA_HEADS, DA_HALF), cos, sin).reshape(b, l, dq)
    lat = jnp.concatenate([q, k, qkv[..., 2 * dq:]], axis=-1).astype(BF16)
    ctx = jnp.concatenate([qkv_c[..., :dq] * scale, qkv_c[..., dq:]], axis=-1).astype(BF16)
    o = diff_attn_core(jnp.concatenate([ctx, lat], axis=1), lam, subln_g, tc, lam_init)
    y = proj(o, w_o)
    return y[:, tc:], y[:, :tc]


def centred_dwconv(x, w):
    k = w.shape[0]
    pad = k // 2
    t = x.shape[1]
    xp = jnp.pad(x, ((0, 0), (pad, pad), (0, 0)))
    out = xp[:, 0:t] * w[0]
    for j in range(1, k):
        out = out + xp[:, j:j + t] * w[j]
    return out


def gated_delta_chunked(q, k, v, log_a, beta, s0, with_output):
    B, T, H, DK = k.shape
    C = GDN_CHUNK
    n = T // C

    def chunks(t):
        return t.reshape(B, n, C, H, -1).transpose(1, 0, 3, 2, 4)

    def chunks_s(t):
        return t.reshape(B, n, C, H).transpose(1, 0, 3, 2)

    kc, vc = chunks(k), chunks(v)
    bc = chunks_s(beta)[..., None]
    g = jnp.cumsum(chunks_s(log_a), axis=-1)
    idx = jnp.arange(C)
    incl = idx[:, None] >= idx[None, :]
    strict = idx[:, None] > idx[None, :]
    decay = jnp.exp(jnp.where(incl, g[..., :, None] - g[..., None, :], -jnp.inf))
    kb = kc * bc
    a_mat = jnp.where(strict, jnp.einsum('nbhid,nbhjd->nbhij', kb, kc) * decay, 0.0)
    eye = jnp.eye(C, dtype=F32)
    t_mat = lax.linalg.triangular_solve(eye + a_mat, jnp.broadcast_to(eye, a_mat.shape),
                                        left_side=True, lower=True)
    u = t_mat @ (vc * bc)
    w = t_mat @ (kb * jnp.exp(g)[..., None])
    g_last = g[..., -1:]
    k_dec = kc * jnp.exp(g_last - g)[..., None]
    s_dec = jnp.exp(g_last)[..., None]
    if not with_output:
        def step_state(s, xs):
            u_i, w_i, kd_i, sd_i = xs
            v_new = u_i - w_i @ s
            return s * sd_i + jnp.swapaxes(kd_i, -1, -2) @ v_new, None
        s, _ = lax.scan(step_state, s0, (u, w, k_dec, s_dec))
        return None, s
    qc = chunks(q) * DK ** -0.5
    q_dec = qc * jnp.exp(g)[..., None]
    qk = jnp.where(incl, jnp.einsum('nbhid,nbhjd->nbhij', qc, kc) * decay, 0.0)

    def step(s, xs):
        u_i, w_i, kd_i, sd_i, qd_i, qk_i = xs
        v_new = u_i - w_i @ s
        o = qd_i @ s + qk_i @ v_new
        return s * sd_i + jnp.swapaxes(kd_i, -1, -2) @ v_new, o

    s, o = lax.scan(step, s0, (u, w, k_dec, s_dec, q_dec, qk))
    return o.transpose(1, 0, 3, 2, 4).reshape(B, T, H, -1), s


def gdn_features(t, w_in, conv_w, w_ab, dt_bias, a_log, with_q):
    B, T, _ = t.shape
    wq = GDN_HEADS * GDN_DK
    wv = GDN_HEADS * GDN_DV
    lo = 0 if with_q else wq
    hi = 2 * wq + wv
    z = jax.nn.silu(centred_dwconv(proj(t, w_in[:, lo:hi]), conv_w[:, lo:hi])).astype(F32)
    if with_q:
        q = l2norm(z[..., :wq].reshape(B, T, GDN_HEADS, GDN_DK))
        z = z[..., wq:]
    else:
        q = None
    k = l2norm(z[..., :wq].reshape(B, T, GDN_HEADS, GDN_DK))
    v = z[..., wq:].reshape(B, T, GDN_HEADS, GDN_DV)
    ab = proj(t, w_ab).astype(F32).reshape(B, T, 2, 2, GDN_HEADS)
    log_a = -jnp.exp(a_log.astype(F32)) * jax.nn.softplus(ab[:, :, 0] + dt_bias.astype(F32))
    beta = jax.nn.sigmoid(ab[:, :, 1])
    return q, k, v, log_a, beta


def gated_deltanet(h, hc, w_in, conv_w, w_ab, dt_bias, a_log, norm_g, w_o, ctx_out):
    B = h.shape[0]
    gate_lo = 2 * GDN_HEADS * GDN_DK + GDN_HEADS * GDN_DV
    s0 = jnp.zeros((B, GDN_HEADS, GDN_DK, GDN_DV), F32)
    qc, kc, vc, lac, bc = gdn_features(hc, w_in, conv_w, w_ab, dt_bias, a_log, ctx_out)
    q, k, v, la, be = gdn_features(h, w_in, conv_w, w_ab, dt_bias, a_log, True)
    oc_f, sc_f = gated_delta_chunked(qc, kc, vc, lac[:, :, 0], bc[:, :, 0], s0, ctx_out)
    o_f, _ = gated_delta_chunked(q, k, v, la[:, :, 0], be[:, :, 0], sc_f, True)
    oc_b, sc_b = gated_delta_chunked(flip(qc), flip(kc), flip(vc), flip(lac[:, :, 1]), flip(bc[:, :, 1]), s0, ctx_out)
    o_b, _ = gated_delta_chunked(flip(q), flip(k), flip(v), flip(la[:, :, 1]), flip(be[:, :, 1]), sc_b, True)

    def finish(t, o):
        T = t.shape[1]
        gate = jax.nn.silu(proj(t, w_in[:, gate_lo:])).reshape(B, T, GDN_HEADS, GDN_DV)
        o = rmsnorm(o, norm_g).astype(F32) * gate
        return proj(o.reshape(B, T, -1), w_o)

    y = finish(h, o_f + flip(o_b))
    yc = finish(hc, oc_f + flip(oc_b)) if ctx_out else None
    return y, yc


def neighbourhood_attention(h, hc, w_qkv, rpb, w_o, ctx_out):
    B, L, _ = h.shape
    tc = hc.shape[1]
    rows = L // GRID_W
    kr = min(WIN_R, rows)
    hd = NA_HEADS * NA_DIM
    scale = NA_DIM ** -0.5
    qkv = proj(h, w_qkv).reshape(B, L, 3, NA_HEADS, NA_DIM)
    q, k, v = qkv[:, :, 0], qkv[:, :, 1], qkv[:, :, 2]
    if ctx_out:
        qkv_c = proj(hc, w_qkv).reshape(B, tc, 3, NA_HEADS, NA_DIM)
        qc, kc, vc = qkv_c[:, :, 0], qkv_c[:, :, 1], qkv_c[:, :, 2]
    else:
        kv_c = proj(hc, w_qkv[:, hd:]).reshape(B, tc, 2, NA_HEADS, NA_DIM)
        kc, vc = kv_c[:, :, 0], kv_c[:, :, 1]
    k_g = k.reshape(B, rows, GRID_W, NA_HEADS, NA_DIM)
    v_g = v.reshape(B, rows, GRID_W, NA_HEADS, NA_DIM)
    col = jnp.arange(GRID_W)
    c0 = jnp.clip(col - WIN_C // 2, 0, GRID_W - WIN_C)
    col_mask = (col[None, :] >= c0[:, None]) & (col[None, :] < c0[:, None] + WIN_C)
    dc_idx = jnp.clip(col[None, :] - col[:, None], -(WIN_C - 1), WIN_C - 1) + (WIN_C - 1)

    def row_block(args):
        r, q_r = args
        r0 = jnp.clip(r - kr // 2, 0, rows - kr)
        k_b = lax.dynamic_slice_in_dim(k_g, r0, kr, axis=1)
        v_b = lax.dynamic_slice_in_dim(v_g, r0, kr, axis=1)
        dr_idx = r0 + jnp.arange(kr) - r + (WIN_R - 1)
        bias = rpb[:, dr_idx[None, :, None], dc_idx[:, None, :]].astype(F32)
        s_win = jnp.einsum('bqhd,brwhd->bhqrw', q_r, k_b).astype(F32) * scale + bias
        s_win = jnp.where(col_mask[:, None, :], s_win, -jnp.inf).reshape(B, NA_HEADS, GRID_W, kr * GRID_W)
        s_ctx = jnp.einsum('bqhd,bkhd->bhqk', q_r, kc).astype(F32) * scale
        p = jax.nn.softmax(jnp.concatenate([s_win, s_ctx], axis=-1), axis=-1).astype(v.dtype)
        p_win = p[..., :kr * GRID_W].reshape(B, NA_HEADS, GRID_W, kr, GRID_W)
        return (jnp.einsum('bhqrw,brwhd->bqhd', p_win, v_b)
                + jnp.einsum('bhqk,bkhd->bqhd', p[..., kr * GRID_W:], vc))

    o = lax.map(row_block, (jnp.arange(rows), q.reshape(B, rows, GRID_W, NA_HEADS, NA_DIM).swapaxes(0, 1)))
    y = proj(o.swapaxes(0, 1).reshape(B, L, hd), w_o)
    if ctx_out:
        s = jnp.einsum('bqhd,bkhd->bhqk', qc, kc).astype(F32) * scale
        p = jax.nn.softmax(s, axis=-1).astype(vc.dtype)
        yc = proj(jnp.einsum('bhqk,bkhd->bqhd', p, vc).reshape(B, tc, hd), w_o)
    else:
        yc = None
    return y, yc


def gla_chunked(q, k, v, log_f, s0, with_output):
    B, T, H, DK = k.shape
    C = HG_CHUNK
    n = T // C

    def chunks(t):
        return t.reshape(B, n, C, H, -1).transpose(1, 0, 3, 2, 4)

    kc, vc = chunks(k), chunks(v)
    b = jnp.cumsum(chunks(log_f), axis=-2)
    b_last = b[..., -1:, :]
    k_dec = kc * jnp.exp(b_last - b)
    s_dec = jnp.swapaxes(jnp.exp(b_last), -1, -2)
    if not with_output:
        def step_state(s, xs):
            kd_i, v_i, sd_i = xs
            return s * sd_i + jnp.swapaxes(kd_i, -1, -2) @ v_i, None
        s, _ = lax.scan(step_state, s0, (k_dec, vc, s_dec))
        return None, s
    qc = chunks(q) * DK ** -0.5
    idx = jnp.arange(C)
    incl = (idx[:, None] >= idx[None, :])[:, :, None]

    def step(s, xs):
        q_i, k_i, v_i, b_i, kd_i, sd_i = xs
        rel = jnp.exp(jnp.where(incl, b_i[..., :, None, :] - b_i[..., None, :, :], -jnp.inf))
        att = jnp.sum(q_i[..., :, None, :] * k_i[..., None, :, :] * rel, axis=-1)
        o = (q_i * jnp.exp(b_i)) @ s + att @ v_i
        return s * sd_i + jnp.swapaxes(kd_i, -1, -2) @ v_i, o

    s, o = lax.scan(step, s0, (qc, kc, vc, b, k_dec, s_dec))
    return o.transpose(1, 0, 3, 2, 4).reshape(B, T, H, -1), s


def hgrn2(h, hc, w_q, w_i, w_f, b_f, w_g, norm_g, w_o, lb, ctx_out):
    B = h.shape[0]
    lbf = lb.astype(F32)

    def feats(t, with_q):
        T = t.shape[1]
        z = (jnp.stack([proj(t, w_f[0]), proj(t, w_f[1])], axis=2) + b_f).astype(F32)
        log_f = jnp.log(lbf + (1.0 - lbf) * jax.nn.sigmoid(z))
        k = (1.0 - lbf) * jax.nn.sigmoid(-z)
        shp = (B, T, 2, HG_HEADS, HG_DK)
        v = proj(t, w_i).astype(F32).reshape(B, T, HG_HEADS, HG_DV)
        q = jax.nn.silu(proj(t, w_q).astype(F32)).reshape(B, T, HG_HEADS, HG_DK) if with_q else None
        return q, k.reshape(shp), v, log_f.reshape(shp)

    s0 = jnp.zeros((B, HG_HEADS, HG_DK, HG_DV), F32)
    qc, kc, vc, lfc = feats(hc, ctx_out)
    q, k, v, lf = feats(h, True)
    oc_f, sc_f = gla_chunked(qc, kc[:, :, 0], vc, lfc[:, :, 0], s0, ctx_out)
    oc_b, sc_b = gla_chunked(flip(qc), flip(kc[:, :, 1]), flip(vc), flip(lfc[:, :, 1]), s0, ctx_out)
    o_f, _ = gla_chunked(q, k[:, :, 0], v, lf[:, :, 0], sc_f, True)
    o_b, _ = gla_chunked(flip(q), flip(k[:, :, 1]), flip(v), flip(lf[:, :, 1]), sc_b, True)

    def finish(t, o):
        T = t.shape[1]
        gate = jax.nn.sigmoid(proj(t, w_g)).reshape(B, T, HG_HEADS, HG_DV)
        o = rmsnorm(o, norm_g).astype(F32) * gate
        return proj(o.reshape(B, T, -1), w_o)

    y = finish(h, o_f + flip(o_b))
    yc = finish(hc, oc_f + flip(oc_b)) if ctx_out else None
    return y, yc


def kernel(x, c, ctx, c_ctx, w_mod, b_mod, norm_mix_g, norm_ffn_g, da_w_qkv, da_w_o, da_lam, da_subln_g, gdn_w_in, gdn_conv, gdn_w_ab, gdn_dt_bias, gdn_a_log, gdn_norm_g, gdn_w_o, na_w_qkv, na_rpb, na_w_o, hg_w_q, hg_w_i, hg_w_f, hg_b_f, hg_w_g, hg_norm_g, hg_w_o, hg_lb_logits, w_router, b_router, e_w_gate, e_w_up, e_w_down, final_norm_g):
    B, L, D = x.shape
    tc = ctx.shape[1]
    xc = ctx
    cond = jax.nn.silu(c)
    cond_ctx = jax.nn.silu(c_ctx)
    p_lb = jax.nn.softmax(hg_lb_logits.astype(F32), axis=0)
    lb_all = jnp.cumsum(p_lb, axis=0) - p_lb[0]
    cond_all = jnp.concatenate([cond, cond_ctx[None]], axis=0)
    for i in range(DEPTH):
        last = i == DEPTH - 1
        kind, j = i % N_MIXERS, i // N_MIXERS
        mod = mm(jnp.pad(cond_all, ((0, -(B + 1) % 128), (0, 0))), w_mod[i])[:B + 1] + b_mod[i]
        sh1, sc1, g1, sh2, sc2, g2 = jnp.split(mod[:B, None, :], 6, axis=-1)
        sh1c, sc1c, g1c, sh2c, sc2c, g2c = jnp.split(mod[B], 6, axis=-1)

        def bc_ctx(v):
            return jnp.broadcast_to(v[None, None, :], (B, 1, D))

        h = norm_modulate(x, norm_mix_g[i], sc1, sh1)
        hc = norm_modulate(xc, norm_mix_g[i], bc_ctx(sc1c), bc_ctx(sh1c))
        if kind == 0:
            y, yc = diff_attention(h, hc, da_w_qkv[j], da_w_o[j], da_lam[j], da_subln_g[j], i)
        elif kind == 1:
            y, yc = gated_deltanet(h, hc, gdn_w_in[j], gdn_conv[j], gdn_w_ab[j], gdn_dt_bias[j],
                                   gdn_a_log[j], gdn_norm_g[j], gdn_w_o[j], not last)
        elif kind == 2:
            y, yc = neighbourhood_attention(h, hc, na_w_qkv[j], na_rpb[j], na_w_o[j], not last)
        else:
            y, yc = hgrn2(h, hc, hg_w_q[j], hg_w_i[j], hg_w_f[j], hg_b_f[j], hg_w_g[j],
                          hg_norm_g[j], hg_w_o[j], lb_all[i], not last)
        x = x + g1 * y
        h2, aff = norm_modulate(x, norm_ffn_g[i], sc2, sh2, w_router)
        if last:
            f = grouped_moe(h2.reshape(B * L, D), aff.reshape(B * L, ROUTER_PAD)[:, :N_EXPERTS],
                            b_router, e_w_gate[i], e_w_up[i], e_w_down[i])
            x = x + g2 * f.reshape(B, L, D)
        else:
            xc = xc + g1c * yc
            h2c, affc = norm_modulate(xc, norm_ffn_g[i], bc_ctx(sc2c), bc_ctx(sh2c), w_router)
            toks = jnp.concatenate([h2c, h2], axis=1).reshape(-1, D)
            affs = jnp.concatenate([affc, aff], axis=1).reshape(-1, ROUTER_PAD)[:, :N_EXPERTS]
            f = grouped_moe(toks, affs, b_router, e_w_gate[i], e_w_up[i], e_w_down[i]).reshape(B, tc + L, D)
            xc = xc + g2c * f[:, :tc]
            x = x + g2 * f[:, tc:]
    return final_rmsnorm(x, final_norm_g)
```

```python
import functools
import math

import jax
import jax.numpy as jnp
from jax import lax
from jax.experimental import pallas as pl
from jax.experimental.pallas import tpu as pltpu

F32 = jnp.float32
BF16 = jnp.bfloat16

D_MODEL = 2048
DEPTH = 4
GRID_W = 64
N_MIXERS = 4
EPS = 1e-6
DA_HEADS = 16
DA_HALF = D_MODEL // DA_HEADS // 2
DA_VDIM = 2 * DA_HALF
ROPE_BASE = 10000.0
GDN_HEADS = 16
GDN_DK = D_MODEL // GDN_HEADS
GDN_DV = D_MODEL // GDN_HEADS
GDN_CHUNK = 64
NA_HEADS = 16
NA_DIM = D_MODEL // NA_HEADS
WIN_R = 8
WIN_C = 16
HG_HEADS = 16
HG_DK = D_MODEL // HG_HEADS
HG_DV = D_MODEL // HG_HEADS
HG_CHUNK = 32
N_EXPERTS = 16
N_GROUPS = 4
EXPERTS_PER_GROUP = N_EXPERTS // N_GROUPS
GROUP_SCORE_TOPK = 2
TOP_K = 2
D_EXPERT = D_MODEL // 2

V7X_VMEM_LIMIT_BYTES = 56 * 1024 * 1024
LANES = 128
MOE_TILE = 512
ROUTER_PAD = LANES


def _largest_divisor(n, candidates):
    for c in candidates:
        if n % c == 0:
            return c
    raise ValueError(f"no tile in {candidates} divides {n}")


def _cparams(*sem):
    return pltpu.CompilerParams(dimension_semantics=sem, vmem_limit_bytes=V7X_VMEM_LIMIT_BYTES)


def _mm_kernel(a_ref, w_ref, o_ref):
    o_ref[...] = jnp.dot(a_ref[...], w_ref[...], preferred_element_type=F32).astype(o_ref.dtype)


def mm(a, w, out_dtype=F32):
    m, k = a.shape
    n = w.shape[1]
    n_pad = -n % LANES
    if n_pad:
        w = jnp.pad(w, ((0, 0), (0, n_pad)))
    a = a.astype(BF16)
    w = w.astype(BF16)
    np_ = n + n_pad
    tm = _largest_divisor(m, (1024, 512, 256, 128))
    tn = _largest_divisor(np_, (512, 256, 128))
    out = pl.pallas_call(
        _mm_kernel,
        out_shape=jax.ShapeDtypeStruct((m, np_), out_dtype),
        grid=(m // tm, np_ // tn),
        in_specs=[pl.BlockSpec((tm, k), lambda i, j: (i, 0)),
                  pl.BlockSpec((k, tn), lambda i, j: (0, j))],
        out_specs=pl.BlockSpec((tm, tn), lambda i, j: (i, j)),
        compiler_params=_cparams("parallel", "arbitrary"),
        name="proj_mm",
    )(a, w)
    return out[:, :n] if n_pad else out


def _normmod_kernel(x_ref, g_ref, sc_ref, sh_ref, o_ref):
    x = x_ref[0].astype(F32)
    y = x * lax.rsqrt(jnp.mean(x * x, axis=-1, keepdims=True) + EPS)
    o_ref[0] = ((y * g_ref[...]) * (1.0 + sc_ref[0]) + sh_ref[0]).astype(o_ref.dtype)


def _normmod_router_kernel(x_ref, g_ref, sc_ref, sh_ref, wr_ref, o_ref, aff_ref):
    x = x_ref[0].astype(F32)
    y = x * lax.rsqrt(jnp.mean(x * x, axis=-1, keepdims=True) + EPS)
    h = (y * g_ref[...]) * (1.0 + sc_ref[0]) + sh_ref[0]
    o_ref[0] = h.astype(o_ref.dtype)
    logits = jnp.dot(h, wr_ref[...], preferred_element_type=F32, precision=lax.Precision.HIGHEST)
    aff_ref[0] = jax.nn.sigmoid(logits)


def norm_modulate(x, g, sc, sh, w_router=None):
    b, t, d = x.shape
    tm = _largest_divisor(t, (512, 256, 128))
    x_spec = pl.BlockSpec((1, tm, d), lambda i, j: (i, j, 0))
    g_spec = pl.BlockSpec((1, d), lambda i, j: (0, 0))
    mod_spec = pl.BlockSpec((1, 1, d), lambda i, j: (i, 0, 0))
    g2 = g.reshape(1, d).astype(F32)
    if w_router is None:
        return pl.pallas_call(
            _normmod_kernel,
            out_shape=jax.ShapeDtypeStruct((b, t, d), BF16),
            grid=(b, t // tm),
            in_specs=[x_spec, g_spec, mod_spec, mod_spec],
            out_specs=x_spec,
            compiler_params=_cparams("parallel", "parallel"),
            name="norm_modulate",
        )(x, g2, sc, sh)
    wr = jnp.pad(w_router.astype(F32), ((0, 0), (0, ROUTER_PAD - w_router.shape[1])))
    return pl.pallas_call(
        _normmod_router_kernel,
        out_shape=(jax.ShapeDtypeStruct((b, t, d), BF16),
                   jax.ShapeDtypeStruct((b, t, ROUTER_PAD), F32)),
        grid=(b, t // tm),
        in_specs=[x_spec, g_spec, mod_spec, mod_spec,
                  pl.BlockSpec((d, ROUTER_PAD), lambda i, j: (0, 0))],
        out_specs=(x_spec, pl.BlockSpec((1, tm, ROUTER_PAD), lambda i, j: (i, j, 0))),
        compiler_params=_cparams("parallel", "parallel"),
        name="norm_modulate_router",
    )(x, g2, sc, sh, wr)


def _rmsnorm_kernel(x_ref, g_ref, o_ref):
    x = x_ref[0].astype(F32)
    y = x * lax.rsqrt(jnp.mean(x * x, axis=-1, keepdims=True) + EPS)
    o_ref[0] = y * g_ref[...]


def final_rmsnorm(x, g):
    b, t, d = x.shape
    tm = _largest_divisor(t, (512, 256, 128))
    spec = pl.BlockSpec((1, tm, d), lambda i, j: (i, j, 0))
    return pl.pallas_call(
        _rmsnorm_kernel,
        out_shape=jax.ShapeDtypeStruct((b, t, d), F32),
        grid=(b, t // tm),
        in_specs=[spec, pl.BlockSpec((1, d), lambda i, j: (0, 0))],
        out_specs=spec,
        compiler_params=_cparams("parallel", "parallel"),
        name="final_rmsnorm",
    )(x, g.reshape(1, d).astype(F32))


def _moe_kernel(te_ref, tv_ref, x_ref, wg_ref, wu_ref, wd_ref, o_ref):
    del te_ref
    i = pl.program_id(0)

    @pl.when(tv_ref[i] > 0)
    def _():
        x = x_ref[...]
        g = jnp.dot(x, wg_ref[0], preferred_element_type=F32)
        u = jnp.dot(x, wu_ref[0], preferred_element_type=F32)
        act = (g * jax.nn.sigmoid(g) * u).astype(BF16)
        o_ref[...] = jnp.dot(act, wd_ref[0], preferred_element_type=F32)

    @pl.when(tv_ref[i] == 0)
    def _():
        o_ref[...] = jnp.zeros_like(o_ref)


def _route(aff, b_router):
    n = aff.shape[0]
    sel = (aff + b_router.astype(F32)).reshape(n, N_GROUPS, EXPERTS_PER_GROUP)
    group_score = jnp.sum(lax.top_k(sel, GROUP_SCORE_TOPK)[0], axis=-1)
    g_best = jnp.argmax(group_score, axis=-1)
    in_group = jnp.take_along_axis(sel, g_best[:, None, None], axis=1)[:, 0]
    _, local = lax.top_k(in_group, TOP_K)
    expert_idx = g_best[:, None] * EXPERTS_PER_GROUP + local
    w = jnp.take_along_axis(aff, expert_idx, axis=-1)
    w = w / jnp.sum(w, axis=-1, keepdims=True)
    return expert_idx.astype(jnp.int32), w


def grouped_moe(h2, aff, b_router, w_gate, w_up, w_down):
    n, d = h2.shape
    expert_idx, w = _route(aff, b_router)
    n_slots = TOP_K * n
    e_flat = expert_idx.reshape(-1)
    onehot = (e_flat[:, None] == jnp.arange(N_EXPERTS, dtype=jnp.int32)[None, :]).astype(jnp.int32)
    csum = jnp.cumsum(onehot, axis=0)
    rank = jnp.take_along_axis(csum, e_flat[:, None], axis=1)[:, 0] - 1
    counts = csum[-1]
    padded = ((counts + MOE_TILE - 1) // MOE_TILE) * MOE_TILE
    pad_end = jnp.cumsum(padded)
    pad_off = pad_end - padded
    dest_flat = pad_off[e_flat] + rank
    dest = dest_flat.reshape(n, TOP_K)

    n_tiles = -(-n_slots // MOE_TILE) + N_EXPERTS
    n_pad_slots = n_tiles * MOE_TILE
    tile_start = jnp.arange(n_tiles, dtype=jnp.int32) * MOE_TILE
    tile_expert = jnp.minimum(jnp.searchsorted(pad_end, tile_start, side="right"), N_EXPERTS - 1).astype(jnp.int32)
    tile_valid = (tile_start < pad_end[-1]).astype(jnp.int32)

    slot_token = jnp.zeros((n_pad_slots,), jnp.int32).at[dest_flat].set(
        jnp.arange(n_slots, dtype=jnp.int32) // TOP_K, unique_indices=True)

    xs = jnp.take(h2, slot_token, axis=0)
    f = D_EXPERT
    ys = pl.pallas_call(
        _moe_kernel,
        out_shape=jax.ShapeDtypeStruct((n_pad_slots, d), F32),
        grid_spec=pltpu.PrefetchScalarGridSpec(
            num_scalar_prefetch=2,
            grid=(n_tiles,),
            in_specs=[pl.BlockSpec((MOE_TILE, d), lambda i, te, tv: (i, 0)),
                      pl.BlockSpec((1, d, f), lambda i, te, tv: (te[i], 0, 0)),
                      pl.BlockSpec((1, d, f), lambda i, te, tv: (te[i], 0, 0)),
                      pl.BlockSpec((1, f, d), lambda i, te, tv: (te[i], 0, 0))],
            out_specs=pl.BlockSpec((MOE_TILE, d), lambda i, te, tv: (i, 0)),
        ),
        compiler_params=_cparams("arbitrary"),
        name="moe_experts",
    )(tile_expert, tile_valid, xs, w_gate.astype(BF16), w_up.astype(BF16), w_down.astype(BF16))
    out = jnp.take(ys, dest[:, 0], axis=0) * w[:, 0:1] + jnp.take(ys, dest[:, 1], axis=0) * w[:, 1:2]
    return out


def _diff_attn_kernel(lam_ref, q_ref, k_ref, v_ref, g_ref, o_ref, *, n_ctx, lam_init):
    i = pl.program_id(2)
    lam = lam_ref[0]
    lane = lax.broadcasted_iota(jnp.int32, (1, 2 * DA_HALF), 1)
    q = q_ref[0]
    q1 = jnp.where(lane < DA_HALF, q, jnp.zeros_like(q))
    q2 = jnp.where(lane >= DA_HALF, q, jnp.zeros_like(q))
    nt = (((1,), (1,)), ((), ()))

    def attend(k, v):
        def softmax(s):
            m = jnp.max(s, axis=-1, keepdims=True)
            p = jnp.exp(s - m)
            return p * (1.0 / jnp.sum(p, axis=-1, keepdims=True))

        p1 = softmax(lax.dot_general(q1, k, nt, preferred_element_type=F32))
        p2 = softmax(lax.dot_general(q2, k, nt, preferred_element_type=F32))
        p = (p1 - lam * p2).astype(BF16)
        o = jnp.dot(p, v, preferred_element_type=F32)
        o = o * lax.rsqrt(jnp.mean(o * o, axis=-1, keepdims=True) + 1e-5)
        o_ref[0] = ((o * g_ref[...]) * (1.0 - lam_init)).astype(o_ref.dtype)

    @pl.when(i == 0)
    def _():
        attend(k_ref[0, :n_ctx, :], v_ref[0, :n_ctx, :])

    @pl.when(i > 0)
    def _():
        attend(k_ref[0], v_ref[0])


def diff_attn_core(qkv, lam, subln_g, n_ctx, lam_init):
    b, t, _ = qkv.shape
    tq = n_ctx
    nh = DA_HEADS
    hw = 2 * DA_HALF
    kern = functools.partial(_diff_attn_kernel, n_ctx=n_ctx, lam_init=lam_init)
    return pl.pallas_call(
        kern,
        out_shape=jax.ShapeDtypeStruct((b, t, nh * hw), BF16),
        grid=(b, nh, t // tq),
        in_specs=[pl.BlockSpec(memory_space=pltpu.SMEM),
                  pl.BlockSpec((1, tq, hw), lambda bi, h, i: (bi, i, h)),
                  pl.BlockSpec((1, t, hw), lambda bi, h, i: (bi, 0, nh + h)),
                  pl.BlockSpec((1, t, hw), lambda bi, h, i: (bi, 0, 2 * nh + h)),
                  pl.BlockSpec((1, hw), lambda bi, h, i: (0, 0))],
        out_specs=pl.BlockSpec((1, tq, hw), lambda bi, h, i: (bi, i, h)),
        compiler_params=_cparams("parallel", "parallel", "arbitrary"),
        name="diff_attn",
    )(lam.reshape(1).astype(F32), qkv, qkv, qkv, subln_g.reshape(1, hw).astype(F32))


NA_QROWS = 8
NA_KROWS = 2 * NA_QROWS
NA_DR_PAD = 16
NEG_INF = float("-inf")
NA_BIAS_ROWS = 2 * WIN_R - 1
NA_BIAS_COLS = 2 * WIN_C - 1


def _na_kernel(rpb_ref, q_ref, k_ref, v_ref, o_ref, bias_ref, *, n_ctx, rows):
    h = pl.program_id(0)
    b = pl.program_id(1)
    j = pl.program_id(2)
    w = GRID_W
    nq = NA_QROWS * w
    nk = NA_KROWS * w
    lane = lax.broadcasted_iota(jnp.int32, (w, 2 * w), 1)
    nt = (((1,), (1,)), ((), ()))

    @pl.when((b == 0) & (j == 0))
    def _():
        c = lax.broadcasted_iota(jnp.int32, (w, 2 * w), 0)
        kc = lane % w
        c0 = jnp.clip(c - WIN_C // 2, 0, w - WIN_C)
        in_win = (kc >= c0) & (kc < c0 + WIN_C)
        single = []
        for dr in range(-(WIN_R - 1), WIN_R):
            t = jnp.full((w, 2 * w), NEG_INF, F32)
            for dc in range(-(WIN_C - 1), WIN_C):
                t = jnp.where(kc - c == dc, rpb_ref[(h * NA_BIAS_ROWS + dr + WIN_R - 1) * NA_BIAS_COLS + dc + WIN_C - 1], t)
            single.append(jnp.where(in_win, t, NEG_INF))
        neg = jnp.full((w, 2 * w), NEG_INF, F32)

        def at(dr):
            return single[dr + WIN_R - 1] if abs(dr) < WIN_R else neg

        for d in range(2 * NA_DR_PAD + 1):
            bias_ref[d] = jnp.where(lane < w, at(d - NA_DR_PAD), at(d - NA_DR_PAD + 1))

    kctx = k_ref[0, :n_ctx, :]
    vctx = v_ref[0, :n_ctx, :]

    def finish(parts, q_lo, n):
        m = parts[0][0].max(axis=-1, keepdims=True)
        for s, _ in parts[1:]:
            m = jnp.maximum(m, s.max(axis=-1, keepdims=True))
        l = 0.0
        o = 0.0
        for s, vv in parts:
            p = jnp.exp(s - m)
            l = l + p.sum(axis=-1, keepdims=True)
            o = o + jnp.dot(p.astype(BF16), vv, preferred_element_type=F32)
        o_ref[0, pl.ds(q_lo, n), :] = (o * (1.0 / l)).astype(o_ref.dtype)

    @pl.when(j < rows // NA_QROWS)
    def _():
        r_lo = j * NA_QROWS
        ks = jnp.clip(r_lo - WIN_R // 2, 0, rows - NA_KROWS)
        q_lo = pl.multiple_of(n_ctx + r_lo * w, w)
        k_lo = pl.multiple_of(n_ctx + ks * w, w)
        q = q_ref[0, pl.ds(q_lo, nq), :]
        kwin = k_ref[0, pl.ds(k_lo, nk), :]
        vwin = v_ref[0, pl.ds(k_lo, nk), :]
        s_win = lax.dot_general(q, kwin, nt, preferred_element_type=F32)
        half = (lane >= w).astype(jnp.int32)
        row_blocks = []
        for i in range(NA_QROWS):
            r = r_lo + i
            r0 = jnp.clip(r - WIN_R // 2, 0, rows - WIN_R)
            tiles = []
            for jj in range(NA_KROWS // 2):
                kr = ks + 2 * jj
                off = lax.bitcast_convert_type(half + (kr - r0), jnp.uint32)
                t = s_win[i * w:(i + 1) * w, 2 * jj * w:(2 * jj + 2) * w] + bias_ref[kr - r + NA_DR_PAD]
                tiles.append(jnp.where(off < WIN_R, t, NEG_INF))
            row_blocks.append(jnp.concatenate(tiles, axis=1))
        s_win = jnp.concatenate(row_blocks, axis=0)
        s_ctx = lax.dot_general(q, kctx, nt, preferred_element_type=F32)
        finish([(s_win, vwin), (s_ctx, vctx)], q_lo, nq)

    @pl.when(j == rows // NA_QROWS)
    def _():
        q = q_ref[0, :n_ctx, :]
        finish([(lax.dot_general(q, kctx, nt, preferred_element_type=F32), vctx)], 0, n_ctx)


def na_attn_core(qkv, rpb, n_ctx):
    b, t, _ = qkv.shape
    rows = (t - n_ctx) // GRID_W
    assert rows % NA_QROWS == 0 and rows >= NA_KROWS
    nh, hd = NA_HEADS, NA_DIM
    kern = functools.partial(_na_kernel, n_ctx=n_ctx, rows=rows)

    def col(off):
        return pl.BlockSpec((1, t, hd), lambda h, bi, j: (bi, 0, off + h))

    return pl.pallas_call(
        kern,
        out_shape=jax.ShapeDtypeStruct((b, t, nh * hd), BF16),
        grid=(nh, b, rows // NA_QROWS + 1),
        in_specs=[pl.BlockSpec(memory_space=pltpu.SMEM), col(0), col(nh), col(2 * nh)],
        out_specs=col(0),
        scratch_shapes=[pltpu.VMEM((2 * NA_DR_PAD + 1, GRID_W, 2 * GRID_W), F32)],
        compiler_params=_cparams("arbitrary", "arbitrary", "arbitrary"),
        name="na_attn",
    )(rpb.astype(F32).reshape(-1), qkv, qkv, qkv)


def rmsnorm(x, g, eps=EPS):
    xf = x.astype(F32)
    y = xf * lax.rsqrt(jnp.mean(xf * xf, axis=-1, keepdims=True) + eps)
    return y.astype(x.dtype) * g


def l2norm(x):
    return x * lax.rsqrt(jnp.sum(x * x, axis=-1, keepdims=True) + EPS)


def flip(t):
    return None if t is None else jnp.flip(t, axis=1)


def proj(t, w, out_dtype=F32):
    b, tt, k = t.shape
    return mm(t.reshape(b * tt, k), w, out_dtype).reshape(b, tt, w.shape[1])


def axial_rope_tables(length, dim):
    n_freq = dim // 4
    t = jnp.arange(length)
    pos = jnp.stack([t // GRID_W, t % GRID_W], axis=-1).astype(F32)
    inv = ROPE_BASE ** (-jnp.arange(n_freq, dtype=F32) / n_freq)
    ang = pos[:, :, None] * inv
    return jnp.cos(ang), jnp.sin(ang)


def apply_rope(x, cos, sin):
    n_freq = cos.shape[-1]
    xr = x.reshape(*x.shape[:-1], 2, 2, n_freq)
    x1, x2 = xr[..., 0, :], xr[..., 1, :]
    c = cos[:, None].astype(x.dtype)
    s = sin[:, None].astype(x.dtype)
    out = jnp.stack([x1 * c - x2 * s, x1 * s + x2 * c], axis=-2)
    return out.reshape(x.shape)


def diff_attention(h, hc, w_qkv, w_o, lam_vec, subln_g, layer_idx):
    b, l, d = h.shape
    tc = hc.shape[1]
    dq = 2 * DA_HEADS * DA_HALF
    lam_init = 0.8 - 0.6 * math.exp(-0.3 * layer_idx)
    lv = lam_vec.astype(F32)
    lam = jnp.exp(jnp.sum(lv[0] * lv[1])) - jnp.exp(jnp.sum(lv[2] * lv[3])) + lam_init
    scale = DA_HALF ** -0.5

    qkv = proj(h, w_qkv)
    qkv_c = proj(hc, w_qkv)
    cos, sin = axial_rope_tables(l, DA_HALF)
    q = apply_rope(qkv[..., :dq].reshape(b, l, 2 * DA_HEADS, DA_HALF), cos, sin).reshape(b, l, dq) * scale
    k = apply_rope(qkv[..., dq:2 * dq].reshape(b, l, 2 * DA_HEADS, DA_HALF), cos, sin).reshape(b, l, dq)
    lat = jnp.concatenate([q, k, qkv[..., 2 * dq:]], axis=-1).astype(BF16)
    ctx = jnp.concatenate([qkv_c[..., :dq] * scale, qkv_c[..., dq:]], axis=-1).astype(BF16)
    o = diff_attn_core(jnp.concatenate([ctx, lat], axis=1), lam, subln_g, tc, lam_init)
    y = proj(o, w_o)
    return y[:, tc:], y[:, :tc]


def centred_dwconv(x, w):
    k = w.shape[0]
    pad = k // 2
    t = x.shape[1]
    xp = jnp.pad(x, ((0, 0), (pad, pad), (0, 0)))
    out = xp[:, 0:t] * w[0]
    for j in range(1, k):
        out = out + xp[:, j:j + t] * w[j]
    return out


GDN_HEAD_GROUP = 16
GDN_SUB = 16


def _bdot(a, b):
    return jnp.dot(a.astype(BF16), b.astype(BF16), preferred_element_type=F32)


def _scan_chunk(d, s, n, n_ctx_chunks):
    back = jnp.where(s < n_ctx_chunks, n_ctx_chunks - 1 - s, n + n_ctx_chunks - 1 - s)
    return jnp.where(d == 0, s, back)


def _gdn_kernel(sdec_ref, qs_ref, k_ref, qg_ref, kb_ref, kbg_ref, kd_ref, vb_ref, gc_ref, gr_ref, o_ref, s_ref,
                *, n_ctx_chunks):
    d = pl.program_id(0)
    b = pl.program_id(1)
    hg = pl.program_id(2)
    s = pl.program_id(3)
    n = pl.num_programs(3)
    c = GDN_CHUNK

    @pl.when(s == 0)
    def _():
        s_ref[...] = jnp.zeros_like(s_ref)

    row = lax.broadcasted_iota(jnp.int32, (c, c), 0)
    col = lax.broadcasted_iota(jnp.int32, (c, c), 1)
    ahead = (row - col) * (1 - 2 * d)
    strict = ahead > 0
    incl = ahead >= 0
    same_blk = (row // GDN_SUB) == (col // GDN_SUB)
    eye = (row == col).astype(F32)
    nt = (((1,), (1,)), ((), ()))
    tn = (((0,), (0,)), ((), ()))
    chunk = _scan_chunk(d, s, n, n_ctx_chunks)
    sdec_base = ((d * pl.num_programs(1) + b) * n + chunk) * GDN_HEADS + hg * GDN_HEAD_GROUP
    heads = range(GDN_HEAD_GROUP)
    sls = [slice(g * GDN_DK, (g + 1) * GDN_DK) for g in heads]

    def each(fn, *lists):
        return [fn(*vals) for vals in zip(*lists)]

    ks = [k_ref[:, sl] for sl in sls]
    decay = [jnp.exp(jnp.where(incl, gc_ref[:, g:g + 1] - gr_ref[g:g + 1, :], NEG_INF)) for g in heads]
    a = each(lambda sl, k, dc: jnp.where(
        strict, lax.dot_general(kb_ref[:, sl], k, nt, preferred_element_type=F32) * dc, 0.0), sls, ks, decay)
    qk = each(lambda sl, k, dc: jnp.where(
        incl, lax.dot_general(qs_ref[:, sl], k, nt, preferred_element_type=F32) * dc, 0.0), sls, ks, decay)
    dblk = each(lambda m: jnp.where(same_blk, m, 0.0), a)
    d2 = each(lambda m: _bdot(m, m), dblk)
    x = each(lambda m, m2: _bdot(eye - m, eye + m2), dblk, d2)
    d4 = each(lambda m: _bdot(m, m), d2)
    x = each(lambda m, m4: _bdot(m, eye + m4), x, d4)
    d8 = each(lambda m: _bdot(m, m), d4)
    x = each(lambda m, m8: _bdot(m, eye + m8), x, d8)
    nmat = each(lambda m, am, dm: _bdot(m, am - dm), x, a, dblk)
    n2 = each(lambda m: _bdot(m, m), nmat)
    y = each(lambda m, m2: _bdot(eye - m, eye + m2), nmat, n2)
    t = each(_bdot, y, x)
    uw = each(lambda m, sl: _bdot(m, jnp.concatenate([vb_ref[:, sl], kbg_ref[:, sl]], axis=1)), t, sls)
    state = [s_ref[g] for g in heads]
    v_new = each(lambda m, st: m[:, :GDN_DV] - _bdot(m[:, GDN_DV:], st), uw, state)
    o_state = each(lambda sl, st: _bdot(qg_ref[:, sl], st), sls, state)
    o_new = each(_bdot, qk, v_new)
    s_new = each(lambda sl, vn: lax.dot_general(kd_ref[:, sl], vn.astype(BF16), tn, preferred_element_type=F32),
                 sls, v_new)
    for g in heads:
        s_ref[g] = state[g] * sdec_ref[sdec_base + g] + s_new[g]
    o_ref[...] = jnp.concatenate(each(lambda p, q: p + q, o_state, o_new), axis=1)


def gated_delta_core(qs, k, qg, kb, kbg, kd, vb, g, g_last, n_ctx_chunks):
    b, t, hd = k.shape
    c, hh, gg = GDN_CHUNK, GDN_HEADS, GDN_HEAD_GROUP
    n = t // c
    gw = gg * GDN_DK
    g_col = g.reshape(2, b, t, hh // gg, gg).transpose(0, 1, 3, 2, 4)
    g_row = g.transpose(0, 1, 2, 4, 3).reshape(2, b, n, hh // gg, gg, c)
    sdec = jnp.exp(g_last).reshape(-1)

    def chunk_of(d, s):
        return _scan_chunk(d, s, n, n_ctx_chunks)

    shared = pl.BlockSpec((None, c, gw), lambda d, bi, hg, s, sd: (bi, chunk_of(d, s), hg))
    per_dir = pl.BlockSpec((None, None, c, gw), lambda d, bi, hg, s, sd: (d, bi, chunk_of(d, s), hg))
    kern = functools.partial(_gdn_kernel, n_ctx_chunks=n_ctx_chunks)
    return pl.pallas_call(
        kern,
        out_shape=jax.ShapeDtypeStruct((2, b, t, hd), F32),
        grid_spec=pltpu.PrefetchScalarGridSpec(
            num_scalar_prefetch=1,
            grid=(2, b, hh // gg, n),
            in_specs=[shared, shared, per_dir, per_dir, per_dir, per_dir, per_dir,
                      pl.BlockSpec((None, None, None, c, gg), lambda d, bi, hg, s, sd: (d, bi, hg, chunk_of(d, s), 0)),
                      pl.BlockSpec((None, None, None, None, gg, c),
                                   lambda d, bi, hg, s, sd: (d, bi, chunk_of(d, s), hg, 0, 0))],
            out_specs=per_dir,
            scratch_shapes=[pltpu.VMEM((gg, GDN_DK, GDN_DV), F32)],
        ),
        compiler_params=_cparams("parallel", "parallel", "parallel", "arbitrary"),
        name="gated_delta",
    )(sdec, qs, k, qg, kb, kbg, kd, vb, g_col, g_row)


def gated_deltanet(h, hc, w_in, conv_w, w_ab, dt_bias, a_log, norm_g, w_o):
    B = h.shape[0]
    tc = hc.shape[1]
    H, C = GDN_HEADS, GDN_CHUNK
    wq = H * GDN_DK
    hi = 2 * wq + H * GDN_DV

    def feats(t):
        T = t.shape[1]
        zz = proj(t, w_in)
        z = jax.nn.silu(centred_dwconv(zz[..., :hi], conv_w)).astype(F32)
        gate = jax.nn.silu(zz[..., hi:])
        q = l2norm(z[..., :wq].reshape(B, T, H, GDN_DK))
        k = l2norm(z[..., wq:2 * wq].reshape(B, T, H, GDN_DK))
        v = z[..., 2 * wq:].reshape(B, T, H, GDN_DV)
        ab = proj(t, w_ab).astype(F32).reshape(B, T, 2, 2, H)
        log_a = -jnp.exp(a_log.astype(F32)) * jax.nn.softplus(ab[:, :, 0] + dt_bias.astype(F32))
        beta = jax.nn.sigmoid(ab[:, :, 1])
        return q, k, v, log_a, beta, gate

    q, k, v, la, be, gate = [jnp.concatenate([a, b], axis=1) for a, b in zip(feats(hc), feats(h))]
    T = q.shape[1]
    n = T // C
    la_c = la.reshape(B, n, C, 2, H)
    g_f = jnp.cumsum(la_c[:, :, :, 0], axis=2)
    g_b = jnp.flip(jnp.cumsum(jnp.flip(la_c[:, :, :, 1], axis=2), axis=2), axis=2)
    g = jnp.stack([g_f, g_b])
    g_last = jnp.stack([g_f[:, :, -1], g_b[:, :, 0]])
    be_d = jnp.moveaxis(be, 2, 0)[..., None]
    eg = jnp.exp(g).reshape(2, B, T, H, 1)
    ekd = jnp.exp(g_last[:, :, :, None] - g).reshape(2, B, T, H, 1)

    def flat(a):
        return a.astype(BF16).reshape(*a.shape[:-2], -1)

    qs = q * GDN_DK ** -0.5
    kb = k[None] * be_d
    o = gated_delta_core(flat(qs), flat(k), flat(qs[None] * eg), flat(kb), flat(kb * eg), flat(k[None] * ekd),
                         flat(v[None] * be_d), g, g_last, tc // C)
    o = (o[0] + o[1]).reshape(B, T, H, GDN_DV)
    o = rmsnorm(o, norm_g).astype(F32) * gate.reshape(B, T, H, GDN_DV)
    y = proj(o.reshape(B, T, -1), w_o)
    return y[:, tc:], y[:, :tc]


def neighbourhood_attention(h, hc, w_qkv, rpb, w_o):
    tc = hc.shape[1]
    hd = NA_HEADS * NA_DIM
    scale = NA_DIM ** -0.5

    def scaled_qkv(t):
        qkv = proj(t, w_qkv)
        return jnp.concatenate([qkv[..., :hd] * scale, qkv[..., hd:]], axis=-1).astype(BF16)

    o = na_attn_core(jnp.concatenate([scaled_qkv(hc), scaled_qkv(h)], axis=1), rpb, tc)
    y = proj(o, w_o)
    return y[:, tc:], y[:, :tc]


def gla_chunked(q, k, v, log_f, s0, with_output):
    B, T, H, DK = k.shape
    C = HG_CHUNK
    n = T // C

    def chunks(t):
        return t.reshape(B, n, C, H, -1).transpose(1, 0, 3, 2, 4)

    kc, vc = chunks(k), chunks(v)
    b = jnp.cumsum(chunks(log_f), axis=-2)
    b_last = b[..., -1:, :]
    k_dec = kc * jnp.exp(b_last - b)
    s_dec = jnp.swapaxes(jnp.exp(b_last), -1, -2)
    if not with_output:
        def step_state(s, xs):
            kd_i, v_i, sd_i = xs
            return s * sd_i + jnp.swapaxes(kd_i, -1, -2) @ v_i, None
        s, _ = lax.scan(step_state, s0, (k_dec, vc, s_dec))
        return None, s
    qc = chunks(q) * DK ** -0.5
    idx = jnp.arange(C)
    incl = (idx[:, None] >= idx[None, :])[:, :, None]

    def step(s, xs):
        q_i, k_i, v_i, b_i, kd_i, sd_i = xs
        rel = jnp.exp(jnp.where(incl, b_i[..., :, None, :] - b_i[..., None, :, :], -jnp.inf))
        att = jnp.sum(q_i[..., :, None, :] * k_i[..., None, :, :] * rel, axis=-1)
        o = (q_i * jnp.exp(b_i)) @ s + att @ v_i
        return s * sd_i + jnp.swapaxes(kd_i, -1, -2) @ v_i, o

    s, o = lax.scan(step, s0, (qc, kc, vc, b, k_dec, s_dec))
    return o.transpose(1, 0, 3, 2, 4).reshape(B, T, H, -1), s


def hgrn2(h, hc, w_q, w_i, w_f, b_f, w_g, norm_g, w_o, lb, ctx_out):
    B = h.shape[0]
    lbf = lb.astype(F32)

    def feats(t, with_q):
        T = t.shape[1]
        z = (jnp.stack([proj(t, w_f[0]), proj(t, w_f[1])], axis=2) + b_f).astype(F32)
        log_f = jnp.log(lbf + (1.0 - lbf) * jax.nn.sigmoid(z))
        k = (1.0 - lbf) * jax.nn.sigmoid(-z)
        shp = (B, T, 2, HG_HEADS, HG_DK)
        v = proj(t, w_i).astype(F32).reshape(B, T, HG_HEADS, HG_DV)
        q = jax.nn.silu(proj(t, w_q).astype(F32)).reshape(B, T, HG_HEADS, HG_DK) if with_q else None
        return q, k.reshape(shp), v, log_f.reshape(shp)

    s0 = jnp.zeros((B, HG_HEADS, HG_DK, HG_DV), F32)
    qc, kc, vc, lfc = feats(hc, ctx_out)
    q, k, v, lf = feats(h, True)
    oc_f, sc_f = gla_chunked(qc, kc[:, :, 0], vc, lfc[:, :, 0], s0, ctx_out)
    oc_b, sc_b = gla_chunked(flip(qc), flip(kc[:, :, 1]), flip(vc), flip(lfc[:, :, 1]), s0, ctx_out)
    o_f, _ = gla_chunked(q, k[:, :, 0], v, lf[:, :, 0], sc_f, True)
    o_b, _ = gla_chunked(flip(q), flip(k[:, :, 1]), flip(v), flip(lf[:, :, 1]), sc_b, True)

    def finish(t, o):
        T = t.shape[1]
        gate = jax.nn.sigmoid(proj(t, w_g)).reshape(B, T, HG_HEADS, HG_DV)
        o = rmsnorm(o, norm_g).astype(F32) * gate
        return proj(o.reshape(B, T, -1), w_o)

    y = finish(h, o_f + flip(o_b))
    yc = finish(hc, oc_f + flip(oc_b)) if ctx_out else None
    return y, yc


def kernel(x, c, ctx, c_ctx, w_mod, b_mod, norm_mix_g, norm_ffn_g, da_w_qkv, da_w_o, da_lam, da_subln_g, gdn_w_in, gdn_conv, gdn_w_ab, gdn_dt_bias, gdn_a_log, gdn_norm_g, gdn_w_o, na_w_qkv, na_rpb, na_w_o, hg_w_q, hg_w_i, hg_w_f, hg_b_f, hg_w_g, hg_norm_g, hg_w_o, hg_lb_logits, w_router, b_router, e_w_gate, e_w_up, e_w_down, final_norm_g):
    B, L, D = x.shape
    tc = ctx.shape[1]
    xc = ctx
    cond = jax.nn.silu(c)
    cond_ctx = jax.nn.silu(c_ctx)
    p_lb = jax.nn.softmax(hg_lb_logits.astype(F32), axis=0)
    lb_all = jnp.cumsum(p_lb, axis=0) - p_lb[0]
    cond_all = jnp.concatenate([cond, cond_ctx[None]], axis=0)
    for i in range(DEPTH):
        last = i == DEPTH - 1
        kind, j = i % N_MIXERS, i // N_MIXERS
        mod = mm(jnp.pad(cond_all, ((0, -(B + 1) % 128), (0, 0))), w_mod[i])[:B + 1] + b_mod[i]
        sh1, sc1, g1, sh2, sc2, g2 = jnp.split(mod[:B, None, :], 6, axis=-1)
        sh1c, sc1c, g1c, sh2c, sc2c, g2c = jnp.split(mod[B], 6, axis=-1)

        def bc_ctx(v):
            return jnp.broadcast_to(v[None, None, :], (B, 1, D))

        h = norm_modulate(x, norm_mix_g[i], sc1, sh1)
        hc = norm_modulate(xc, norm_mix_g[i], bc_ctx(sc1c), bc_ctx(sh1c))
        if kind == 0:
            y, yc = diff_attention(h, hc, da_w_qkv[j], da_w_o[j], da_lam[j], da_subln_g[j], i)
        elif kind == 1:
            y, yc = gated_deltanet(h, hc, gdn_w_in[j], gdn_conv[j], gdn_w_ab[j], gdn_dt_bias[j],
                                   gdn_a_log[j], gdn_norm_g[j], gdn_w_o[j])
        elif kind == 2:
            y, yc = neighbourhood_attention(h, hc, na_w_qkv[j], na_rpb[j], na_w_o[j])
        else:
            y, yc = hgrn2(h, hc, hg_w_q[j], hg_w_i[j], hg_w_f[j], hg_b_f[j], hg_w_g[j],
                          hg_norm_g[j], hg_w_o[j], lb_all[i], not last)
        x = x + g1 * y
        h2, aff = norm_modulate(x, norm_ffn_g[i], sc2, sh2, w_router)
        if last:
            f = grouped_moe(h2.reshape(B * L, D), aff.reshape(B * L, ROUTER_PAD)[:, :N_EXPERTS],
                            b_router, e_w_gate[i], e_w_up[i], e_w_down[i])
            x = x + g2 * f.reshape(B, L, D)
        else:
            xc = xc + g1c * yc
            h2c, affc = norm_modulate(xc, norm_ffn_g[i], bc_ctx(sc2c), bc_ctx(sh2c), w_router)
            toks = jnp.concatenate([h2c, h2], axis=1).reshape(-1, D)
            affs = jnp.concatenate([affc, aff], axis=1).reshape(-1, ROUTER_PAD)[:, :N_EXPERTS]
            f = grouped_moe(toks, affs, b_router, e_w_gate[i], e_w_up[i], e_w_down[i]).reshape(B, tc + L, D)
            xc = xc + g2c * f[:, :tc]
            x = x + g2 * f[:, tc:]
    return final_rmsnorm(x, final_norm_g)
```

```python
import functools
import math

import jax
import jax.numpy as jnp
from jax import lax
from jax.experimental import pallas as pl
from jax.experimental.pallas import tpu as pltpu

F32 = jnp.float32
BF16 = jnp.bfloat16

D_MODEL = 2048
DEPTH = 4
GRID_W = 64
N_MIXERS = 4
EPS = 1e-6
DA_HEADS = 16
DA_HALF = D_MODEL // DA_HEADS // 2
DA_VDIM = 2 * DA_HALF
ROPE_BASE = 10000.0
GDN_HEADS = 16
GDN_DK = D_MODEL // GDN_HEADS
GDN_DV = D_MODEL // GDN_HEADS
GDN_CHUNK = 64
NA_HEADS = 16
NA_DIM = D_MODEL // NA_HEADS
WIN_R = 8
WIN_C = 16
HG_HEADS = 16
HG_DK = D_MODEL // HG_HEADS
HG_DV = D_MODEL // HG_HEADS
HG_CHUNK = 32
N_EXPERTS = 16
N_GROUPS = 4
EXPERTS_PER_GROUP = N_EXPERTS // N_GROUPS
GROUP_SCORE_TOPK = 2
TOP_K = 2
D_EXPERT = D_MODEL // 2

V7X_VMEM_LIMIT_BYTES = 56 * 1024 * 1024
LANES = 128
MOE_TILE = 512
ROUTER_PAD = LANES


def _largest_divisor(n, candidates):
    for c in candidates:
        if n % c == 0:
            return c
    raise ValueError(f"no tile in {candidates} divides {n}")


def _cparams(*sem):
    return pltpu.CompilerParams(dimension_semantics=sem, vmem_limit_bytes=V7X_VMEM_LIMIT_BYTES)


def _mm_kernel(a_ref, w_ref, o_ref):
    o_ref[...] = jnp.dot(a_ref[...], w_ref[...], preferred_element_type=F32).astype(o_ref.dtype)


def mm(a, w, out_dtype=F32):
    m, k = a.shape
    n = w.shape[1]
    n_pad = -n % LANES
    if n_pad:
        w = jnp.pad(w, ((0, 0), (0, n_pad)))
    a = a.astype(BF16)
    w = w.astype(BF16)
    np_ = n + n_pad
    tm = _largest_divisor(m, (1024, 512, 256, 128))
    tn = _largest_divisor(np_, (512, 256, 128))
    out = pl.pallas_call(
        _mm_kernel,
        out_shape=jax.ShapeDtypeStruct((m, np_), out_dtype),
        grid=(m // tm, np_ // tn),
        in_specs=[pl.BlockSpec((tm, k), lambda i, j: (i, 0)),
                  pl.BlockSpec((k, tn), lambda i, j: (0, j))],
        out_specs=pl.BlockSpec((tm, tn), lambda i, j: (i, j)),
        compiler_params=_cparams("parallel", "arbitrary"),
        name="proj_mm",
    )(a, w)
    return out[:, :n] if n_pad else out


def _normmod_kernel(x_ref, g_ref, sc_ref, sh_ref, o_ref):
    x = x_ref[0].astype(F32)
    y = x * lax.rsqrt(jnp.mean(x * x, axis=-1, keepdims=True) + EPS)
    o_ref[0] = ((y * g_ref[...]) * (1.0 + sc_ref[0]) + sh_ref[0]).astype(o_ref.dtype)


def _normmod_router_kernel(x_ref, g_ref, sc_ref, sh_ref, wr_ref, o_ref, aff_ref):
    x = x_ref[0].astype(F32)
    y = x * lax.rsqrt(jnp.mean(x * x, axis=-1, keepdims=True) + EPS)
    h = (y * g_ref[...]) * (1.0 + sc_ref[0]) + sh_ref[0]
    o_ref[0] = h.astype(o_ref.dtype)
    logits = jnp.dot(h, wr_ref[...], preferred_element_type=F32, precision=lax.Precision.HIGHEST)
    aff_ref[0] = jax.nn.sigmoid(logits)


def norm_modulate(x, g, sc, sh, w_router=None):
    b, t, d = x.shape
    tm = _largest_divisor(t, (512, 256, 128))
    x_spec = pl.BlockSpec((1, tm, d), lambda i, j: (i, j, 0))
    g_spec = pl.BlockSpec((1, d), lambda i, j: (0, 0))
    mod_spec = pl.BlockSpec((1, 1, d), lambda i, j: (i, 0, 0))
    g2 = g.reshape(1, d).astype(F32)
    if w_router is None:
        return pl.pallas_call(
            _normmod_kernel,
            out_shape=jax.ShapeDtypeStruct((b, t, d), BF16),
            grid=(b, t // tm),
            in_specs=[x_spec, g_spec, mod_spec, mod_spec],
            out_specs=x_spec,
            compiler_params=_cparams("parallel", "parallel"),
            name="norm_modulate",
        )(x, g2, sc, sh)
    wr = jnp.pad(w_router.astype(F32), ((0, 0), (0, ROUTER_PAD - w_router.shape[1])))
    return pl.pallas_call(
        _normmod_router_kernel,
        out_shape=(jax.ShapeDtypeStruct((b, t, d), BF16),
                   jax.ShapeDtypeStruct((b, t, ROUTER_PAD), F32)),
        grid=(b, t // tm),
        in_specs=[x_spec, g_spec, mod_spec, mod_spec,
                  pl.BlockSpec((d, ROUTER_PAD), lambda i, j: (0, 0))],
        out_specs=(x_spec, pl.BlockSpec((1, tm, ROUTER_PAD), lambda i, j: (i, j, 0))),
        compiler_params=_cparams("parallel", "parallel"),
        name="norm_modulate_router",
    )(x, g2, sc, sh, wr)


def _rmsnorm_kernel(x_ref, g_ref, o_ref):
    x = x_ref[0].astype(F32)
    y = x * lax.rsqrt(jnp.mean(x * x, axis=-1, keepdims=True) + EPS)
    o_ref[0] = y * g_ref[...]


def final_rmsnorm(x, g):
    b, t, d = x.shape
    tm = _largest_divisor(t, (512, 256, 128))
    spec = pl.BlockSpec((1, tm, d), lambda i, j: (i, j, 0))
    return pl.pallas_call(
        _rmsnorm_kernel,
        out_shape=jax.ShapeDtypeStruct((b, t, d), F32),
        grid=(b, t // tm),
        in_specs=[spec, pl.BlockSpec((1, d), lambda i, j: (0, 0))],
        out_specs=spec,
        compiler_params=_cparams("parallel", "parallel"),
        name="final_rmsnorm",
    )(x, g.reshape(1, d).astype(F32))


def _moe_kernel(te_ref, tv_ref, x_ref, wg_ref, wu_ref, wd_ref, o_ref):
    del te_ref
    i = pl.program_id(0)

    @pl.when(tv_ref[i] > 0)
    def _():
        x = x_ref[...]
        g = jnp.dot(x, wg_ref[0], preferred_element_type=F32)
        u = jnp.dot(x, wu_ref[0], preferred_element_type=F32)
        act = (g * jax.nn.sigmoid(g) * u).astype(BF16)
        o_ref[...] = jnp.dot(act, wd_ref[0], preferred_element_type=F32)

    @pl.when(tv_ref[i] == 0)
    def _():
        o_ref[...] = jnp.zeros_like(o_ref)


def _route(aff, b_router):
    assert GROUP_SCORE_TOPK == 2 and TOP_K == 2
    epg = EXPERTS_PER_GROUP
    sel = aff + b_router.astype(F32)
    s = [sel[:, e] for e in range(N_EXPERTS)]
    a = [aff[:, e] for e in range(N_EXPERTS)]

    def first_max(vals):
        idx, best = jnp.zeros_like(vals[0], dtype=jnp.int32), vals[0]
        for e in range(1, len(vals)):
            upd = vals[e] > best
            idx, best = jnp.where(upd, e, idx), jnp.where(upd, vals[e], best)
        return idx, best

    def pick(vals, idx):
        out = vals[0]
        for e in range(1, len(vals)):
            out = jnp.where(idx == e, vals[e], out)
        return out

    def top2_sum(v):
        pairs = [v[i] + v[j] for i in range(len(v)) for j in range(i + 1, len(v))]
        return functools.reduce(jnp.maximum, pairs)

    g_best, _ = first_max([top2_sum(s[g * epg:(g + 1) * epg]) for g in range(N_GROUPS)])
    in_s = [pick([s[g * epg + e] for g in range(N_GROUPS)], g_best) for e in range(epg)]
    in_a = [pick([a[g * epg + e] for g in range(N_GROUPS)], g_best) for e in range(epg)]
    i1, _ = first_max(in_s)
    i2, _ = first_max([jnp.where(i1 == e, NEG_INF, in_s[e]) for e in range(epg)])
    w1, w2 = pick(in_a, i1), pick(in_a, i2)
    tot = w1 + w2
    expert_idx = jnp.stack([g_best * epg + i1, g_best * epg + i2], axis=-1)
    return expert_idx.astype(jnp.int32), jnp.stack([w1 / tot, w2 / tot], axis=-1)


def grouped_moe(h2, aff, b_router, w_gate, w_up, w_down):
    n, d = h2.shape
    expert_idx, w = _route(aff, b_router)
    n_slots = TOP_K * n
    e_flat = expert_idx.reshape(-1)
    onehot = (e_flat[:, None] == jnp.arange(N_EXPERTS, dtype=jnp.int32)[None, :]).astype(jnp.int32)
    csum = jnp.cumsum(onehot, axis=0)
    rank = jnp.take_along_axis(csum, e_flat[:, None], axis=1)[:, 0] - 1
    counts = csum[-1]
    padded = ((counts + MOE_TILE - 1) // MOE_TILE) * MOE_TILE
    pad_end = jnp.cumsum(padded)
    pad_off = pad_end - padded
    dest_flat = pad_off[e_flat] + rank
    dest = dest_flat.reshape(n, TOP_K)

    n_tiles = -(-n_slots // MOE_TILE) + N_EXPERTS
    n_pad_slots = n_tiles * MOE_TILE
    tile_start = jnp.arange(n_tiles, dtype=jnp.int32) * MOE_TILE
    tile_expert = jnp.minimum(jnp.searchsorted(pad_end, tile_start, side="right"), N_EXPERTS - 1).astype(jnp.int32)
    tile_valid = (tile_start < pad_end[-1]).astype(jnp.int32)

    slot_token = jnp.zeros((n_pad_slots,), jnp.int32).at[dest_flat].set(
        jnp.arange(n_slots, dtype=jnp.int32) // TOP_K, unique_indices=True)

    xs = jnp.take(h2, slot_token, axis=0)
    f = D_EXPERT
    ys = pl.pallas_call(
        _moe_kernel,
        out_shape=jax.ShapeDtypeStruct((n_pad_slots, d), F32),
        grid_spec=pltpu.PrefetchScalarGridSpec(
            num_scalar_prefetch=2,
            grid=(n_tiles,),
            in_specs=[pl.BlockSpec((MOE_TILE, d), lambda i, te, tv: (i, 0)),
                      pl.BlockSpec((1, d, f), lambda i, te, tv: (te[i], 0, 0)),
                      pl.BlockSpec((1, d, f), lambda i, te, tv: (te[i], 0, 0)),
                      pl.BlockSpec((1, f, d), lambda i, te, tv: (te[i], 0, 0))],
            out_specs=pl.BlockSpec((MOE_TILE, d), lambda i, te, tv: (i, 0)),
        ),
        compiler_params=_cparams("arbitrary"),
        name="moe_experts",
    )(tile_expert, tile_valid, xs, w_gate.astype(BF16), w_up.astype(BF16), w_down.astype(BF16))
    out = jnp.take(ys, dest[:, 0], axis=0) * w[:, 0:1] + jnp.take(ys, dest[:, 1], axis=0) * w[:, 1:2]
    return out


def _diff_attn_kernel(lam_ref, q_ref, k_ref, v_ref, g_ref, o_ref, *, n_ctx, lam_init):
    i = pl.program_id(2)
    lam = lam_ref[0]
    lane = lax.broadcasted_iota(jnp.int32, (1, 2 * DA_HALF), 1)
    q = q_ref[0]
    q1 = jnp.where(lane < DA_HALF, q, jnp.zeros_like(q))
    q2 = jnp.where(lane >= DA_HALF, q, jnp.zeros_like(q))
    nt = (((1,), (1,)), ((), ()))

    def attend(k, v):
        def softmax(s):
            m = jnp.max(s, axis=-1, keepdims=True)
            p = jnp.exp(s - m)
            return p * (1.0 / jnp.sum(p, axis=-1, keepdims=True))

        p1 = softmax(lax.dot_general(q1, k, nt, preferred_element_type=F32))
        p2 = softmax(lax.dot_general(q2, k, nt, preferred_element_type=F32))
        p = (p1 - lam * p2).astype(BF16)
        o = jnp.dot(p, v, preferred_element_type=F32)
        o = o * lax.rsqrt(jnp.mean(o * o, axis=-1, keepdims=True) + 1e-5)
        o_ref[0] = ((o * g_ref[...]) * (1.0 - lam_init)).astype(o_ref.dtype)

    @pl.when(i == 0)
    def _():
        attend(k_ref[0, :n_ctx, :], v_ref[0, :n_ctx, :])

    @pl.when(i > 0)
    def _():
        attend(k_ref[0], v_ref[0])


def diff_attn_core(qkv, lam, subln_g, n_ctx, lam_init):
    b, t, _ = qkv.shape
    tq = n_ctx
    nh = DA_HEADS
    hw = 2 * DA_HALF
    kern = functools.partial(_diff_attn_kernel, n_ctx=n_ctx, lam_init=lam_init)
    return pl.pallas_call(
        kern,
        out_shape=jax.ShapeDtypeStruct((b, t, nh * hw), BF16),
        grid=(b, nh, t // tq),
        in_specs=[pl.BlockSpec(memory_space=pltpu.SMEM),
                  pl.BlockSpec((1, tq, hw), lambda bi, h, i: (bi, i, h)),
                  pl.BlockSpec((1, t, hw), lambda bi, h, i: (bi, 0, nh + h)),
                  pl.BlockSpec((1, t, hw), lambda bi, h, i: (bi, 0, 2 * nh + h)),
                  pl.BlockSpec((1, hw), lambda bi, h, i: (0, 0))],
        out_specs=pl.BlockSpec((1, tq, hw), lambda bi, h, i: (bi, i, h)),
        compiler_params=_cparams("parallel", "parallel", "arbitrary"),
        name="diff_attn",
    )(lam.reshape(1).astype(F32), qkv, qkv, qkv, subln_g.reshape(1, hw).astype(F32))


NA_QROWS = 8
NA_KROWS = 2 * NA_QROWS
NA_DR_PAD = 16
NEG_INF = float("-inf")
NA_BIAS_ROWS = 2 * WIN_R - 1
NA_BIAS_COLS = 2 * WIN_C - 1


def _na_kernel(rpb_ref, q_ref, k_ref, v_ref, o_ref, bias_ref, *, n_ctx, rows):
    h = pl.program_id(0)
    b = pl.program_id(1)
    j = pl.program_id(2)
    w = GRID_W
    nq = NA_QROWS * w
    nk = NA_KROWS * w
    lane = lax.broadcasted_iota(jnp.int32, (w, 2 * w), 1)
    nt = (((1,), (1,)), ((), ()))

    @pl.when((b == 0) & (j == 0))
    def _():
        c = lax.broadcasted_iota(jnp.int32, (w, 2 * w), 0)
        kc = lane % w
        c0 = jnp.clip(c - WIN_C // 2, 0, w - WIN_C)
        in_win = (kc >= c0) & (kc < c0 + WIN_C)
        single = []
        for dr in range(-(WIN_R - 1), WIN_R):
            t = jnp.full((w, 2 * w), NEG_INF, F32)
            for dc in range(-(WIN_C - 1), WIN_C):
                t = jnp.where(kc - c == dc, rpb_ref[(h * NA_BIAS_ROWS + dr + WIN_R - 1) * NA_BIAS_COLS + dc + WIN_C - 1], t)
            single.append(jnp.where(in_win, t, NEG_INF))
        neg = jnp.full((w, 2 * w), NEG_INF, F32)

        def at(dr):
            return single[dr + WIN_R - 1] if abs(dr) < WIN_R else neg

        for d in range(2 * NA_DR_PAD + 1):
            bias_ref[d] = jnp.where(lane < w, at(d - NA_DR_PAD), at(d - NA_DR_PAD + 1))

    kctx = k_ref[0, :n_ctx, :]
    vctx = v_ref[0, :n_ctx, :]

    def finish(parts, q_lo, n):
        m = parts[0][0].max(axis=-1, keepdims=True)
        for s, _ in parts[1:]:
            m = jnp.maximum(m, s.max(axis=-1, keepdims=True))
        l = 0.0
        o = 0.0
        for s, vv in parts:
            p = jnp.exp(s - m)
            l = l + p.sum(axis=-1, keepdims=True)
            o = o + jnp.dot(p.astype(BF16), vv, preferred_element_type=F32)
        o_ref[0, pl.ds(q_lo, n), :] = (o * (1.0 / l)).astype(o_ref.dtype)

    @pl.when(j < rows // NA_QROWS)
    def _():
        r_lo = j * NA_QROWS
        ks = jnp.clip(r_lo - WIN_R // 2, 0, rows - NA_KROWS)
        q_lo = pl.multiple_of(n_ctx + r_lo * w, w)
        k_lo = pl.multiple_of(n_ctx + ks * w, w)
        q = q_ref[0, pl.ds(q_lo, nq), :]
        kwin = k_ref[0, pl.ds(k_lo, nk), :]
        vwin = v_ref[0, pl.ds(k_lo, nk), :]
        s_win = lax.dot_general(q, kwin, nt, preferred_element_type=F32)
        half = (lane >= w).astype(jnp.int32)
        row_blocks = []
        for i in range(NA_QROWS):
            r = r_lo + i
            r0 = jnp.clip(r - WIN_R // 2, 0, rows - WIN_R)
            tiles = []
            for jj in range(NA_KROWS // 2):
                kr = ks + 2 * jj
                off = lax.bitcast_convert_type(half + (kr - r0), jnp.uint32)
                t = s_win[i * w:(i + 1) * w, 2 * jj * w:(2 * jj + 2) * w] + bias_ref[kr - r + NA_DR_PAD]
                tiles.append(jnp.where(off < WIN_R, t, NEG_INF))
            row_blocks.append(jnp.concatenate(tiles, axis=1))
        s_win = jnp.concatenate(row_blocks, axis=0)
        s_ctx = lax.dot_general(q, kctx, nt, preferred_element_type=F32)
        finish([(s_win, vwin), (s_ctx, vctx)], q_lo, nq)

    @pl.when(j == rows // NA_QROWS)
    def _():
        q = q_ref[0, :n_ctx, :]
        finish([(lax.dot_general(q, kctx, nt, preferred_element_type=F32), vctx)], 0, n_ctx)


def na_attn_core(qkv, rpb, n_ctx):
    b, t, _ = qkv.shape
    rows = (t - n_ctx) // GRID_W
    assert rows % NA_QROWS == 0 and rows >= NA_KROWS
    nh, hd = NA_HEADS, NA_DIM
    kern = functools.partial(_na_kernel, n_ctx=n_ctx, rows=rows)

    def col(off):
        return pl.BlockSpec((1, t, hd), lambda h, bi, j: (bi, 0, off + h))

    return pl.pallas_call(
        kern,
        out_shape=jax.ShapeDtypeStruct((b, t, nh * hd), BF16),
        grid=(nh, b, rows // NA_QROWS + 1),
        in_specs=[pl.BlockSpec(memory_space=pltpu.SMEM), col(0), col(nh), col(2 * nh)],
        out_specs=col(0),
        scratch_shapes=[pltpu.VMEM((2 * NA_DR_PAD + 1, GRID_W, 2 * GRID_W), F32)],
        compiler_params=_cparams("arbitrary", "arbitrary", "arbitrary"),
        name="na_attn",
    )(rpb.astype(F32).reshape(-1), qkv, qkv, qkv)


def rmsnorm(x, g, eps=EPS):
    xf = x.astype(F32)
    y = xf * lax.rsqrt(jnp.mean(xf * xf, axis=-1, keepdims=True) + eps)
    return y.astype(x.dtype) * g


def l2norm(x):
    return x * lax.rsqrt(jnp.sum(x * x, axis=-1, keepdims=True) + EPS)


def flip(t):
    return None if t is None else jnp.flip(t, axis=1)


def proj(t, w, out_dtype=F32):
    b, tt, k = t.shape
    return mm(t.reshape(b * tt, k), w, out_dtype).reshape(b, tt, w.shape[1])


def axial_rope_tables(length, dim):
    n_freq = dim // 4
    t = jnp.arange(length)
    pos = jnp.stack([t // GRID_W, t % GRID_W], axis=-1).astype(F32)
    inv = ROPE_BASE ** (-jnp.arange(n_freq, dtype=F32) / n_freq)
    ang = pos[:, :, None] * inv
    return jnp.cos(ang), jnp.sin(ang)


def apply_rope(x, cos, sin):
    n_freq = cos.shape[-1]
    xr = x.reshape(*x.shape[:-1], 2, 2, n_freq)
    x1, x2 = xr[..., 0, :], xr[..., 1, :]
    c = cos[:, None].astype(x.dtype)
    s = sin[:, None].astype(x.dtype)
    out = jnp.stack([x1 * c - x2 * s, x1 * s + x2 * c], axis=-2)
    return out.reshape(x.shape)


def diff_attention(h, hc, w_qkv, w_o, lam_vec, subln_g, layer_idx):
    b, l, d = h.shape
    tc = hc.shape[1]
    dq = 2 * DA_HEADS * DA_HALF
    lam_init = 0.8 - 0.6 * math.exp(-0.3 * layer_idx)
    lv = lam_vec.astype(F32)
    lam = jnp.exp(jnp.sum(lv[0] * lv[1])) - jnp.exp(jnp.sum(lv[2] * lv[3])) + lam_init
    scale = DA_HALF ** -0.5

    qkv = proj(h, w_qkv)
    qkv_c = proj(hc, w_qkv)
    cos, sin = axial_rope_tables(l, DA_HALF)
    q = apply_rope(qkv[..., :dq].reshape(b, l, 2 * DA_HEADS, DA_HALF), cos, sin).reshape(b, l, dq) * scale
    k = apply_rope(qkv[..., dq:2 * dq].reshape(b, l, 2 * DA_HEADS, DA_HALF), cos, sin).reshape(b, l, dq)
    lat = jnp.concatenate([q, k, qkv[..., 2 * dq:]], axis=-1).astype(BF16)
    ctx = jnp.concatenate([qkv_c[..., :dq] * scale, qkv_c[..., dq:]], axis=-1).astype(BF16)
    o = diff_attn_core(jnp.concatenate([ctx, lat], axis=1), lam, subln_g, tc, lam_init)
    y = proj(o, w_o)
    return y[:, tc:], y[:, :tc]


def centred_dwconv(x, w):
    k = w.shape[0]
    pad = k // 2
    t = x.shape[1]
    xp = jnp.pad(x, ((0, 0), (pad, pad), (0, 0)))
    out = xp[:, 0:t] * w[0]
    for j in range(1, k):
        out = out + xp[:, j:j + t] * w[j]
    return out


GDN_HEAD_GROUP = 16
GDN_SUB = 16


def _bdot(a, b):
    return jnp.dot(a.astype(BF16), b.astype(BF16), preferred_element_type=F32)


def _scan_chunk(d, s, n, n_ctx_chunks):
    back = jnp.where(s < n_ctx_chunks, n_ctx_chunks - 1 - s, n + n_ctx_chunks - 1 - s)
    return jnp.where(d == 0, s, back)


def _gdn_kernel(sdec_ref, qs_ref, k_ref, qg_ref, kb_ref, kbg_ref, kd_ref, vb_ref, gc_ref, gr_ref, o_ref, s_ref,
                *, n_ctx_chunks):
    d = pl.program_id(0)
    b = pl.program_id(1)
    hg = pl.program_id(2)
    s = pl.program_id(3)
    n = pl.num_programs(3)
    c = GDN_CHUNK

    @pl.when(s == 0)
    def _():
        s_ref[...] = jnp.zeros_like(s_ref)

    row = lax.broadcasted_iota(jnp.int32, (c, c), 0)
    col = lax.broadcasted_iota(jnp.int32, (c, c), 1)
    ahead = (row - col) * (1 - 2 * d)
    strict = ahead > 0
    incl = ahead >= 0
    same_blk = (row // GDN_SUB) == (col // GDN_SUB)
    eye = (row == col).astype(F32)
    nt = (((1,), (1,)), ((), ()))
    tn = (((0,), (0,)), ((), ()))
    chunk = _scan_chunk(d, s, n, n_ctx_chunks)
    sdec_base = ((d * pl.num_programs(1) + b) * n + chunk) * GDN_HEADS + hg * GDN_HEAD_GROUP
    heads = range(GDN_HEAD_GROUP)
    sls = [slice(g * GDN_DK, (g + 1) * GDN_DK) for g in heads]

    def each(fn, *lists):
        return [fn(*vals) for vals in zip(*lists)]

    ks = [k_ref[:, sl] for sl in sls]
    decay = [jnp.exp(jnp.where(incl, gc_ref[:, g:g + 1] - gr_ref[g:g + 1, :], NEG_INF)) for g in heads]
    a = each(lambda sl, k, dc: jnp.where(
        strict, lax.dot_general(kb_ref[:, sl], k, nt, preferred_element_type=F32) * dc, 0.0), sls, ks, decay)
    qk = each(lambda sl, k, dc: jnp.where(
        incl, lax.dot_general(qs_ref[:, sl], k, nt, preferred_element_type=F32) * dc, 0.0), sls, ks, decay)
    dblk = each(lambda m: jnp.where(same_blk, m, 0.0), a)
    d2 = each(lambda m: _bdot(m, m), dblk)
    x = each(lambda m, m2: _bdot(eye - m, eye + m2), dblk, d2)
    d4 = each(lambda m: _bdot(m, m), d2)
    x = each(lambda m, m4: _bdot(m, eye + m4), x, d4)
    d8 = each(lambda m: _bdot(m, m), d4)
    x = each(lambda m, m8: _bdot(m, eye + m8), x, d8)
    nmat = each(lambda m, am, dm: _bdot(m, am - dm), x, a, dblk)
    n2 = each(lambda m: _bdot(m, m), nmat)
    y = each(lambda m, m2: _bdot(eye - m, eye + m2), nmat, n2)
    t = each(_bdot, y, x)
    uw = each(lambda m, sl: _bdot(m, jnp.concatenate([vb_ref[:, sl], kbg_ref[:, sl]], axis=1)), t, sls)
    state = [s_ref[g] for g in heads]
    v_new = each(lambda m, st: m[:, :GDN_DV] - _bdot(m[:, GDN_DV:], st), uw, state)
    o_state = each(lambda sl, st: _bdot(qg_ref[:, sl], st), sls, state)
    o_new = each(_bdot, qk, v_new)
    s_new = each(lambda sl, vn: lax.dot_general(kd_ref[:, sl], vn.astype(BF16), tn, preferred_element_type=F32),
                 sls, v_new)
    for g in heads:
        s_ref[g] = state[g] * sdec_ref[sdec_base + g] + s_new[g]
    o_ref[...] = jnp.concatenate(each(lambda p, q: p + q, o_state, o_new), axis=1)


def gated_delta_core(qs, k, qg, kb, kbg, kd, vb, g, g_last, n_ctx_chunks):
    b, t, hd = k.shape
    c, hh, gg = GDN_CHUNK, GDN_HEADS, GDN_HEAD_GROUP
    n = t // c
    gw = gg * GDN_DK
    g_col = g.reshape(2, b, t, hh // gg, gg).transpose(0, 1, 3, 2, 4)
    g_row = g.transpose(0, 1, 2, 4, 3).reshape(2, b, n, hh // gg, gg, c)
    sdec = jnp.exp(g_last).reshape(-1)

    def chunk_of(d, s):
        return _scan_chunk(d, s, n, n_ctx_chunks)

    shared = pl.BlockSpec((None, c, gw), lambda d, bi, hg, s, sd: (bi, chunk_of(d, s), hg))
    per_dir = pl.BlockSpec((None, None, c, gw), lambda d, bi, hg, s, sd: (d, bi, chunk_of(d, s), hg))
    kern = functools.partial(_gdn_kernel, n_ctx_chunks=n_ctx_chunks)
    return pl.pallas_call(
        kern,
        out_shape=jax.ShapeDtypeStruct((2, b, t, hd), F32),
        grid_spec=pltpu.PrefetchScalarGridSpec(
            num_scalar_prefetch=1,
            grid=(2, b, hh // gg, n),
            in_specs=[shared, shared, per_dir, per_dir, per_dir, per_dir, per_dir,
                      pl.BlockSpec((None, None, None, c, gg), lambda d, bi, hg, s, sd: (d, bi, hg, chunk_of(d, s), 0)),
                      pl.BlockSpec((None, None, None, None, gg, c),
                                   lambda d, bi, hg, s, sd: (d, bi, chunk_of(d, s), hg, 0, 0))],
            out_specs=per_dir,
            scratch_shapes=[pltpu.VMEM((gg, GDN_DK, GDN_DV), F32)],
        ),
        compiler_params=_cparams("parallel", "parallel", "parallel", "arbitrary"),
        name="gated_delta",
    )(sdec, qs, k, qg, kb, kbg, kd, vb, g_col, g_row)


def gated_deltanet(h, hc, w_in, conv_w, w_ab, dt_bias, a_log, norm_g, w_o):
    B = h.shape[0]
    tc = hc.shape[1]
    H, C = GDN_HEADS, GDN_CHUNK
    wq = H * GDN_DK
    hi = 2 * wq + H * GDN_DV

    def feats(t):
        T = t.shape[1]
        zz = proj(t, w_in)
        z = jax.nn.silu(centred_dwconv(zz[..., :hi], conv_w)).astype(F32)
        gate = jax.nn.silu(zz[..., hi:])
        q = l2norm(z[..., :wq].reshape(B, T, H, GDN_DK))
        k = l2norm(z[..., wq:2 * wq].reshape(B, T, H, GDN_DK))
        v = z[..., 2 * wq:].reshape(B, T, H, GDN_DV)
        ab = proj(t, w_ab).astype(F32).reshape(B, T, 2, 2, H)
        log_a = -jnp.exp(a_log.astype(F32)) * jax.nn.softplus(ab[:, :, 0] + dt_bias.astype(F32))
        beta = jax.nn.sigmoid(ab[:, :, 1])
        return q, k, v, log_a, beta, gate

    q, k, v, la, be, gate = [jnp.concatenate([a, b], axis=1) for a, b in zip(feats(hc), feats(h))]
    T = q.shape[1]
    n = T // C
    la_c = la.reshape(B, n, C, 2, H)
    g_f = jnp.cumsum(la_c[:, :, :, 0], axis=2)
    g_b = jnp.flip(jnp.cumsum(jnp.flip(la_c[:, :, :, 1], axis=2), axis=2), axis=2)
    g = jnp.stack([g_f, g_b])
    g_last = jnp.stack([g_f[:, :, -1], g_b[:, :, 0]])
    be_d = jnp.moveaxis(be, 2, 0)[..., None]
    eg = jnp.exp(g).reshape(2, B, T, H, 1)
    ekd = jnp.exp(g_last[:, :, :, None] - g).reshape(2, B, T, H, 1)

    def flat(a):
        return a.astype(BF16).reshape(*a.shape[:-2], -1)

    qs = q * GDN_DK ** -0.5
    kb = k[None] * be_d
    o = gated_delta_core(flat(qs), flat(k), flat(qs[None] * eg), flat(kb), flat(kb * eg), flat(k[None] * ekd),
                         flat(v[None] * be_d), g, g_last, tc // C)
    o = (o[0] + o[1]).reshape(B, T, H, GDN_DV)
    o = rmsnorm(o, norm_g).astype(F32) * gate.reshape(B, T, H, GDN_DV)
    y = proj(o.reshape(B, T, -1), w_o)
    return y[:, tc:], y[:, :tc]


def neighbourhood_attention(h, hc, w_qkv, rpb, w_o):
    tc = hc.shape[1]
    hd = NA_HEADS * NA_DIM
    scale = NA_DIM ** -0.5

    def scaled_qkv(t):
        qkv = proj(t, w_qkv)
        return jnp.concatenate([qkv[..., :hd] * scale, qkv[..., hd:]], axis=-1).astype(BF16)

    o = na_attn_core(jnp.concatenate([scaled_qkv(hc), scaled_qkv(h)], axis=1), rpb, tc)
    y = proj(o, w_o)
    return y[:, tc:], y[:, :tc]


GLA_CHUNK = 64
GLA_LEVELS = (1, 2, 4, 8, 16, 32)


def _gla_kernel(q_ref, k_ref, v_ref, b_ref, o_ref, st_ref, *, reverse):
    s = pl.program_id(1)
    c = GLA_CHUNK

    @pl.when(s == 0)
    def _():
        st_ref[...] = jnp.zeros_like(st_ref)

    row = lax.broadcasted_iota(jnp.int32, (c, c), 0)
    col = lax.broadcasted_iota(jnp.int32, (c, c), 1)
    late, early = (col, row) if reverse else (row, col)
    masks = []
    for m in GLA_LEVELS:
        masks.append(((row // (2 * m)) == (col // (2 * m))) & ((late % (2 * m)) >= m) & ((early % (2 * m)) < m))
    diag = row == col
    trow = lax.broadcasted_iota(jnp.int32, (c, HG_DK), 0)
    nt = (((1,), (1,)), ((), ()))
    tn = (((0,), (0,)), ((), ()))
    heads = range(HG_HEADS)
    sls = [slice(g * HG_DK, (g + 1) * HG_DK) for g in heads]

    def ref_rows(b, m):
        p = m if reverse else m - 1
        if 2 * m >= 8:
            blocks = b.reshape(c // (2 * m), 2 * m, HG_DK)
            return jnp.broadcast_to(blocks[:, p:p + 1, :], blocks.shape).reshape(c, HG_DK)
        out = b
        for rho in range(2 * m):
            if rho != p:
                out = jnp.where(trow % (2 * m) == rho, pltpu.roll(b, (rho - p) % c, axis=0), out)
        return out

    qs = [q_ref[:, sl].astype(F32) for sl in sls]
    ks = [k_ref[:, sl].astype(F32) for sl in sls]
    bs = [b_ref[:, sl] for sl in sls]
    att = [jnp.where(diag, lax.dot_general(q_ref[:, sl], k_ref[:, sl], nt, preferred_element_type=F32), 0.0)
           for sl in sls]
    for m, mask in zip(GLA_LEVELS, masks):
        es = [jnp.exp(-jnp.abs(b - ref_rows(b, m))) for b in bs]
        sc = [lax.dot_general((q * e).astype(BF16), (k * e).astype(BF16), nt, preferred_element_type=F32)
              for q, k, e in zip(qs, ks, es)]
        att = [jnp.where(mask, x, a) for x, a in zip(sc, att)]
    last = c - 1 if not reverse else 0
    b_last = [b[last:last + 1, :] for b in bs]
    state = [st_ref[g] for g in heads]
    o_state = [lax.dot_general((q * jnp.exp(b)).astype(BF16), st.astype(BF16), nt, preferred_element_type=F32)
               for q, b, st in zip(qs, bs, state)]
    o_new = [jnp.dot(a.astype(BF16), v_ref[:, sl], preferred_element_type=F32) for a, sl in zip(att, sls)]
    s_new = [lax.dot_general(v_ref[:, sl], (k * jnp.exp(bl - b)).astype(BF16), tn, preferred_element_type=F32)
             for sl, k, b, bl in zip(sls, ks, bs, b_last)]
    for g in heads:
        st_ref[g] = state[g] * jnp.exp(b_last[g]) + s_new[g]
    o_ref[...] = jnp.concatenate([x + y for x, y in zip(o_state, o_new)], axis=1)


def gla_core(q, k, v, b, n_ctx_chunks, reverse):
    bsz, t, hd = q.shape
    c = GLA_CHUNK
    n = t // c

    def chunk_of(s):
        return _scan_chunk(1, s, n, n_ctx_chunks) if reverse else s

    spec = pl.BlockSpec((None, c, hd), lambda bi, s: (bi, chunk_of(s), 0))
    return pl.pallas_call(
        functools.partial(_gla_kernel, reverse=reverse),
        out_shape=jax.ShapeDtypeStruct((bsz, t, hd), F32),
        grid=(bsz, n),
        in_specs=[spec, spec, spec, spec],
        out_specs=spec,
        scratch_shapes=[pltpu.VMEM((HG_HEADS, HG_DV, HG_DK), F32)],
        compiler_params=_cparams("parallel", "arbitrary"),
        name="gla_bwd" if reverse else "gla_fwd",
    )(q, k, v, b)


def hgrn2(h, hc, w_q, w_i, w_f, b_f, w_g, norm_g, w_o, lb):
    B = h.shape[0]
    tc = hc.shape[1]
    lbf = lb.astype(F32)
    C = GLA_CHUNK
    hw = HG_HEADS * HG_DK

    def feats(t):
        zz = proj(t, jnp.concatenate([w_q, w_i, w_f[0], w_f[1], w_g], axis=1))
        q = jax.nn.silu(zz[..., :hw]) * HG_DK ** -0.5
        v = zz[..., hw:2 * hw]
        z = jnp.stack([zz[..., 2 * hw:3 * hw], zz[..., 3 * hw:4 * hw]], axis=0) + b_f[:, None, None, :]
        log_f = jnp.log(lbf + (1.0 - lbf) * jax.nn.sigmoid(z))
        k = (1.0 - lbf) * jax.nn.sigmoid(-z)
        gate = jax.nn.sigmoid(zz[..., 4 * hw:])
        return q, v, k, log_f, gate

    fc, fl = feats(hc), feats(h)
    q, v, gate = [jnp.concatenate([fc[i], fl[i]], axis=1) for i in (0, 1, 4)]
    k = jnp.concatenate([fc[2], fl[2]], axis=2)
    log_f = jnp.concatenate([fc[3], fl[3]], axis=2)
    T = q.shape[1]
    lf_c = log_f.reshape(2, B, T // C, C, hw)
    b_f_ = jnp.cumsum(lf_c[0], axis=2).reshape(B, T, hw)
    b_b_ = jnp.flip(jnp.cumsum(jnp.flip(lf_c[1], axis=2), axis=2), axis=2).reshape(B, T, hw)
    qb, vb = q.astype(BF16), v.astype(BF16)
    o = (gla_core(qb, k[0].astype(BF16), vb, b_f_, tc // C, False)
         + gla_core(qb, k[1].astype(BF16), vb, b_b_, tc // C, True))
    o = o.reshape(B, T, HG_HEADS, HG_DV)
    o = rmsnorm(o, norm_g).astype(F32) * gate.reshape(B, T, HG_HEADS, HG_DV)
    y = proj(o.reshape(B, T, -1), w_o)
    return y[:, tc:], y[:, :tc]


def kernel(x, c, ctx, c_ctx, w_mod, b_mod, norm_mix_g, norm_ffn_g, da_w_qkv, da_w_o, da_lam, da_subln_g, gdn_w_in, gdn_conv, gdn_w_ab, gdn_dt_bias, gdn_a_log, gdn_norm_g, gdn_w_o, na_w_qkv, na_rpb, na_w_o, hg_w_q, hg_w_i, hg_w_f, hg_b_f, hg_w_g, hg_norm_g, hg_w_o, hg_lb_logits, w_router, b_router, e_w_gate, e_w_up, e_w_down, final_norm_g):
    B, L, D = x.shape
    tc = ctx.shape[1]
    xc = ctx
    cond = jax.nn.silu(c)
    cond_ctx = jax.nn.silu(c_ctx)
    p_lb = jax.nn.softmax(hg_lb_logits.astype(F32), axis=0)
    lb_all = jnp.cumsum(p_lb, axis=0) - p_lb[0]
    cond_all = jnp.concatenate([cond, cond_ctx[None]], axis=0)
    for i in range(DEPTH):
        last = i == DEPTH - 1
        kind, j = i % N_MIXERS, i // N_MIXERS
        mod = mm(jnp.pad(cond_all, ((0, -(B + 1) % 128), (0, 0))), w_mod[i])[:B + 1] + b_mod[i]
        sh1, sc1, g1, sh2, sc2, g2 = jnp.split(mod[:B, None, :], 6, axis=-1)
        sh1c, sc1c, g1c, sh2c, sc2c, g2c = jnp.split(mod[B], 6, axis=-1)

        def bc_ctx(v):
            return jnp.broadcast_to(v[None, None, :], (B, 1, D))

        h = norm_modulate(x, norm_mix_g[i], sc1, sh1)
        hc = norm_modulate(xc, norm_mix_g[i], bc_ctx(sc1c), bc_ctx(sh1c))
        if kind == 0:
            y, yc = diff_attention(h, hc, da_w_qkv[j], da_w_o[j], da_lam[j], da_subln_g[j], i)
        elif kind == 1:
            y, yc = gated_deltanet(h, hc, gdn_w_in[j], gdn_conv[j], gdn_w_ab[j], gdn_dt_bias[j],
                                   gdn_a_log[j], gdn_norm_g[j], gdn_w_o[j])
        elif kind == 2:
            y, yc = neighbourhood_attention(h, hc, na_w_qkv[j], na_rpb[j], na_w_o[j])
        else:
            y, yc = hgrn2(h, hc, hg_w_q[j], hg_w_i[j], hg_w_f[j], hg_b_f[j], hg_w_g[j],
                          hg_norm_g[j], hg_w_o[j], lb_all[i])
        x = x + g1 * y
        h2, aff = norm_modulate(x, norm_ffn_g[i], sc2, sh2, w_router)
        if last:
            f = grouped_moe(h2.reshape(B * L, D), aff.reshape(B * L, ROUTER_PAD)[:, :N_EXPERTS],
                            b_router, e_w_gate[i], e_w_up[i], e_w_down[i])
            x = x + g2 * f.reshape(B, L, D)
        else:
            xc = xc + g1c * yc
            h2c, affc = norm_modulate(xc, norm_ffn_g[i], bc_ctx(sc2c), bc_ctx(sh2c), w_router)
            toks = jnp.concatenate([h2c, h2], axis=1).reshape(-1, D)
            affs = jnp.concatenate([affc, aff], axis=1).reshape(-1, ROUTER_PAD)[:, :N_EXPERTS]
            f = grouped_moe(toks, affs, b_router, e_w_gate[i], e_w_up[i], e_w_down[i]).reshape(B, tc + L, D)
            xc = xc + g2c * f[:, :tc]
            x = x + g2 * f[:, tc:]
    return final_rmsnorm(x, final_norm_g)
```

```python
import functools
import math

import jax
import jax.numpy as jnp
from jax import lax
from jax.experimental import pallas as pl
from jax.experimental.pallas import tpu as pltpu

F32 = jnp.float32
BF16 = jnp.bfloat16

D_MODEL = 2048
DEPTH = 4
GRID_W = 64
N_MIXERS = 4
EPS = 1e-6
DA_HEADS = 16
DA_HALF = D_MODEL // DA_HEADS // 2
DA_VDIM = 2 * DA_HALF
ROPE_BASE = 10000.0
GDN_HEADS = 16
GDN_DK = D_MODEL // GDN_HEADS
GDN_DV = D_MODEL // GDN_HEADS
GDN_CHUNK = 64
NA_HEADS = 16
NA_DIM = D_MODEL // NA_HEADS
WIN_R = 8
WIN_C = 16
HG_HEADS = 16
HG_DK = D_MODEL // HG_HEADS
HG_DV = D_MODEL // HG_HEADS
HG_CHUNK = 32
N_EXPERTS = 16
N_GROUPS = 4
EXPERTS_PER_GROUP = N_EXPERTS // N_GROUPS
GROUP_SCORE_TOPK = 2
TOP_K = 2
D_EXPERT = D_MODEL // 2

V7X_VMEM_LIMIT_BYTES = 56 * 1024 * 1024
LANES = 128
MOE_TILE = 512
ROUTER_PAD = LANES


def _largest_divisor(n, candidates):
    for c in candidates:
        if n % c == 0:
            return c
    raise ValueError(f"no tile in {candidates} divides {n}")


def _cparams(*sem):
    return pltpu.CompilerParams(dimension_semantics=sem, vmem_limit_bytes=V7X_VMEM_LIMIT_BYTES)


def _mm_kernel(a_ref, w_ref, o_ref):
    o_ref[...] = jnp.dot(a_ref[...], w_ref[...], preferred_element_type=F32).astype(o_ref.dtype)


def mm(a, w, out_dtype=F32):
    m, k = a.shape
    n = w.shape[1]
    n_pad = -n % LANES
    if n_pad:
        w = jnp.pad(w, ((0, 0), (0, n_pad)))
    a = a.astype(BF16)
    w = w.astype(BF16)
    np_ = n + n_pad
    tm = _largest_divisor(m, (1024, 512, 256, 128))
    tn = _largest_divisor(np_, (512, 256, 128))
    out = pl.pallas_call(
        _mm_kernel,
        out_shape=jax.ShapeDtypeStruct((m, np_), out_dtype),
        grid=(m // tm, np_ // tn),
        in_specs=[pl.BlockSpec((tm, k), lambda i, j: (i, 0)),
                  pl.BlockSpec((k, tn), lambda i, j: (0, j))],
        out_specs=pl.BlockSpec((tm, tn), lambda i, j: (i, j)),
        compiler_params=_cparams("parallel", "arbitrary"),
        name="proj_mm",
    )(a, w)
    return out[:, :n] if n_pad else out


def _gated_norm_proj_kernel(of_ref, ob_ref, zg_ref, g_ref, w_ref, o_ref, a_scr, *, head_dim, silu_gate):
    @pl.when(pl.program_id(1) == 0)
    def _():
        o = of_ref[...] + ob_ref[...]
        zg = zg_ref[...].astype(F32)
        sig = jax.nn.sigmoid(zg)
        gate = zg * sig if silu_gate else sig
        for h in range(o.shape[1] // head_dim):
            sl = slice(h * head_dim, (h + 1) * head_dim)
            oh = o[:, sl]
            oh = oh * lax.rsqrt(jnp.mean(oh * oh, axis=-1, keepdims=True) + EPS) * g_ref[...]
            a_scr[:, sl] = (oh * gate[:, sl]).astype(BF16)

    o_ref[...] = jnp.dot(a_scr[...], w_ref[...], preferred_element_type=F32).astype(o_ref.dtype)


def gated_norm_proj(o_fwd, o_bwd, zz, gate_block, norm_g, w_o, *, silu_gate):
    (fa, fi), (ba, bi) = o_fwd, o_bwd
    m, hd = fa.shape[-2:]
    k, n = w_o.shape
    assert k == hd
    head_dim = norm_g.shape[0]
    tm = _largest_divisor(m, (512, 256, 128))
    tn = _largest_divisor(n, (512, 256, 128))

    def rows(idx):
        if idx is None:
            return pl.BlockSpec((tm, hd), lambda i, j: (i, 0))
        return pl.BlockSpec((None, tm, hd), lambda i, j: (idx, i, 0))

    kern = functools.partial(_gated_norm_proj_kernel, head_dim=head_dim, silu_gate=silu_gate)
    return pl.pallas_call(
        kern,
        out_shape=jax.ShapeDtypeStruct((m, n), BF16),
        grid=(m // tm, n // tn),
        in_specs=[rows(fi), rows(bi),
                  pl.BlockSpec((tm, hd), lambda i, j: (i, gate_block)),
                  pl.BlockSpec((1, head_dim), lambda i, j: (0, 0)),
                  pl.BlockSpec((k, tn), lambda i, j: (0, j))],
        out_specs=pl.BlockSpec((tm, tn), lambda i, j: (i, j)),
        scratch_shapes=[pltpu.VMEM((tm, hd), BF16)],
        compiler_params=_cparams("parallel", "arbitrary"),
        name="gated_norm_proj",
    )(fa, ba, zz, norm_g.reshape(1, head_dim).astype(F32), w_o.astype(BF16))


def _resid_norm_kernel(*refs, n_branch, weighted, modulated, routed):
    refs = list(refs)
    x = refs.pop(0)[...].astype(F32)
    ys = [refs.pop(0)[...].astype(F32) for _ in range(n_branch)]
    if weighted:
        w = refs.pop(0)[...]
        ys = [y * w[:, k:k + 1] for k, y in enumerate(ys)]
    if n_branch:
        x = x + refs.pop(0)[...] * functools.reduce(lambda p, q: p + q, ys)
    g = refs.pop(0)[...]
    out = x * lax.rsqrt(jnp.mean(x * x, axis=-1, keepdims=True) + EPS) * g
    if modulated:
        sc = refs.pop(0)[...]
        sh = refs.pop(0)[...]
        out = out * (1.0 + sc) + sh
    wr = refs.pop(0) if routed else None
    if n_branch:
        refs.pop(0)[...] = x
    o_ref = refs.pop(0)
    o_ref[...] = out.astype(o_ref.dtype)
    if routed:
        logits = jnp.dot(out, wr[...], preferred_element_type=F32, precision=lax.Precision.HIGHEST)
        refs.pop(0)[...] = jax.nn.sigmoid(logits)


def resid_norm(x, n_ctx, gain, *, branches=(), weights=None, gate=None, sc=None, sh=None, w_router=None,
               latent_only=False):
    b, t, d = x.shape
    tm = n_ctx
    assert t % tm == 0
    off = 1 if latent_only else 0
    rows = pl.BlockSpec((None, tm, d), lambda i, j: (i, j + off, 0))
    out_rows = pl.BlockSpec((None, tm, d), lambda i, j: (i, j, 0))
    seg = pl.BlockSpec((None, None, 1, d), lambda i, j: (i, jnp.minimum(j + off, 1), 0, 0))
    args, specs = [x], [rows]
    for y in branches:
        args.append(y)
        specs.append(rows)
    if weights is not None:
        args.append(weights)
        specs.append(pl.BlockSpec((None, tm, weights.shape[-1]), lambda i, j: (i, j + off, 0)))
    if branches:
        args.append(gate)
        specs.append(seg)
    args.append(gain.reshape(1, d).astype(F32))
    specs.append(pl.BlockSpec((1, d), lambda i, j: (0, 0)))
    modulated = sc is not None
    if modulated:
        args += [sc, sh]
        specs += [seg, seg]
    routed = w_router is not None
    if routed:
        args.append(jnp.pad(w_router.astype(F32), ((0, 0), (0, ROUTER_PAD - w_router.shape[1]))))
        specs.append(pl.BlockSpec((d, ROUTER_PAD), lambda i, j: (0, 0)))
    t_out = t - off * tm
    out_shapes, out_specs = [], []
    if branches:
        out_shapes.append(jax.ShapeDtypeStruct((b, t_out, d), F32))
        out_specs.append(out_rows)
    out_shapes.append(jax.ShapeDtypeStruct((b, t_out, d), BF16 if modulated else F32))
    out_specs.append(out_rows)
    if routed:
        out_shapes.append(jax.ShapeDtypeStruct((b, t_out, ROUTER_PAD), F32))
        out_specs.append(pl.BlockSpec((None, tm, ROUTER_PAD), lambda i, j: (i, j, 0)))
    kern = functools.partial(_resid_norm_kernel, n_branch=len(branches), weighted=weights is not None,
                             modulated=modulated, routed=routed)
    return pl.pallas_call(
        kern,
        out_shape=tuple(out_shapes),
        grid=(b, t_out // tm),
        in_specs=specs,
        out_specs=tuple(out_specs),
        compiler_params=_cparams("parallel", "parallel"),
        name="resid_norm",
    )(*args)


def _moe_kernel(te_ref, tv_ref, x_ref, wg_ref, wu_ref, wd_ref, o_ref):
    del te_ref
    i = pl.program_id(0)

    @pl.when(tv_ref[i] > 0)
    def _():
        x = x_ref[...]
        g = jnp.dot(x, wg_ref[0], preferred_element_type=F32)
        u = jnp.dot(x, wu_ref[0], preferred_element_type=F32)
        act = (g * jax.nn.sigmoid(g) * u).astype(BF16)
        o_ref[...] = jnp.dot(act, wd_ref[0], preferred_element_type=F32).astype(o_ref.dtype)

    @pl.when(tv_ref[i] == 0)
    def _():
        o_ref[...] = jnp.zeros_like(o_ref)


def _route(aff, b_router):
    assert GROUP_SCORE_TOPK == 2 and TOP_K == 2
    epg = EXPERTS_PER_GROUP
    sel = aff + b_router.astype(F32)
    s = [sel[:, e] for e in range(N_EXPERTS)]
    a = [aff[:, e] for e in range(N_EXPERTS)]

    def first_max(vals):
        idx, best = jnp.zeros_like(vals[0], dtype=jnp.int32), vals[0]
        for e in range(1, len(vals)):
            upd = vals[e] > best
            idx, best = jnp.where(upd, e, idx), jnp.where(upd, vals[e], best)
        return idx, best

    def pick(vals, idx):
        out = vals[0]
        for e in range(1, len(vals)):
            out = jnp.where(idx == e, vals[e], out)
        return out

    def top2_sum(v):
        pairs = [v[i] + v[j] for i in range(len(v)) for j in range(i + 1, len(v))]
        return functools.reduce(jnp.maximum, pairs)

    g_best, _ = first_max([top2_sum(s[g * epg:(g + 1) * epg]) for g in range(N_GROUPS)])
    in_s = [pick([s[g * epg + e] for g in range(N_GROUPS)], g_best) for e in range(epg)]
    in_a = [pick([a[g * epg + e] for g in range(N_GROUPS)], g_best) for e in range(epg)]
    i1, _ = first_max(in_s)
    i2, _ = first_max([jnp.where(i1 == e, NEG_INF, in_s[e]) for e in range(epg)])
    w1, w2 = pick(in_a, i1), pick(in_a, i2)
    tot = w1 + w2
    expert_idx = jnp.stack([g_best * epg + i1, g_best * epg + i2], axis=-1)
    return expert_idx.astype(jnp.int32), jnp.stack([w1 / tot, w2 / tot], axis=-1)


def grouped_moe(h2, aff, b_router, w_gate, w_up, w_down):
    n, d = h2.shape
    expert_idx, w = _route(aff, b_router)
    n_slots = TOP_K * n
    e_flat = expert_idx.reshape(-1)
    onehot = (e_flat[:, None] == jnp.arange(N_EXPERTS, dtype=jnp.int32)[None, :]).astype(jnp.int32)
    csum = jnp.cumsum(onehot, axis=0)
    rank = jnp.take_along_axis(csum, e_flat[:, None], axis=1)[:, 0] - 1
    counts = csum[-1]
    padded = ((counts + MOE_TILE - 1) // MOE_TILE) * MOE_TILE
    pad_end = jnp.cumsum(padded)
    pad_off = pad_end - padded
    dest_flat = pad_off[e_flat] + rank
    dest = dest_flat.reshape(n, TOP_K)

    n_tiles = -(-n_slots // MOE_TILE) + N_EXPERTS
    n_pad_slots = n_tiles * MOE_TILE
    tile_start = jnp.arange(n_tiles, dtype=jnp.int32) * MOE_TILE
    tile_expert = jnp.minimum(jnp.sum((tile_start[:, None] >= pad_end[None, :]).astype(jnp.int32), axis=1),
                              N_EXPERTS - 1)
    tile_valid = (tile_start < pad_end[-1]).astype(jnp.int32)

    slot_token = jnp.zeros((n_pad_slots,), jnp.int32).at[dest_flat].set(
        jnp.arange(n_slots, dtype=jnp.int32) // TOP_K, unique_indices=True)

    xs = jnp.take(h2, slot_token, axis=0)
    f = D_EXPERT
    ys = pl.pallas_call(
        _moe_kernel,
        out_shape=jax.ShapeDtypeStruct((n_pad_slots, d), BF16),
        grid_spec=pltpu.PrefetchScalarGridSpec(
            num_scalar_prefetch=2,
            grid=(n_tiles,),
            in_specs=[pl.BlockSpec((MOE_TILE, d), lambda i, te, tv: (i, 0)),
                      pl.BlockSpec((1, d, f), lambda i, te, tv: (te[i], 0, 0)),
                      pl.BlockSpec((1, d, f), lambda i, te, tv: (te[i], 0, 0)),
                      pl.BlockSpec((1, f, d), lambda i, te, tv: (te[i], 0, 0))],
            out_specs=pl.BlockSpec((MOE_TILE, d), lambda i, te, tv: (i, 0)),
        ),
        compiler_params=_cparams("arbitrary"),
        name="moe_experts",
    )(tile_expert, tile_valid, xs, w_gate.astype(BF16), w_up.astype(BF16), w_down.astype(BF16))
    return jnp.take(ys, dest[:, 0], axis=0), jnp.take(ys, dest[:, 1], axis=0), w


def _diff_attn_kernel(lam_ref, q_ref, k_ref, v_ref, cos_ref, sin_ref, g_ref, o_ref, k_scr, v_scr,
                      *, n_ctx, lam_init):
    i = pl.program_id(2)
    tq = q_ref.shape[0]
    t = k_ref.shape[0]
    hw = 2 * DA_HALF
    lam = lam_ref[0]
    lane = lax.broadcasted_iota(jnp.int32, (1, hw), 1)
    first_half = (lane % (DA_HALF // 2)) < DA_HALF // 4
    nt = (((1,), (1,)), ((), ()))

    def rope(x, lo):
        xf = x.astype(F32)
        partner = jnp.where(first_half, pltpu.roll(xf, hw - DA_HALF // 4, axis=1), pltpu.roll(xf, DA_HALF // 4, axis=1))
        return xf * cos_ref[pl.ds(lo, tq), :] + partner * sin_ref[pl.ds(lo, tq), :]

    @pl.when(i == 0)
    def _():
        def body(c, carry):
            lo = pl.multiple_of(c * tq, tq)
            k_scr[pl.ds(lo, tq), :] = rope(k_ref[pl.ds(lo, tq), :], lo).astype(BF16)
            return carry

        lax.fori_loop(0, t // tq, body, 0)
        v_scr[:, :hw] = v_ref[...]
        v_scr[:, hw:] = jnp.ones((t, hw), BF16)

    q = rope(q_ref[...], pl.multiple_of(i * tq, tq))
    q1 = jnp.where(lane < DA_HALF, q, 0.0).astype(BF16)
    q2 = jnp.where(lane >= DA_HALF, q, 0.0).astype(BF16)

    def attend(k, v_ext):
        def one_map(qm):
            s = lax.dot_general(qm, k, nt, preferred_element_type=F32)
            p = jnp.exp((s - jnp.max(s, axis=-1, keepdims=True)).astype(BF16))
            oe = jnp.dot(p, v_ext, preferred_element_type=F32)
            return oe[:, :hw] * (1.0 / oe[:, hw:hw + 1])

        o = one_map(q1) - lam * one_map(q2)
        o = o * lax.rsqrt(jnp.mean(o * o, axis=-1, keepdims=True) + 1e-5)
        o_ref[...] = ((o * g_ref[...]) * (1.0 - lam_init)).astype(o_ref.dtype)

    @pl.when(i == 0)
    def _():
        attend(k_scr[:n_ctx, :], v_scr[:n_ctx, :])

    @pl.when(i > 0)
    def _():
        attend(k_scr[...], v_scr[...])


def diff_attn_core(qkv, cos, sin, lam, subln_g, n_ctx, lam_init):
    b, t, _ = qkv.shape
    tq = n_ctx
    nh = DA_HEADS
    hw = 2 * DA_HALF
    kern = functools.partial(_diff_attn_kernel, n_ctx=n_ctx, lam_init=lam_init)
    table = pl.BlockSpec((t, hw), lambda bi, h, i: (0, 0))
    return pl.pallas_call(
        kern,
        out_shape=jax.ShapeDtypeStruct((b, t, nh * hw), BF16),
        grid=(b, nh, t // tq),
        in_specs=[pl.BlockSpec(memory_space=pltpu.SMEM),
                  pl.BlockSpec((None, tq, hw), lambda bi, h, i: (bi, i, h)),
                  pl.BlockSpec((None, t, hw), lambda bi, h, i: (bi, 0, nh + h)),
                  pl.BlockSpec((None, t, hw), lambda bi, h, i: (bi, 0, 2 * nh + h)),
                  table, table,
                  pl.BlockSpec((1, hw), lambda bi, h, i: (0, 0))],
        out_specs=pl.BlockSpec((None, tq, hw), lambda bi, h, i: (bi, i, h)),
        scratch_shapes=[pltpu.VMEM((t, hw), BF16), pltpu.VMEM((t, 2 * hw), BF16)],
        compiler_params=_cparams("parallel", "parallel", "arbitrary"),
        name="diff_attn",
    )(lam.reshape(1).astype(F32), qkv, qkv, qkv, cos, sin, subln_g.reshape(1, hw).astype(F32))


NA_QROWS = 8
NA_KROWS = 2 * NA_QROWS
NA_DR_PAD = 16
NEG_INF = float("-inf")
NA_BIAS_ROWS = 2 * WIN_R - 1
NA_BIAS_COLS = 2 * WIN_C - 1


def _na_kernel(rpb_ref, q_ref, k_ref, v_ref, o_ref, bias_ref, *, n_ctx, rows):
    h = pl.program_id(0)
    b = pl.program_id(1)
    j = pl.program_id(2)
    w = GRID_W
    nq = NA_QROWS * w
    nk = NA_KROWS * w
    lane = lax.broadcasted_iota(jnp.int32, (w, 2 * w), 1)
    nt = (((1,), (1,)), ((), ()))

    @pl.when((b == 0) & (j == 0))
    def _():
        c = lax.broadcasted_iota(jnp.int32, (w, 2 * w), 0)
        kc = lane % w
        c0 = jnp.clip(c - WIN_C // 2, 0, w - WIN_C)
        in_win = (kc >= c0) & (kc < c0 + WIN_C)
        single = []
        for dr in range(-(WIN_R - 1), WIN_R):
            t = jnp.full((w, 2 * w), NEG_INF, F32)
            for dc in range(-(WIN_C - 1), WIN_C):
                t = jnp.where(kc - c == dc, rpb_ref[(h * NA_BIAS_ROWS + dr + WIN_R - 1) * NA_BIAS_COLS + dc + WIN_C - 1], t)
            single.append(jnp.where(in_win, t, NEG_INF))
        neg = jnp.full((w, 2 * w), NEG_INF, F32)

        def at(dr):
            return single[dr + WIN_R - 1] if abs(dr) < WIN_R else neg

        for d in range(2 * NA_DR_PAD + 1):
            bias_ref[d] = jnp.where(lane < w, at(d - NA_DR_PAD), at(d - NA_DR_PAD + 1))

    kctx = k_ref[0, :n_ctx, :]
    vctx = v_ref[0, :n_ctx, :]

    def finish(parts, q_lo, n):
        m = parts[0][0].max(axis=-1, keepdims=True)
        for s, _ in parts[1:]:
            m = jnp.maximum(m, s.max(axis=-1, keepdims=True))
        l = 0.0
        o = 0.0
        for s, vv in parts:
            p = jnp.exp(s - m)
            l = l + p.sum(axis=-1, keepdims=True)
            o = o + jnp.dot(p.astype(BF16), vv, preferred_element_type=F32)
        o_ref[0, pl.ds(q_lo, n), :] = (o * (1.0 / l)).astype(o_ref.dtype)

    @pl.when(j < rows // NA_QROWS)
    def _():
        r_lo = j * NA_QROWS
        ks = jnp.clip(r_lo - WIN_R // 2, 0, rows - NA_KROWS)
        q_lo = pl.multiple_of(n_ctx + r_lo * w, w)
        k_lo = pl.multiple_of(n_ctx + ks * w, w)
        q = q_ref[0, pl.ds(q_lo, nq), :]
        kwin = k_ref[0, pl.ds(k_lo, nk), :]
        vwin = v_ref[0, pl.ds(k_lo, nk), :]
        s_win = lax.dot_general(q, kwin, nt, preferred_element_type=F32)
        half = (lane >= w).astype(jnp.int32)
        row_blocks = []
        for i in range(NA_QROWS):
            r = r_lo + i
            r0 = jnp.clip(r - WIN_R // 2, 0, rows - WIN_R)
            tiles = []
            for jj in range(NA_KROWS // 2):
                kr = ks + 2 * jj
                off = lax.bitcast_convert_type(half + (kr - r0), jnp.uint32)
                t = s_win[i * w:(i + 1) * w, 2 * jj * w:(2 * jj + 2) * w] + bias_ref[kr - r + NA_DR_PAD]
                tiles.append(jnp.where(off < WIN_R, t, NEG_INF))
            row_blocks.append(jnp.concatenate(tiles, axis=1))
        s_win = jnp.concatenate(row_blocks, axis=0)
        s_ctx = lax.dot_general(q, kctx, nt, preferred_element_type=F32)
        finish([(s_win, vwin), (s_ctx, vctx)], q_lo, nq)

    @pl.when(j == rows // NA_QROWS)
    def _():
        q = q_ref[0, :n_ctx, :]
        finish([(lax.dot_general(q, kctx, nt, preferred_element_type=F32), vctx)], 0, n_ctx)


def na_attn_core(qkv, rpb, n_ctx):
    b, t, _ = qkv.shape
    rows = (t - n_ctx) // GRID_W
    assert rows % NA_QROWS == 0 and rows >= NA_KROWS
    nh, hd = NA_HEADS, NA_DIM
    kern = functools.partial(_na_kernel, n_ctx=n_ctx, rows=rows)

    def col(off):
        return pl.BlockSpec((1, t, hd), lambda h, bi, j: (bi, 0, off + h))

    return pl.pallas_call(
        kern,
        out_shape=jax.ShapeDtypeStruct((b, t, nh * hd), BF16),
        grid=(nh, b, rows // NA_QROWS + 1),
        in_specs=[pl.BlockSpec(memory_space=pltpu.SMEM), col(0), col(nh), col(2 * nh)],
        out_specs=col(0),
        scratch_shapes=[pltpu.VMEM((2 * NA_DR_PAD + 1, GRID_W, 2 * GRID_W), F32)],
        compiler_params=_cparams("arbitrary", "arbitrary", "arbitrary"),
        name="na_attn",
    )(rpb.astype(F32).reshape(-1), qkv, qkv, qkv)


def rmsnorm(x, g, eps=EPS):
    xf = x.astype(F32)
    y = xf * lax.rsqrt(jnp.mean(xf * xf, axis=-1, keepdims=True) + eps)
    return y.astype(x.dtype) * g


def l2norm(x):
    return x * lax.rsqrt(jnp.sum(x * x, axis=-1, keepdims=True) + EPS)


def flip(t):
    return None if t is None else jnp.flip(t, axis=1)


def proj(t, w, out_dtype=F32):
    b, tt, k = t.shape
    return mm(t.reshape(b * tt, k), w, out_dtype).reshape(b, tt, w.shape[1])


def axial_rope_tables(length, dim):
    n_freq = dim // 4
    t = jnp.arange(length)
    pos = jnp.stack([t // GRID_W, t % GRID_W], axis=-1).astype(F32)
    inv = ROPE_BASE ** (-jnp.arange(n_freq, dtype=F32) / n_freq)
    ang = pos[:, :, None] * inv
    return jnp.cos(ang), jnp.sin(ang)


def apply_rope(x, cos, sin):
    n_freq = cos.shape[-1]
    xr = x.reshape(*x.shape[:-1], 2, 2, n_freq)
    x1, x2 = xr[..., 0, :], xr[..., 1, :]
    c = cos[:, None].astype(x.dtype)
    s = sin[:, None].astype(x.dtype)
    out = jnp.stack([x1 * c - x2 * s, x1 * s + x2 * c], axis=-2)
    return out.reshape(x.shape)


def diff_attention(h, n_ctx, w_qkv, w_o, lam_vec, subln_g, layer_idx):
    t = h.shape[1]
    dq = 2 * DA_HEADS * DA_HALF
    lam_init = 0.8 - 0.6 * math.exp(-0.3 * layer_idx)
    lv = lam_vec.astype(F32)
    lam = jnp.exp(jnp.sum(lv[0] * lv[1])) - jnp.exp(jnp.sum(lv[2] * lv[3])) + lam_init
    w = jnp.concatenate([w_qkv[:, :dq] * DA_HALF ** -0.5, w_qkv[:, dq:]], axis=1)
    cos, sin = axial_rope_tables(t - n_ctx, DA_HALF)
    n_freq = cos.shape[-1]
    sign = jnp.array([-1.0, 1.0], F32)[None, None, :, None]
    cos_t = jnp.broadcast_to(cos[:, :, None, :], (t - n_ctx, 2, 2, n_freq)).reshape(t - n_ctx, DA_HALF)
    sin_t = (sin[:, :, None, :] * sign).reshape(t - n_ctx, DA_HALF)
    cos_t = jnp.concatenate([jnp.ones((n_ctx, 2 * DA_HALF), F32), jnp.tile(cos_t, (1, 2))], axis=0)
    sin_t = jnp.concatenate([jnp.zeros((n_ctx, 2 * DA_HALF), F32), jnp.tile(sin_t, (1, 2))], axis=0)
    o = diff_attn_core(proj(h, w, BF16), cos_t, sin_t, lam, subln_g, n_ctx, lam_init)
    return proj(o, w_o, BF16)


def centred_dwconv(x, w):
    k = w.shape[0]
    pad = k // 2
    t = x.shape[1]
    xp = jnp.pad(x, ((0, 0), (pad, pad), (0, 0)))
    out = xp[:, 0:t] * w[0]
    for j in range(1, k):
        out = out + xp[:, j:j + t] * w[j]
    return out


GDN_HEAD_GROUP = 16
GDN_SUB = 16


def _bdot(a, b):
    return jnp.dot(a.astype(BF16), b.astype(BF16), preferred_element_type=F32)


def _scan_chunk(d, s, n, n_ctx_chunks):
    back = jnp.where(s < n_ctx_chunks, n_ctx_chunks - 1 - s, n + n_ctx_chunks - 1 - s)
    return jnp.where(d == 0, s, back)


def _gdn_kernel(qs_ref, k_ref, v_ref, gc_ref, bc_ref, gr_ref, o_ref, s_ref):
    d = pl.program_id(0)
    s = pl.program_id(3)
    c = GDN_CHUNK

    @pl.when(s == 0)
    def _():
        s_ref[...] = jnp.zeros_like(s_ref)

    row = lax.broadcasted_iota(jnp.int32, (c, c), 0)
    col = lax.broadcasted_iota(jnp.int32, (c, c), 1)
    ahead = (row - col) * (1 - 2 * d)
    strict = ahead > 0
    incl = ahead >= 0
    same_blk = (row // GDN_SUB) == (col // GDN_SUB)
    eye = (row == col).astype(F32)
    nt = (((1,), (1,)), ((), ()))
    tn = (((0,), (0,)), ((), ()))
    heads = range(GDN_HEAD_GROUP)
    sls = [slice(g * GDN_DK, (g + 1) * GDN_DK) for g in heads]

    def each(fn, *lists):
        return [fn(*vals) for vals in zip(*lists)]

    gcol = gc_ref[...]
    g_last = jnp.min(gcol, axis=0, keepdims=True)
    e_g, e_rest, e_all = jnp.exp(gcol), jnp.exp(g_last - gcol), jnp.exp(g_last)
    beta = bc_ref[...]
    ks = [k_ref[:, sl] for sl in sls]
    kf = [k.astype(F32) for k in ks]
    kb = [k * beta[:, g:g + 1] for g, k in zip(heads, kf)]
    kbg = [(k * e_g[:, g:g + 1]).astype(BF16) for g, k in zip(heads, kb)]
    kd = [(k * e_rest[:, g:g + 1]).astype(BF16) for g, k in zip(heads, kf)]
    vb = [(v_ref[:, sl].astype(F32) * beta[:, g:g + 1]).astype(BF16) for g, sl in zip(heads, sls)]
    qg = [(qs_ref[:, sl].astype(F32) * e_g[:, g:g + 1]).astype(BF16) for g, sl in zip(heads, sls)]
    decay = [jnp.exp(jnp.where(incl, gcol[:, g:g + 1] - gr_ref[g:g + 1, :], NEG_INF)) for g in heads]
    a = each(lambda m, k, dc: jnp.where(
        strict, lax.dot_general(m.astype(BF16), k, nt, preferred_element_type=F32) * dc, 0.0), kb, ks, decay)
    qk = each(lambda sl, k, dc: jnp.where(
        incl, lax.dot_general(qs_ref[:, sl], k, nt, preferred_element_type=F32) * dc, 0.0), sls, ks, decay)
    dblk = each(lambda m: jnp.where(same_blk, m, 0.0), a)
    d2 = each(lambda m: _bdot(m, m), dblk)
    x = each(lambda m, m2: _bdot(eye - m, eye + m2), dblk, d2)
    d4 = each(lambda m: _bdot(m, m), d2)
    x = each(lambda m, m4: _bdot(m, eye + m4), x, d4)
    d8 = each(lambda m: _bdot(m, m), d4)
    x = each(lambda m, m8: _bdot(m, eye + m8), x, d8)
    nmat = each(lambda m, am, dm: _bdot(m, am - dm), x, a, dblk)
    n2 = each(lambda m: _bdot(m, m), nmat)
    y = each(lambda m, m2: _bdot(eye - m, eye + m2), nmat, n2)
    t = each(_bdot, y, x)
    uw = each(lambda m, p, q: _bdot(m, jnp.concatenate([p, q], axis=1)), t, vb, kbg)
    state = [s_ref[g] for g in heads]
    v_new = each(lambda m, st: m[:, :GDN_DV] - _bdot(m[:, GDN_DV:], st), uw, state)
    o_state = each(_bdot, qg, state)
    o_new = each(_bdot, qk, v_new)
    s_new = each(lambda m, vn: lax.dot_general(m, vn.astype(BF16), tn, preferred_element_type=F32), kd, v_new)
    for g in heads:
        s_ref[g] = state[g] * e_all[:, g:g + 1] + s_new[g]
    o_ref[...] = jnp.concatenate(each(lambda p, q: p + q, o_state, o_new), axis=1)


def gated_delta_core(qs, k, v, g, beta, n_ctx_chunks):
    b, t, hd = k.shape
    c, hh, gg = GDN_CHUNK, GDN_HEADS, GDN_HEAD_GROUP
    n = t // c
    gw = gg * GDN_DK

    def cols(a):
        return a.reshape(2, b, t, hh // gg, gg).transpose(0, 1, 3, 2, 4)

    g_row = g.reshape(2, b, n, c, hh // gg, gg).transpose(0, 1, 2, 4, 5, 3)

    def chunk_of(d, s):
        return _scan_chunk(d, s, n, n_ctx_chunks)

    shared = pl.BlockSpec((None, c, gw), lambda d, bi, hg, s: (bi, chunk_of(d, s), hg))
    col_spec = pl.BlockSpec((None, None, None, c, gg), lambda d, bi, hg, s: (d, bi, hg, chunk_of(d, s), 0))
    return pl.pallas_call(
        _gdn_kernel,
        out_shape=jax.ShapeDtypeStruct((2, b, t, hd), F32),
        grid=(2, b, hh // gg, n),
        in_specs=[shared, shared, shared, col_spec, col_spec,
                  pl.BlockSpec((None, None, None, None, gg, c), lambda d, bi, hg, s: (d, bi, chunk_of(d, s), hg, 0, 0))],
        out_specs=pl.BlockSpec((None, None, c, gw), lambda d, bi, hg, s: (d, bi, chunk_of(d, s), hg)),
        scratch_shapes=[pltpu.VMEM((gg, GDN_DK, GDN_DV), F32)],
        compiler_params=_cparams("parallel", "parallel", "parallel", "arbitrary"),
        name="gated_delta",
    )(qs, k, v, cols(g), cols(beta), g_row)


def gated_deltanet(h, n_ctx, w_in, conv_w, w_ab, dt_bias, a_log, norm_g, w_o):
    B, T, _ = h.shape
    tc = n_ctx
    H, C = GDN_HEADS, GDN_CHUNK
    wq = H * GDN_DK
    hi = 2 * wq + H * GDN_DV

    zz = proj(h, w_in)
    conv = jnp.concatenate([centred_dwconv(zz[:, :tc, :hi], conv_w), centred_dwconv(zz[:, tc:, :hi], conv_w)], axis=1)
    z = jax.nn.silu(conv).astype(F32)
    q = l2norm(z[..., :wq].reshape(B, T, H, GDN_DK))
    k = l2norm(z[..., wq:2 * wq].reshape(B, T, H, GDN_DK))
    v = z[..., 2 * wq:].reshape(B, T, H, GDN_DV)
    ab = proj(h, w_ab).astype(F32).reshape(B, T, 2, 2, H)
    la = -jnp.exp(a_log.astype(F32)) * jax.nn.softplus(ab[:, :, 0] + dt_bias.astype(F32))
    be = jax.nn.sigmoid(ab[:, :, 1])
    n = T // C
    la_c = la.reshape(B, n, C, 2, H)
    g_f = jnp.cumsum(la_c[:, :, :, 0], axis=2)
    g_b = jnp.flip(jnp.cumsum(jnp.flip(la_c[:, :, :, 1], axis=2), axis=2), axis=2)
    g = jnp.stack([g_f, g_b]).reshape(2, B, T, H)

    def flat(a):
        return a.astype(BF16).reshape(B, T, -1)

    o = gated_delta_core(flat(q * GDN_DK ** -0.5), flat(k), flat(v), g, jnp.moveaxis(be, 2, 0), tc // C)
    o = o.reshape(2, B * T, -1)
    y = gated_norm_proj((o, 0), (o, 1), zz.reshape(B * T, -1), hi // (H * GDN_DV), norm_g, w_o, silu_gate=True)
    return y.reshape(B, T, -1)


def neighbourhood_attention(h, n_ctx, w_qkv, rpb, w_o):
    hd = NA_HEADS * NA_DIM
    w = jnp.concatenate([w_qkv[:, :hd] * NA_DIM ** -0.5, w_qkv[:, hd:]], axis=1)
    return proj(na_attn_core(proj(h, w, BF16), rpb, n_ctx), w_o, BF16)


GLA_CHUNK = 64
GLA_LEVELS = (1, 2, 4, 8, 16, 32)


def _gla_kernel(zq_ref, zv_ref, zf_ref, lb_ref, bf_ref, o_ref, st_ref, *, reverse):
    s = pl.program_id(1)
    c = GLA_CHUNK

    @pl.when(s == 0)
    def _():
        st_ref[...] = jnp.zeros_like(st_ref)

    row = lax.broadcasted_iota(jnp.int32, (c, c), 0)
    col = lax.broadcasted_iota(jnp.int32, (c, c), 1)
    late, early = (col, row) if reverse else (row, col)
    masks = []
    for m in GLA_LEVELS:
        masks.append(((row // (2 * m)) == (col // (2 * m))) & ((late % (2 * m)) >= m) & ((early % (2 * m)) < m))
    diag = row == col
    trow = lax.broadcasted_iota(jnp.int32, (c, HG_DK), 0)
    nt = (((1,), (1,)), ((), ()))
    tn = (((0,), (0,)), ((), ()))
    heads = range(HG_HEADS)
    sls = [slice(g * HG_DK, (g + 1) * HG_DK) for g in heads]

    def ref_rows(b, m):
        p = m if reverse else m - 1
        if 2 * m >= 8:
            blocks = b.reshape(c // (2 * m), 2 * m, HG_DK)
            return jnp.broadcast_to(blocks[:, p:p + 1, :], blocks.shape).reshape(c, HG_DK)
        out = b
        for rho in range(2 * m):
            if rho != p:
                out = jnp.where(trow % (2 * m) == rho, pltpu.roll(b, (rho - p) % c, axis=0), out)
        return out

    zq = zq_ref[...]
    q_all = zq * jax.nn.sigmoid(zq) * HG_DK ** -0.5
    v_all = zv_ref[...].astype(BF16)
    z = zf_ref[...] + bf_ref[...]
    t = jnp.exp(-jnp.abs(z))
    r = 1.0 / (1.0 + t)
    pos = z >= 0
    lb = lb_ref[...]
    log_f = jnp.log(lb + (1.0 - lb) * jnp.where(pos, r, t * r))
    k_all = (1.0 - lb) * jnp.where(pos, t * r, r)
    scan = (col >= row) if reverse else (col <= row)
    b_all = jnp.dot(scan.astype(F32), log_f, preferred_element_type=F32, precision=lax.Precision.HIGHEST)
    qs = [q_all[:, sl] for sl in sls]
    ks = [k_all[:, sl] for sl in sls]
    bs = [b_all[:, sl] for sl in sls]
    vs = [v_all[:, sl] for sl in sls]
    att = [jnp.where(diag, lax.dot_general(q.astype(BF16), k.astype(BF16), nt, preferred_element_type=F32), 0.0)
           for q, k in zip(qs, ks)]
    for m, mask in zip(GLA_LEVELS, masks):
        es = [jnp.exp(-jnp.abs(b - ref_rows(b, m))) for b in bs]
        sc = [lax.dot_general((q * e).astype(BF16), (k * e).astype(BF16), nt, preferred_element_type=F32)
              for q, k, e in zip(qs, ks, es)]
        att = [jnp.where(mask, x, a) for x, a in zip(sc, att)]
    last = c - 1 if not reverse else 0
    b_last = [b[last:last + 1, :] for b in bs]
    state = [st_ref[g] for g in heads]
    o_state = [lax.dot_general((q * jnp.exp(b)).astype(BF16), st.astype(BF16), nt, preferred_element_type=F32)
               for q, b, st in zip(qs, bs, state)]
    o_new = [jnp.dot(a.astype(BF16), v, preferred_element_type=F32) for a, v in zip(att, vs)]
    s_new = [lax.dot_general(v, (k * jnp.exp(bl - b)).astype(BF16), tn, preferred_element_type=F32)
             for v, k, b, bl in zip(vs, ks, bs, b_last)]
    for g in heads:
        st_ref[g] = state[g] * jnp.exp(b_last[g]) + s_new[g]
    o_ref[...] = jnp.concatenate([x + y for x, y in zip(o_state, o_new)], axis=1)


def gla_core(zz, lb, b_f, n_ctx_chunks, reverse):
    bsz, t, _ = zz.shape
    hd = HG_HEADS * HG_DK
    c = GLA_CHUNK
    n = t // c

    def chunk_of(s):
        return _scan_chunk(1, s, n, n_ctx_chunks) if reverse else s

    def cols(j):
        return pl.BlockSpec((None, c, hd), lambda bi, s: (bi, chunk_of(s), j))

    vec = pl.BlockSpec((1, hd), lambda bi, s: (0, 0))
    return pl.pallas_call(
        functools.partial(_gla_kernel, reverse=reverse),
        out_shape=jax.ShapeDtypeStruct((bsz, t, hd), F32),
        grid=(bsz, n),
        in_specs=[cols(0), cols(1), cols(3 if reverse else 2), vec, vec],
        out_specs=cols(0),
        scratch_shapes=[pltpu.VMEM((HG_HEADS, HG_DV, HG_DK), F32)],
        compiler_params=_cparams("parallel", "arbitrary"),
        name="gla_bwd" if reverse else "gla_fwd",
    )(zz, zz, zz, lb.reshape(1, hd).astype(F32), b_f.reshape(1, hd).astype(F32))


def hgrn2(h, n_ctx, w_q, w_i, w_f, b_f, w_g, norm_g, w_o, lb):
    B, T, _ = h.shape
    C = GLA_CHUNK
    hw = HG_HEADS * HG_DK

    zz = proj(h, jnp.concatenate([w_q, w_i, w_f[0], w_f[1], w_g], axis=1))
    o_f = gla_core(zz, lb, b_f[0], n_ctx // C, False).reshape(B * T, hw)
    o_b = gla_core(zz, lb, b_f[1], n_ctx // C, True).reshape(B * T, hw)
    y = gated_norm_proj((o_f, None), (o_b, None), zz.reshape(B * T, -1), 4, norm_g, w_o, silu_gate=False)
    return y.reshape(B, T, -1)


def kernel(x, c, ctx, c_ctx, w_mod, b_mod, norm_mix_g, norm_ffn_g, da_w_qkv, da_w_o, da_lam, da_subln_g, gdn_w_in, gdn_conv, gdn_w_ab, gdn_dt_bias, gdn_a_log, gdn_norm_g, gdn_w_o, na_w_qkv, na_rpb, na_w_o, hg_w_q, hg_w_i, hg_w_f, hg_b_f, hg_w_g, hg_norm_g, hg_w_o, hg_lb_logits, w_router, b_router, e_w_gate, e_w_up, e_w_down, final_norm_g):
    B, L, D = x.shape
    tc = ctx.shape[1]
    T = tc + L
    xa = jnp.concatenate([ctx, x], axis=1)
    p_lb = jax.nn.softmax(hg_lb_logits.astype(F32), axis=0)
    lb_all = jnp.cumsum(p_lb, axis=0) - p_lb[0]
    cond_all = jnp.concatenate([jax.nn.silu(c), jax.nn.silu(c_ctx)[None]], axis=0)
    cond_all = jnp.pad(cond_all, ((0, -(B + 1) % LANES), (0, 0)))

    def mods(i):
        mod = mm(cond_all, w_mod[i])[:B + 1] + b_mod[i]
        both = jnp.stack([jnp.broadcast_to(mod[B], (B, 6 * D)), mod[:B]], axis=1)
        return [both[:, :, None, k * D:(k + 1) * D] for k in range(6)]

    sh1, sc1, g1, sh2, sc2, g2 = mods(0)
    (h,) = resid_norm(xa, tc, norm_mix_g[0], sc=sc1, sh=sh1)
    for i in range(DEPTH):
        kind, j = i % N_MIXERS, i // N_MIXERS
        if kind == 0:
            y = diff_attention(h, tc, da_w_qkv[j], da_w_o[j], da_lam[j], da_subln_g[j], i)
        elif kind == 1:
            y = gated_deltanet(h, tc, gdn_w_in[j], gdn_conv[j], gdn_w_ab[j], gdn_dt_bias[j],
                               gdn_a_log[j], gdn_norm_g[j], gdn_w_o[j])
        elif kind == 2:
            y = neighbourhood_attention(h, tc, na_w_qkv[j], na_rpb[j], na_w_o[j])
        else:
            y = hgrn2(h, tc, hg_w_q[j], hg_w_i[j], hg_w_f[j], hg_b_f[j], hg_w_g[j],
                      hg_norm_g[j], hg_w_o[j], lb_all[i])
        xa, h2, aff = resid_norm(xa, tc, norm_ffn_g[i], branches=(y,), gate=g1, sc=sc2, sh=sh2, w_router=w_router)
        ya, yb, w = grouped_moe(h2.reshape(B * T, D), aff.reshape(B * T, ROUTER_PAD)[:, :N_EXPERTS],
                                b_router, e_w_gate[i], e_w_up[i], e_w_down[i])
        branches = (ya.reshape(B, T, D), yb.reshape(B, T, D))
        w = w.reshape(B, T, TOP_K)
        if i == DEPTH - 1:
            return resid_norm(xa, tc, final_norm_g, branches=branches, weights=w, gate=g2, latent_only=True)[1]
        gate_ffn = g2
        sh1, sc1, g1, sh2, sc2, g2 = mods(i + 1)
        xa, h = resid_norm(xa, tc, norm_mix_g[i + 1], branches=branches, weights=w, gate=gate_ffn, sc=sc1, sh=sh1)
```

```python
import functools
import math

import jax
import jax.numpy as jnp
from jax import lax
from jax.experimental import pallas as pl
from jax.experimental.pallas import tpu as pltpu

F32 = jnp.float32
BF16 = jnp.bfloat16

D_MODEL = 2048
DEPTH = 4
GRID_W = 64
N_MIXERS = 4
EPS = 1e-6
DA_HEADS = 16
DA_HALF = D_MODEL // DA_HEADS // 2
DA_VDIM = 2 * DA_HALF
ROPE_BASE = 10000.0
GDN_HEADS = 16
GDN_DK = D_MODEL // GDN_HEADS
GDN_DV = D_MODEL // GDN_HEADS
GDN_CHUNK = 64
NA_HEADS = 16
NA_DIM = D_MODEL // NA_HEADS
WIN_R = 8
WIN_C = 16
HG_HEADS = 16
HG_DK = D_MODEL // HG_HEADS
HG_DV = D_MODEL // HG_HEADS
HG_CHUNK = 32
N_EXPERTS = 16
N_GROUPS = 4
EXPERTS_PER_GROUP = N_EXPERTS // N_GROUPS
GROUP_SCORE_TOPK = 2
TOP_K = 2
D_EXPERT = D_MODEL // 2

V7X_VMEM_LIMIT_BYTES = 56 * 1024 * 1024
LANES = 128
MOE_TILE = 512
ROUTER_PAD = LANES


def _largest_divisor(n, candidates):
    for c in candidates:
        if n % c == 0:
            return c
    raise ValueError(f"no tile in {candidates} divides {n}")


def _cparams(*sem):
    return pltpu.CompilerParams(dimension_semantics=sem, vmem_limit_bytes=V7X_VMEM_LIMIT_BYTES)


def _mm_kernel(a_ref, w_ref, o_ref):
    o_ref[...] = jnp.dot(a_ref[...], w_ref[...], preferred_element_type=F32).astype(o_ref.dtype)


def mm(a, w, out_dtype=F32):
    m, k = a.shape
    n = w.shape[1]
    n_pad = -n % LANES
    if n_pad:
        w = jnp.pad(w, ((0, 0), (0, n_pad)))
    a = a.astype(BF16)
    w = w.astype(BF16)
    np_ = n + n_pad
    tm = _largest_divisor(m, (1024, 512, 256, 128))
    tn = _largest_divisor(np_, (512, 256, 128))
    out = pl.pallas_call(
        _mm_kernel,
        out_shape=jax.ShapeDtypeStruct((m, np_), out_dtype),
        grid=(m // tm, np_ // tn),
        in_specs=[pl.BlockSpec((tm, k), lambda i, j: (i, 0)),
                  pl.BlockSpec((k, tn), lambda i, j: (0, j))],
        out_specs=pl.BlockSpec((tm, tn), lambda i, j: (i, j)),
        compiler_params=_cparams("parallel", "arbitrary"),
        name="proj_mm",
    )(a, w)
    return out[:, :n] if n_pad else out


def _gated_norm_proj_kernel(of_ref, ob_ref, zg_ref, g_ref, w_ref, o_ref, a_scr, *, head_dim, silu_gate):
    @pl.when(pl.program_id(1) == 0)
    def _():
        o = of_ref[...] + ob_ref[...]
        zg = zg_ref[...].astype(F32)
        sig = jax.nn.sigmoid(zg)
        gate = zg * sig if silu_gate else sig
        for h in range(o.shape[1] // head_dim):
            sl = slice(h * head_dim, (h + 1) * head_dim)
            oh = o[:, sl]
            oh = oh * lax.rsqrt(jnp.mean(oh * oh, axis=-1, keepdims=True) + EPS) * g_ref[...]
            a_scr[:, sl] = (oh * gate[:, sl]).astype(BF16)

    o_ref[...] = jnp.dot(a_scr[...], w_ref[...], preferred_element_type=F32).astype(o_ref.dtype)


def gated_norm_proj(o_fwd, o_bwd, zz, gate_block, norm_g, w_o, *, silu_gate):
    (fa, fi), (ba, bi) = o_fwd, o_bwd
    m, hd = fa.shape[-2:]
    k, n = w_o.shape
    assert k == hd
    head_dim = norm_g.shape[0]
    tm = _largest_divisor(m, (512, 256, 128))
    tn = _largest_divisor(n, (512, 256, 128))

    def rows(idx):
        if idx is None:
            return pl.BlockSpec((tm, hd), lambda i, j: (i, 0))
        return pl.BlockSpec((None, tm, hd), lambda i, j: (idx, i, 0))

    kern = functools.partial(_gated_norm_proj_kernel, head_dim=head_dim, silu_gate=silu_gate)
    return pl.pallas_call(
        kern,
        out_shape=jax.ShapeDtypeStruct((m, n), BF16),
        grid=(m // tm, n // tn),
        in_specs=[rows(fi), rows(bi),
                  pl.BlockSpec((tm, hd), lambda i, j: (i, gate_block)),
                  pl.BlockSpec((1, head_dim), lambda i, j: (0, 0)),
                  pl.BlockSpec((k, tn), lambda i, j: (0, j))],
        out_specs=pl.BlockSpec((tm, tn), lambda i, j: (i, j)),
        scratch_shapes=[pltpu.VMEM((tm, hd), BF16)],
        compiler_params=_cparams("parallel", "arbitrary"),
        name="gated_norm_proj",
    )(fa, ba, zz, norm_g.reshape(1, head_dim).astype(F32), w_o.astype(BF16))


def _resid_norm_kernel(*refs, n_branch, weighted, modulated, routed):
    refs = list(refs)
    x = refs.pop(0)[...].astype(F32)
    ys = [refs.pop(0)[...].astype(F32) for _ in range(n_branch)]
    if weighted:
        w = refs.pop(0)[...]
        ys = [y * w[:, k:k + 1] for k, y in enumerate(ys)]
    if n_branch:
        x = x + refs.pop(0)[...] * functools.reduce(lambda p, q: p + q, ys)
    g = refs.pop(0)[...]
    out = x * lax.rsqrt(jnp.mean(x * x, axis=-1, keepdims=True) + EPS) * g
    if modulated:
        sc = refs.pop(0)[...]
        sh = refs.pop(0)[...]
        out = out * (1.0 + sc) + sh
    wr = refs.pop(0) if routed else None
    if n_branch:
        refs.pop(0)[...] = x
    o_ref = refs.pop(0)
    o_ref[...] = out.astype(o_ref.dtype)
    if routed:
        logits = jnp.dot(out, wr[...], preferred_element_type=F32, precision=lax.Precision.HIGHEST)
        refs.pop(0)[...] = jax.nn.sigmoid(logits)


def resid_norm(x, n_ctx, gain, *, branches=(), weights=None, gate=None, sc=None, sh=None, w_router=None,
               latent_only=False):
    b, t, d = x.shape
    tm = n_ctx
    assert t % tm == 0
    off = 1 if latent_only else 0
    rows = pl.BlockSpec((None, tm, d), lambda i, j: (i, j + off, 0))
    out_rows = pl.BlockSpec((None, tm, d), lambda i, j: (i, j, 0))
    seg = pl.BlockSpec((None, None, 1, d), lambda i, j: (i, jnp.minimum(j + off, 1), 0, 0))
    args, specs = [x], [rows]
    for y in branches:
        args.append(y)
        specs.append(rows)
    if weights is not None:
        args.append(weights)
        specs.append(pl.BlockSpec((None, tm, weights.shape[-1]), lambda i, j: (i, j + off, 0)))
    if branches:
        args.append(gate)
        specs.append(seg)
    args.append(gain.reshape(1, d).astype(F32))
    specs.append(pl.BlockSpec((1, d), lambda i, j: (0, 0)))
    modulated = sc is not None
    if modulated:
        args += [sc, sh]
        specs += [seg, seg]
    routed = w_router is not None
    if routed:
        args.append(jnp.pad(w_router.astype(F32), ((0, 0), (0, ROUTER_PAD - w_router.shape[1]))))
        specs.append(pl.BlockSpec((d, ROUTER_PAD), lambda i, j: (0, 0)))
    t_out = t - off * tm
    out_shapes, out_specs = [], []
    if branches:
        out_shapes.append(jax.ShapeDtypeStruct((b, t_out, d), F32))
        out_specs.append(out_rows)
    out_shapes.append(jax.ShapeDtypeStruct((b, t_out, d), BF16 if modulated else F32))
    out_specs.append(out_rows)
    if routed:
        out_shapes.append(jax.ShapeDtypeStruct((b, t_out, ROUTER_PAD), F32))
        out_specs.append(pl.BlockSpec((None, tm, ROUTER_PAD), lambda i, j: (i, j, 0)))
    kern = functools.partial(_resid_norm_kernel, n_branch=len(branches), weighted=weights is not None,
                             modulated=modulated, routed=routed)
    return pl.pallas_call(
        kern,
        out_shape=tuple(out_shapes),
        grid=(b, t_out // tm),
        in_specs=specs,
        out_specs=tuple(out_specs),
        compiler_params=_cparams("parallel", "parallel"),
        name="resid_norm",
    )(*args)


def _moe_kernel(te_ref, tv_ref, x_ref, wg_ref, wu_ref, wd_ref, o_ref):
    del te_ref
    i = pl.program_id(0)

    @pl.when(tv_ref[i] > 0)
    def _():
        x = x_ref[...]
        g = jnp.dot(x, wg_ref[0], preferred_element_type=F32)
        u = jnp.dot(x, wu_ref[0], preferred_element_type=F32)
        act = (g * jax.nn.sigmoid(g) * u).astype(BF16)
        o_ref[...] = jnp.dot(act, wd_ref[0], preferred_element_type=F32).astype(o_ref.dtype)

    @pl.when(tv_ref[i] == 0)
    def _():
        o_ref[...] = jnp.zeros_like(o_ref)


def _route(aff, b_router):
    assert GROUP_SCORE_TOPK == 2 and TOP_K == 2
    epg = EXPERTS_PER_GROUP
    sel = aff + b_router.astype(F32)
    s = [sel[:, e] for e in range(N_EXPERTS)]
    a = [aff[:, e] for e in range(N_EXPERTS)]

    def first_max(vals):
        idx, best = jnp.zeros_like(vals[0], dtype=jnp.int32), vals[0]
        for e in range(1, len(vals)):
            upd = vals[e] > best
            idx, best = jnp.where(upd, e, idx), jnp.where(upd, vals[e], best)
        return idx, best

    def pick(vals, idx):
        out = vals[0]
        for e in range(1, len(vals)):
            out = jnp.where(idx == e, vals[e], out)
        return out

    def top2_sum(v):
        pairs = [v[i] + v[j] for i in range(len(v)) for j in range(i + 1, len(v))]
        return functools.reduce(jnp.maximum, pairs)

    g_best, _ = first_max([top2_sum(s[g * epg:(g + 1) * epg]) for g in range(N_GROUPS)])
    in_s = [pick([s[g * epg + e] for g in range(N_GROUPS)], g_best) for e in range(epg)]
    in_a = [pick([a[g * epg + e] for g in range(N_GROUPS)], g_best) for e in range(epg)]
    i1, _ = first_max(in_s)
    i2, _ = first_max([jnp.where(i1 == e, NEG_INF, in_s[e]) for e in range(epg)])
    w1, w2 = pick(in_a, i1), pick(in_a, i2)
    tot = w1 + w2
    expert_idx = jnp.stack([g_best * epg + i1, g_best * epg + i2], axis=-1)
    return expert_idx.astype(jnp.int32), jnp.stack([w1 / tot, w2 / tot], axis=-1)


def grouped_moe(h2, aff, b_router, w_gate, w_up, w_down):
    n, d = h2.shape
    expert_idx, w = _route(aff, b_router)
    n_slots = TOP_K * n
    e_flat = expert_idx.reshape(-1)
    onehot = (e_flat[:, None] == jnp.arange(N_EXPERTS, dtype=jnp.int32)[None, :]).astype(jnp.int32)
    csum = jnp.cumsum(onehot, axis=0)
    rank = jnp.take_along_axis(csum, e_flat[:, None], axis=1)[:, 0] - 1
    counts = csum[-1]
    padded = ((counts + MOE_TILE - 1) // MOE_TILE) * MOE_TILE
    pad_end = jnp.cumsum(padded)
    pad_off = pad_end - padded
    dest_flat = pad_off[e_flat] + rank
    dest = dest_flat.reshape(n, TOP_K)

    n_tiles = -(-n_slots // MOE_TILE) + N_EXPERTS
    n_pad_slots = n_tiles * MOE_TILE
    tile_start = jnp.arange(n_tiles, dtype=jnp.int32) * MOE_TILE
    tile_expert = jnp.minimum(jnp.sum((tile_start[:, None] >= pad_end[None, :]).astype(jnp.int32), axis=1),
                              N_EXPERTS - 1)
    tile_valid = (tile_start < pad_end[-1]).astype(jnp.int32)

    slot_token = jnp.zeros((n_pad_slots,), jnp.int32).at[dest_flat].set(
        jnp.arange(n_slots, dtype=jnp.int32) // TOP_K, unique_indices=True, mode="promise_in_bounds")

    xs = h2.at[slot_token].get(mode="promise_in_bounds")
    f = D_EXPERT
    ys = pl.pallas_call(
        _moe_kernel,
        out_shape=jax.ShapeDtypeStruct((n_pad_slots, d), BF16),
        grid_spec=pltpu.PrefetchScalarGridSpec(
            num_scalar_prefetch=2,
            grid=(n_tiles,),
            in_specs=[pl.BlockSpec((MOE_TILE, d), lambda i, te, tv: (i, 0)),
                      pl.BlockSpec((1, d, f), lambda i, te, tv: (te[i], 0, 0)),
                      pl.BlockSpec((1, d, f), lambda i, te, tv: (te[i], 0, 0)),
                      pl.BlockSpec((1, f, d), lambda i, te, tv: (te[i], 0, 0))],
            out_specs=pl.BlockSpec((MOE_TILE, d), lambda i, te, tv: (i, 0)),
        ),
        compiler_params=_cparams("arbitrary"),
        name="moe_experts",
    )(tile_expert, tile_valid, xs, w_gate.astype(BF16), w_up.astype(BF16), w_down.astype(BF16))
    return (ys.at[dest[:, 0]].get(mode="promise_in_bounds"), ys.at[dest[:, 1]].get(mode="promise_in_bounds"), w)


DA_TQ = 512


def _diff_attn_kernel(lam_ref, q_ref, k_ref, v_ref, cos_ref, sin_ref, g_ref, o_ref, k_scr, v_scr,
                      *, n_ctx, lam_init):
    i = pl.program_id(2)
    t = k_ref.shape[0]
    hw = 2 * DA_HALF
    lam = lam_ref[0]
    lane = lax.broadcasted_iota(jnp.int32, (1, hw), 1)
    first_half = (lane % (DA_HALF // 2)) < DA_HALF // 4
    nt = (((1,), (1,)), ((), ()))

    def rope(x, lo, n):
        xf = x.astype(F32)
        partner = jnp.where(first_half, pltpu.roll(xf, hw - DA_HALF // 4, axis=1), pltpu.roll(xf, DA_HALF // 4, axis=1))
        return xf * cos_ref[pl.ds(lo, n), :] + partner * sin_ref[pl.ds(lo, n), :]

    @pl.when(i == 0)
    def _():
        def body(c, carry):
            lo = pl.multiple_of(c * n_ctx, n_ctx)
            k_scr[pl.ds(lo, n_ctx), :] = rope(k_ref[pl.ds(lo, n_ctx), :], lo, n_ctx).astype(BF16)
            return carry

        lax.fori_loop(0, t // n_ctx, body, 0)
        v_scr[:, :hw] = v_ref[...]
        v_scr[:, hw:] = jnp.ones((t, hw), BF16)

    def attend(lo, n, k, v_ext):
        q = rope(q_ref[pl.ds(lo, n), :], lo, n)

        def one_map(in_map):
            qm = jnp.where(in_map, q, 0.0).astype(BF16)
            s = lax.dot_general(qm, k, nt, preferred_element_type=F32)
            p = jnp.exp((s - jnp.max(s, axis=-1, keepdims=True)).astype(BF16))
            oe = jnp.dot(p, v_ext, preferred_element_type=F32)
            return oe[:, :hw] * (1.0 / oe[:, hw:hw + 1])

        o = one_map(lane < DA_HALF) - lam * one_map(lane >= DA_HALF)
        o = o * lax.rsqrt(jnp.mean(o * o, axis=-1, keepdims=True) + 1e-5)
        o_ref[pl.ds(lo, n), :] = ((o * g_ref[...]) * (1.0 - lam_init)).astype(o_ref.dtype)

    @pl.when(i == 0)
    def _():
        attend(0, n_ctx, k_scr[:n_ctx, :], v_scr[:n_ctx, :])

    @pl.when(i > 0)
    def _():
        attend(pl.multiple_of(n_ctx + (i - 1) * DA_TQ, n_ctx), DA_TQ, k_scr[...], v_scr[...])


def diff_attn_core(qkv, cos, sin, lam, subln_g, n_ctx, lam_init):
    b, t, _ = qkv.shape
    assert (t - n_ctx) % DA_TQ == 0 and DA_TQ % n_ctx == 0
    nh = DA_HEADS
    hw = 2 * DA_HALF
    kern = functools.partial(_diff_attn_kernel, n_ctx=n_ctx, lam_init=lam_init)
    table = pl.BlockSpec((t, hw), lambda bi, h, i: (0, 0))

    def col(off):
        return pl.BlockSpec((None, t, hw), lambda bi, h, i: (bi, 0, off + h))

    return pl.pallas_call(
        kern,
        out_shape=jax.ShapeDtypeStruct((b, t, nh * hw), BF16),
        grid=(b, nh, 1 + (t - n_ctx) // DA_TQ),
        in_specs=[pl.BlockSpec(memory_space=pltpu.SMEM), col(0), col(nh), col(2 * nh), table, table,
                  pl.BlockSpec((1, hw), lambda bi, h, i: (0, 0))],
        out_specs=col(0),
        scratch_shapes=[pltpu.VMEM((t, hw), BF16), pltpu.VMEM((t, 2 * hw), BF16)],
        compiler_params=_cparams("parallel", "parallel", "arbitrary"),
        name="diff_attn",
    )(lam.reshape(1).astype(F32), qkv, qkv, qkv, cos, sin, subln_g.reshape(1, hw).astype(F32))


NA_QROWS = 8
NA_KROWS = 2 * NA_QROWS
NA_DR_PAD = 16
NEG_INF = float("-inf")
NA_BIAS_ROWS = 2 * WIN_R - 1
NA_BIAS_COLS = 2 * WIN_C - 1


def _na_kernel(rpb_ref, q_ref, k_ref, v_ref, o_ref, bias_ref, *, n_ctx, rows):
    h = pl.program_id(0)
    b = pl.program_id(1)
    j = pl.program_id(2)
    w = GRID_W
    nq = NA_QROWS * w
    nk = NA_KROWS * w
    lane = lax.broadcasted_iota(jnp.int32, (w, 2 * w), 1)
    nt = (((1,), (1,)), ((), ()))

    @pl.when((b == 0) & (j == 0))
    def _():
        c = lax.broadcasted_iota(jnp.int32, (w, 2 * w), 0)
        kc = lane % w
        c0 = jnp.clip(c - WIN_C // 2, 0, w - WIN_C)
        in_win = (kc >= c0) & (kc < c0 + WIN_C)
        single = []
        for dr in range(-(WIN_R - 1), WIN_R):
            t = jnp.full((w, 2 * w), NEG_INF, F32)
            for dc in range(-(WIN_C - 1), WIN_C):
                t = jnp.where(kc - c == dc, rpb_ref[(h * NA_BIAS_ROWS + dr + WIN_R - 1) * NA_BIAS_COLS + dc + WIN_C - 1], t)
            single.append(jnp.where(in_win, t, NEG_INF))
        neg = jnp.full((w, 2 * w), NEG_INF, F32)

        def at(dr):
            return single[dr + WIN_R - 1] if abs(dr) < WIN_R else neg

        for d in range(2 * NA_DR_PAD + 1):
            bias_ref[d] = jnp.where(lane < w, at(d - NA_DR_PAD), at(d - NA_DR_PAD + 1))

    kctx = k_ref[0, :n_ctx, :]
    vctx = v_ref[0, :n_ctx, :]

    def finish(parts, q_lo, n):
        m = parts[0][0].max(axis=-1, keepdims=True)
        for s, _ in parts[1:]:
            m = jnp.maximum(m, s.max(axis=-1, keepdims=True))
        l = 0.0
        o = 0.0
        for s, vv in parts:
            p = jnp.exp(s - m)
            l = l + p.sum(axis=-1, keepdims=True)
            o = o + jnp.dot(p.astype(BF16), vv, preferred_element_type=F32)
        o_ref[0, pl.ds(q_lo, n), :] = (o * (1.0 / l)).astype(o_ref.dtype)

    @pl.when(j < rows // NA_QROWS)
    def _():
        r_lo = j * NA_QROWS
        ks = jnp.clip(r_lo - WIN_R // 2, 0, rows - NA_KROWS)
        q_lo = pl.multiple_of(n_ctx + r_lo * w, w)
        k_lo = pl.multiple_of(n_ctx + ks * w, w)
        q = q_ref[0, pl.ds(q_lo, nq), :]
        kwin = k_ref[0, pl.ds(k_lo, nk), :]
        vwin = v_ref[0, pl.ds(k_lo, nk), :]
        s_win = lax.dot_general(q, kwin, nt, preferred_element_type=F32)
        half = (lane >= w).astype(jnp.int32)
        row_blocks = []
        for i in range(NA_QROWS):
            r = r_lo + i
            r0 = jnp.clip(r - WIN_R // 2, 0, rows - WIN_R)
            tiles = []
            for jj in range(NA_KROWS // 2):
                kr = ks + 2 * jj
                off = lax.bitcast_convert_type(half + (kr - r0), jnp.uint32)
                t = s_win[i * w:(i + 1) * w, 2 * jj * w:(2 * jj + 2) * w] + bias_ref[kr - r + NA_DR_PAD]
                tiles.append(jnp.where(off < WIN_R, t, NEG_INF))
            row_blocks.append(jnp.concatenate(tiles, axis=1))
        s_win = jnp.concatenate(row_blocks, axis=0)
        s_ctx = lax.dot_general(q, kctx, nt, preferred_element_type=F32)
        finish([(s_win, vwin), (s_ctx, vctx)], q_lo, nq)

    @pl.when(j == rows // NA_QROWS)
    def _():
        q = q_ref[0, :n_ctx, :]
        finish([(lax.dot_general(q, kctx, nt, preferred_element_type=F32), vctx)], 0, n_ctx)


def na_attn_core(qkv, rpb, n_ctx):
    b, t, _ = qkv.shape
    rows = (t - n_ctx) // GRID_W
    assert rows % NA_QROWS == 0 and rows >= NA_KROWS
    nh, hd = NA_HEADS, NA_DIM
    kern = functools.partial(_na_kernel, n_ctx=n_ctx, rows=rows)

    def col(off):
        return pl.BlockSpec((1, t, hd), lambda h, bi, j: (bi, 0, off + h))

    return pl.pallas_call(
        kern,
        out_shape=jax.ShapeDtypeStruct((b, t, nh * hd), BF16),
        grid=(nh, b, rows // NA_QROWS + 1),
        in_specs=[pl.BlockSpec(memory_space=pltpu.SMEM), col(0), col(nh), col(2 * nh)],
        out_specs=col(0),
        scratch_shapes=[pltpu.VMEM((2 * NA_DR_PAD + 1, GRID_W, 2 * GRID_W), F32)],
        compiler_params=_cparams("arbitrary", "arbitrary", "arbitrary"),
        name="na_attn",
    )(rpb.astype(F32).reshape(-1), qkv, qkv, qkv)


def rmsnorm(x, g, eps=EPS):
    xf = x.astype(F32)
    y = xf * lax.rsqrt(jnp.mean(xf * xf, axis=-1, keepdims=True) + eps)
    return y.astype(x.dtype) * g


def l2norm(x):
    return x * lax.rsqrt(jnp.sum(x * x, axis=-1, keepdims=True) + EPS)


def flip(t):
    return None if t is None else jnp.flip(t, axis=1)


def proj(t, w, out_dtype=F32):
    b, tt, k = t.shape
    return mm(t.reshape(b * tt, k), w, out_dtype).reshape(b, tt, w.shape[1])


def axial_rope_tables(length, dim):
    n_freq = dim // 4
    t = jnp.arange(length)
    pos = jnp.stack([t // GRID_W, t % GRID_W], axis=-1).astype(F32)
    inv = ROPE_BASE ** (-jnp.arange(n_freq, dtype=F32) / n_freq)
    ang = pos[:, :, None] * inv
    return jnp.cos(ang), jnp.sin(ang)


def apply_rope(x, cos, sin):
    n_freq = cos.shape[-1]
    xr = x.reshape(*x.shape[:-1], 2, 2, n_freq)
    x1, x2 = xr[..., 0, :], xr[..., 1, :]
    c = cos[:, None].astype(x.dtype)
    s = sin[:, None].astype(x.dtype)
    out = jnp.stack([x1 * c - x2 * s, x1 * s + x2 * c], axis=-2)
    return out.reshape(x.shape)


def diff_attention(h, n_ctx, w_qkv, w_o, lam_vec, subln_g, layer_idx):
    t = h.shape[1]
    dq = 2 * DA_HEADS * DA_HALF
    lam_init = 0.8 - 0.6 * math.exp(-0.3 * layer_idx)
    lv = lam_vec.astype(F32)
    lam = jnp.exp(jnp.sum(lv[0] * lv[1])) - jnp.exp(jnp.sum(lv[2] * lv[3])) + lam_init
    w = jnp.concatenate([w_qkv[:, :dq] * DA_HALF ** -0.5, w_qkv[:, dq:]], axis=1)
    cos, sin = axial_rope_tables(t - n_ctx, DA_HALF)
    n_freq = cos.shape[-1]
    sign = jnp.array([-1.0, 1.0], F32)[None, None, :, None]
    cos_t = jnp.broadcast_to(cos[:, :, None, :], (t - n_ctx, 2, 2, n_freq)).reshape(t - n_ctx, DA_HALF)
    sin_t = (sin[:, :, None, :] * sign).reshape(t - n_ctx, DA_HALF)
    cos_t = jnp.concatenate([jnp.ones((n_ctx, 2 * DA_HALF), F32), jnp.tile(cos_t, (1, 2))], axis=0)
    sin_t = jnp.concatenate([jnp.zeros((n_ctx, 2 * DA_HALF), F32), jnp.tile(sin_t, (1, 2))], axis=0)
    o = diff_attn_core(proj(h, w, BF16), cos_t, sin_t, lam, subln_g, n_ctx, lam_init)
    return proj(o, w_o, BF16)


def centred_dwconv(x, w):
    k = w.shape[0]
    pad = k // 2
    t = x.shape[1]
    xp = jnp.pad(x, ((0, 0), (pad, pad), (0, 0)))
    out = xp[:, 0:t] * w[0]
    for j in range(1, k):
        out = out + xp[:, j:j + t] * w[j]
    return out


GDN_PREP_ROWS = 128
GDN_HALO = 8


def _gdn_prep_kernel(cur_ref, prev_ref, next_ref, w_ref, q_ref, k_ref, v_ref, *, first_latent_tile):
    j = pl.program_id(1)
    nt = pl.num_programs(1)
    tm = GDN_PREP_ROWS
    taps = w_ref.shape[0]
    left_ok = (j != 0) & (j != first_latent_tile)
    right_ok = (j != first_latent_tile - 1) & (j != nt - 1)
    n_ext = tm + 2 * GDN_HALO
    for grp, out_ref in enumerate((q_ref, k_ref, v_ref)):
        cols = slice(grp * out_ref.shape[1], (grp + 1) * out_ref.shape[1])
        prev = jnp.where(left_ok, prev_ref[:, cols], 0.0)
        nxt = jnp.where(right_ok, next_ref[:, cols], 0.0)
        ext = jnp.concatenate([prev, cur_ref[:, cols], nxt], axis=0)
        conv = None
        for tap in range(taps):
            shift = (taps // 2 - tap) % n_ext
            x = (pltpu.roll(ext, shift, axis=0) if shift else ext)[GDN_HALO:GDN_HALO + tm]
            term = x * w_ref[tap:tap + 1, cols]
            conv = term if conv is None else conv + term
        z = conv * jax.nn.sigmoid(conv)
        if out_ref is v_ref:
            out_ref[...] = z.astype(out_ref.dtype)
            continue
        scale = GDN_DK ** -0.5 if out_ref is q_ref else 1.0
        for h in range(GDN_HEADS):
            sl = slice(h * GDN_DK, (h + 1) * GDN_DK)
            zh = z[:, sl]
            out_ref[:, sl] = (zh * (lax.rsqrt(jnp.sum(zh * zh, axis=-1, keepdims=True) + EPS) * scale)
                              ).astype(out_ref.dtype)


def gdn_prep(zz, conv_w, n_ctx):
    b, t, _ = zz.shape
    tm, halo = GDN_PREP_ROWS, GDN_HALO
    hd = GDN_HEADS * GDN_DK
    wid = 3 * hd
    assert n_ctx % tm == 0 and t % tm == 0 and conv_w.shape[0] // 2 <= halo
    per = tm // halo
    out = pl.BlockSpec((None, tm, hd), lambda i, j: (i, j, 0))
    kern = functools.partial(_gdn_prep_kernel, first_latent_tile=n_ctx // tm)
    return pl.pallas_call(
        kern,
        out_shape=tuple(jax.ShapeDtypeStruct((b, t, hd), BF16) for _ in range(3)),
        grid=(b, t // tm),
        in_specs=[pl.BlockSpec((None, tm, wid), lambda i, j: (i, j, 0)),
                  pl.BlockSpec((None, halo, wid), lambda i, j: (i, jnp.maximum(j * per - 1, 0), 0)),
                  pl.BlockSpec((None, halo, wid), lambda i, j: (i, jnp.minimum((j + 1) * per, t // halo - 1), 0)),
                  pl.BlockSpec((conv_w.shape[0], wid), lambda i, j: (0, 0))],
        out_specs=(out, out, out),
        compiler_params=_cparams("parallel", "parallel"),
        name="gdn_prep",
    )(zz, zz, zz, conv_w.astype(F32))


GDN_HEAD_GROUP = 16
GDN_SUB = 16


def _bdot(a, b):
    return jnp.dot(a.astype(BF16), b.astype(BF16), preferred_element_type=F32)


def _scan_chunk(d, s, n, n_ctx_chunks):
    back = jnp.where(s < n_ctx_chunks, n_ctx_chunks - 1 - s, n + n_ctx_chunks - 1 - s)
    return jnp.where(d == 0, s, back)


def _gdn_kernel(qs_ref, k_ref, v_ref, gc_ref, bc_ref, gr_ref, o_ref, s_ref):
    d = pl.program_id(0)
    s = pl.program_id(3)
    c = GDN_CHUNK

    @pl.when(s == 0)
    def _():
        s_ref[...] = jnp.zeros_like(s_ref)

    row = lax.broadcasted_iota(jnp.int32, (c, c), 0)
    col = lax.broadcasted_iota(jnp.int32, (c, c), 1)
    ahead = (row - col) * (1 - 2 * d)
    strict = ahead > 0
    incl = ahead >= 0
    same_blk = (row // GDN_SUB) == (col // GDN_SUB)
    eye = (row == col).astype(F32)
    nt = (((1,), (1,)), ((), ()))
    tn = (((0,), (0,)), ((), ()))
    heads = range(GDN_HEAD_GROUP)
    sls = [slice(g * GDN_DK, (g + 1) * GDN_DK) for g in heads]

    def each(fn, *lists):
        return [fn(*vals) for vals in zip(*lists)]

    gcol = gc_ref[...]
    g_last = jnp.min(gcol, axis=0, keepdims=True)
    e_g, e_rest, e_all = jnp.exp(gcol), jnp.exp(g_last - gcol), jnp.exp(g_last)
    beta = bc_ref[...]
    ks = [k_ref[:, sl] for sl in sls]
    kf = [k.astype(F32) for k in ks]
    kb = [k * beta[:, g:g + 1] for g, k in zip(heads, kf)]
    kbg = [(k * e_g[:, g:g + 1]).astype(BF16) for g, k in zip(heads, kb)]
    kd = [(k * e_rest[:, g:g + 1]).astype(BF16) for g, k in zip(heads, kf)]
    vb = [(v_ref[:, sl].astype(F32) * beta[:, g:g + 1]).astype(BF16) for g, sl in zip(heads, sls)]
    qg = [(qs_ref[:, sl].astype(F32) * e_g[:, g:g + 1]).astype(BF16) for g, sl in zip(heads, sls)]
    decay = [jnp.exp(jnp.where(incl, gcol[:, g:g + 1] - gr_ref[g:g + 1, :], NEG_INF)) for g in heads]
    a = each(lambda m, k, dc: jnp.where(
        strict, lax.dot_general(m.astype(BF16), k, nt, preferred_element_type=F32) * dc, 0.0), kb, ks, decay)
    qk = each(lambda sl, k, dc: jnp.where(
        incl, lax.dot_general(qs_ref[:, sl], k, nt, preferred_element_type=F32) * dc, 0.0), sls, ks, decay)
    dblk = each(lambda m: jnp.where(same_blk, m, 0.0), a)
    d2 = each(lambda m: _bdot(m, m), dblk)
    x = each(lambda m, m2: _bdot(eye - m, eye + m2), dblk, d2)
    d4 = each(lambda m: _bdot(m, m), d2)
    x = each(lambda m, m4: _bdot(m, eye + m4), x, d4)
    d8 = each(lambda m: _bdot(m, m), d4)
    x = each(lambda m, m8: _bdot(m, eye + m8), x, d8)
    nmat = each(lambda m, am, dm: _bdot(m, am - dm), x, a, dblk)
    n2 = each(lambda m: _bdot(m, m), nmat)
    y = each(lambda m, m2: _bdot(eye - m, eye + m2), nmat, n2)
    t = each(_bdot, y, x)
    uw = each(lambda m, p, q: _bdot(m, jnp.concatenate([p, q], axis=1)), t, vb, kbg)
    state = [s_ref[g] for g in heads]
    v_new = each(lambda m, st: m[:, :GDN_DV] - _bdot(m[:, GDN_DV:], st), uw, state)
    o_state = each(_bdot, qg, state)
    o_new = each(_bdot, qk, v_new)
    s_new = each(lambda m, vn: lax.dot_general(m, vn.astype(BF16), tn, preferred_element_type=F32), kd, v_new)
    for g in heads:
        s_ref[g] = state[g] * e_all[:, g:g + 1] + s_new[g]
    o_ref[...] = jnp.concatenate(each(lambda p, q: p + q, o_state, o_new), axis=1)


def gated_delta_core(qs, k, v, g, beta, n_ctx_chunks):
    b, t, hd = k.shape
    c, hh, gg = GDN_CHUNK, GDN_HEADS, GDN_HEAD_GROUP
    n = t // c
    gw = gg * GDN_DK

    def cols(a):
        return a.reshape(2, b, t, hh // gg, gg).transpose(0, 1, 3, 2, 4)

    g_row = g.reshape(2, b, n, c, hh // gg, gg).transpose(0, 1, 2, 4, 5, 3)

    def chunk_of(d, s):
        return _scan_chunk(d, s, n, n_ctx_chunks)

    shared = pl.BlockSpec((None, c, gw), lambda d, bi, hg, s: (bi, chunk_of(d, s), hg))
    col_spec = pl.BlockSpec((None, None, None, c, gg), lambda d, bi, hg, s: (d, bi, hg, chunk_of(d, s), 0))
    return pl.pallas_call(
        _gdn_kernel,
        out_shape=jax.ShapeDtypeStruct((2, b, t, hd), F32),
        grid=(2, b, hh // gg, n),
        in_specs=[shared, shared, shared, col_spec, col_spec,
                  pl.BlockSpec((None, None, None, None, gg, c), lambda d, bi, hg, s: (d, bi, chunk_of(d, s), hg, 0, 0))],
        out_specs=pl.BlockSpec((None, None, c, gw), lambda d, bi, hg, s: (d, bi, chunk_of(d, s), hg)),
        scratch_shapes=[pltpu.VMEM((gg, GDN_DK, GDN_DV), F32)],
        compiler_params=_cparams("parallel", "parallel", "parallel", "arbitrary"),
        name="gated_delta",
    )(qs, k, v, cols(g), cols(beta), g_row)


def gated_deltanet(h, n_ctx, w_in, conv_w, w_ab, dt_bias, a_log, norm_g, w_o):
    B, T, _ = h.shape
    tc = n_ctx
    H, C = GDN_HEADS, GDN_CHUNK
    wq = H * GDN_DK
    hi = 2 * wq + H * GDN_DV

    zz = proj(h, w_in)
    q, k, v = gdn_prep(zz, conv_w, tc)
    ab = proj(h, w_ab).astype(F32).reshape(B, T, 2, 2, H)
    la = -jnp.exp(a_log.astype(F32)) * jax.nn.softplus(ab[:, :, 0] + dt_bias.astype(F32))
    be = jax.nn.sigmoid(ab[:, :, 1])
    n = T // C
    la_c = la.reshape(B, n, C, 2, H)
    g_f = jnp.cumsum(la_c[:, :, :, 0], axis=2)
    g_b = jnp.flip(jnp.cumsum(jnp.flip(la_c[:, :, :, 1], axis=2), axis=2), axis=2)
    g = jnp.stack([g_f, g_b]).reshape(2, B, T, H)
    o = gated_delta_core(q, k, v, g, jnp.moveaxis(be, 2, 0), tc // C)
    o = o.reshape(2, B * T, -1)
    y = gated_norm_proj((o, 0), (o, 1), zz.reshape(B * T, -1), hi // (H * GDN_DV), norm_g, w_o, silu_gate=True)
    return y.reshape(B, T, -1)


def neighbourhood_attention(h, n_ctx, w_qkv, rpb, w_o):
    hd = NA_HEADS * NA_DIM
    w = jnp.concatenate([w_qkv[:, :hd] * NA_DIM ** -0.5, w_qkv[:, hd:]], axis=1)
    return proj(na_attn_core(proj(h, w, BF16), rpb, n_ctx), w_o, BF16)


GLA_CHUNK = 64
GLA_LEVELS = (1, 2, 4, 8, 16, 32)


def _gla_kernel(zq_ref, zv_ref, zf_ref, lb_ref, bf_ref, o_ref, st_ref, *, reverse):
    s = pl.program_id(1)
    c = GLA_CHUNK

    @pl.when(s == 0)
    def _():
        st_ref[...] = jnp.zeros_like(st_ref)

    row = lax.broadcasted_iota(jnp.int32, (c, c), 0)
    col = lax.broadcasted_iota(jnp.int32, (c, c), 1)
    late, early = (col, row) if reverse else (row, col)
    masks = []
    for m in GLA_LEVELS:
        masks.append(((row // (2 * m)) == (col // (2 * m))) & ((late % (2 * m)) >= m) & ((early % (2 * m)) < m))
    diag = row == col
    trow = lax.broadcasted_iota(jnp.int32, (c, HG_DK), 0)
    nt = (((1,), (1,)), ((), ()))
    tn = (((0,), (0,)), ((), ()))
    heads = range(HG_HEADS)
    sls = [slice(g * HG_DK, (g + 1) * HG_DK) for g in heads]

    def ref_rows(b, m):
        p = m if reverse else m - 1
        if 2 * m >= 8:
            blocks = b.reshape(c // (2 * m), 2 * m, HG_DK)
            return jnp.broadcast_to(blocks[:, p:p + 1, :], blocks.shape).reshape(c, HG_DK)
        out = b
        for rho in range(2 * m):
            if rho != p:
                out = jnp.where(trow % (2 * m) == rho, pltpu.roll(b, (rho - p) % c, axis=0), out)
        return out

    zq = zq_ref[...]
    q_all = zq * jax.nn.sigmoid(zq) * HG_DK ** -0.5
    v_all = zv_ref[...].astype(BF16)
    z = zf_ref[...] + bf_ref[...]
    t = jnp.exp(-jnp.abs(z))
    r = 1.0 / (1.0 + t)
    pos = z >= 0
    lb = lb_ref[...]
    log_f = jnp.log(lb + (1.0 - lb) * jnp.where(pos, r, t * r))
    k_all = (1.0 - lb) * jnp.where(pos, t * r, r)
    scan = (col >= row) if reverse else (col <= row)
    b_all = jnp.dot(scan.astype(F32), log_f, preferred_element_type=F32, precision=lax.Precision.HIGHEST)
    qs = [q_all[:, sl] for sl in sls]
    ks = [k_all[:, sl] for sl in sls]
    bs = [b_all[:, sl] for sl in sls]
    vs = [v_all[:, sl] for sl in sls]
    att = [jnp.where(diag, lax.dot_general(q.astype(BF16), k.astype(BF16), nt, preferred_element_type=F32), 0.0)
           for q, k in zip(qs, ks)]
    for m, mask in zip(GLA_LEVELS, masks):
        es = [jnp.exp(-jnp.abs(b - ref_rows(b, m))) for b in bs]
        sc = [lax.dot_general((q * e).astype(BF16), (k * e).astype(BF16), nt, preferred_element_type=F32)
              for q, k, e in zip(qs, ks, es)]
        att = [jnp.where(mask, x, a) for x, a in zip(sc, att)]
    last = c - 1 if not reverse else 0
    b_last = [b[last:last + 1, :] for b in bs]
    state = [st_ref[g] for g in heads]
    o_state = [lax.dot_general((q * jnp.exp(b)).astype(BF16), st.astype(BF16), nt, preferred_element_type=F32)
               for q, b, st in zip(qs, bs, state)]
    o_new = [jnp.dot(a.astype(BF16), v, preferred_element_type=F32) for a, v in zip(att, vs)]
    s_new = [lax.dot_general(v, (k * jnp.exp(bl - b)).astype(BF16), tn, preferred_element_type=F32)
             for v, k, b, bl in zip(vs, ks, bs, b_last)]
    for g in heads:
        st_ref[g] = state[g] * jnp.exp(b_last[g]) + s_new[g]
    o_ref[...] = jnp.concatenate([x + y for x, y in zip(o_state, o_new)], axis=1)


def gla_core(zz, lb, b_f, n_ctx_chunks, reverse):
    bsz, t, _ = zz.shape
    hd = HG_HEADS * HG_DK
    c = GLA_CHUNK
    n = t // c

    def chunk_of(s):
        return _scan_chunk(1, s, n, n_ctx_chunks) if reverse else s

    def cols(j):
        return pl.BlockSpec((None, c, hd), lambda bi, s: (bi, chunk_of(s), j))

    vec = pl.BlockSpec((1, hd), lambda bi, s: (0, 0))
    return pl.pallas_call(
        functools.partial(_gla_kernel, reverse=reverse),
        out_shape=jax.ShapeDtypeStruct((bsz, t, hd), F32),
        grid=(bsz, n),
        in_specs=[cols(0), cols(1), cols(3 if reverse else 2), vec, vec],
        out_specs=cols(0),
        scratch_shapes=[pltpu.VMEM((HG_HEADS, HG_DV, HG_DK), F32)],
        compiler_params=_cparams("parallel", "arbitrary"),
        name="gla_bwd" if reverse else "gla_fwd",
    )(zz, zz, zz, lb.reshape(1, hd).astype(F32), b_f.reshape(1, hd).astype(F32))


def hgrn2(h, n_ctx, w_q, w_i, w_f, b_f, w_g, norm_g, w_o, lb):
    B, T, _ = h.shape
    C = GLA_CHUNK
    hw = HG_HEADS * HG_DK

    zz = proj(h, jnp.concatenate([w_q, w_i, w_f[0], w_f[1], w_g], axis=1))
    o_f = gla_core(zz, lb, b_f[0], n_ctx // C, False).reshape(B * T, hw)
    o_b = gla_core(zz, lb, b_f[1], n_ctx // C, True).reshape(B * T, hw)
    y = gated_norm_proj((o_f, None), (o_b, None), zz.reshape(B * T, -1), 4, norm_g, w_o, silu_gate=False)
    return y.reshape(B, T, -1)


def kernel(x, c, ctx, c_ctx, w_mod, b_mod, norm_mix_g, norm_ffn_g, da_w_qkv, da_w_o, da_lam, da_subln_g, gdn_w_in, gdn_conv, gdn_w_ab, gdn_dt_bias, gdn_a_log, gdn_norm_g, gdn_w_o, na_w_qkv, na_rpb, na_w_o, hg_w_q, hg_w_i, hg_w_f, hg_b_f, hg_w_g, hg_norm_g, hg_w_o, hg_lb_logits, w_router, b_router, e_w_gate, e_w_up, e_w_down, final_norm_g):
    B, L, D = x.shape
    tc = ctx.shape[1]
    T = tc + L
    xa = jnp.concatenate([ctx, x], axis=1)
    p_lb = jax.nn.softmax(hg_lb_logits.astype(F32), axis=0)
    lb_all = jnp.cumsum(p_lb, axis=0) - p_lb[0]
    cond_all = jnp.concatenate([jax.nn.silu(c), jax.nn.silu(c_ctx)[None]], axis=0)
    cond_all = jnp.pad(cond_all, ((0, -(B + 1) % LANES), (0, 0)))

    def mods(i):
        mod = mm(cond_all, w_mod[i])[:B + 1] + b_mod[i]
        both = jnp.stack([jnp.broadcast_to(mod[B], (B, 6 * D)), mod[:B]], axis=1)
        return [both[:, :, None, k * D:(k + 1) * D] for k in range(6)]

    sh1, sc1, g1, sh2, sc2, g2 = mods(0)
    (h,) = resid_norm(xa, tc, norm_mix_g[0], sc=sc1, sh=sh1)
    for i in range(DEPTH):
        kind, j = i % N_MIXERS, i // N_MIXERS
        if kind == 0:
            y = diff_attention(h, tc, da_w_qkv[j], da_w_o[j], da_lam[j], da_subln_g[j], i)
        elif kind == 1:
            y = gated_deltanet(h, tc, gdn_w_in[j], gdn_conv[j], gdn_w_ab[j], gdn_dt_bias[j],
                               gdn_a_log[j], gdn_norm_g[j], gdn_w_o[j])
        elif kind == 2:
            y = neighbourhood_attention(h, tc, na_w_qkv[j], na_rpb[j], na_w_o[j])
        else:
            y = hgrn2(h, tc, hg_w_q[j], hg_w_i[j], hg_w_f[j], hg_b_f[j], hg_w_g[j],
                      hg_norm_g[j], hg_w_o[j], lb_all[i])
        xa, h2, aff = resid_norm(xa, tc, norm_ffn_g[i], branches=(y,), gate=g1, sc=sc2, sh=sh2, w_router=w_router)
        ya, yb, w = grouped_moe(h2.reshape(B * T, D), aff.reshape(B * T, ROUTER_PAD)[:, :N_EXPERTS],
                                b_router, e_w_gate[i], e_w_up[i], e_w_down[i])
        branches = (ya.reshape(B, T, D), yb.reshape(B, T, D))
        w = w.reshape(B, T, TOP_K)
        if i == DEPTH - 1:
            return resid_norm(xa, tc, final_norm_g, branches=branches, weights=w, gate=g2, latent_only=True)[1]
        gate_ffn = g2
        sh1, sc1, g1, sh2, sc2, g2 = mods(i + 1)
        xa, h = resid_norm(xa, tc, norm_mix_g[i + 1], branches=branches, weights=w, gate=gate_ffn, sc=sc1, sh=sh1)
```

```python
import functools
import math

import jax
import jax.numpy as jnp
from jax import lax
from jax.experimental import pallas as pl
from jax.experimental.pallas import tpu as pltpu

F32 = jnp.float32
BF16 = jnp.bfloat16

D_MODEL = 2048
DEPTH = 4
GRID_W = 64
N_MIXERS = 4
EPS = 1e-6
DA_HEADS = 16
DA_HALF = D_MODEL // DA_HEADS // 2
DA_VDIM = 2 * DA_HALF
ROPE_BASE = 10000.0
GDN_HEADS = 16
GDN_DK = D_MODEL // GDN_HEADS
GDN_DV = D_MODEL // GDN_HEADS
GDN_CHUNK = 64
NA_HEADS = 16
NA_DIM = D_MODEL // NA_HEADS
WIN_R = 8
WIN_C = 16
HG_HEADS = 16
HG_DK = D_MODEL // HG_HEADS
HG_DV = D_MODEL // HG_HEADS
HG_CHUNK = 32
N_EXPERTS = 16
N_GROUPS = 4
EXPERTS_PER_GROUP = N_EXPERTS // N_GROUPS
GROUP_SCORE_TOPK = 2
TOP_K = 2
D_EXPERT = D_MODEL // 2

V7X_VMEM_LIMIT_BYTES = 56 * 1024 * 1024
LANES = 128
MOE_TILE = 512
ROUTER_PAD = LANES


def _largest_divisor(n, candidates):
    for c in candidates:
        if n % c == 0:
            return c
    raise ValueError(f"no tile in {candidates} divides {n}")


def _cparams(*sem):
    return pltpu.CompilerParams(dimension_semantics=sem, vmem_limit_bytes=V7X_VMEM_LIMIT_BYTES)


def _mm_kernel(a_ref, w_ref, o_ref):
    o_ref[...] = jnp.dot(a_ref[...], w_ref[...], preferred_element_type=F32).astype(o_ref.dtype)


def mm(a, w, out_dtype=F32):
    m, k = a.shape
    n = w.shape[1]
    n_pad = -n % LANES
    if n_pad:
        w = jnp.pad(w, ((0, 0), (0, n_pad)))
    a = a.astype(BF16)
    w = w.astype(BF16)
    np_ = n + n_pad
    tm = _largest_divisor(m, (1024, 512, 256, 128))
    tn = _largest_divisor(np_, (512, 256, 128))
    out = pl.pallas_call(
        _mm_kernel,
        out_shape=jax.ShapeDtypeStruct((m, np_), out_dtype),
        grid=(m // tm, np_ // tn),
        in_specs=[pl.BlockSpec((tm, k), lambda i, j: (i, 0)),
                  pl.BlockSpec((k, tn), lambda i, j: (0, j))],
        out_specs=pl.BlockSpec((tm, tn), lambda i, j: (i, j)),
        compiler_params=_cparams("parallel", "arbitrary"),
        name="proj_mm",
    )(a, w)
    return out[:, :n] if n_pad else out


def _gated_norm_proj_kernel(of_ref, ob_ref, zg_ref, g_ref, w_ref, o_ref, a_scr, *, head_dim, silu_gate):
    @pl.when(pl.program_id(1) == 0)
    def _():
        o = of_ref[...] + ob_ref[...]
        zg = zg_ref[...].astype(F32)
        sig = jax.nn.sigmoid(zg)
        gate = zg * sig if silu_gate else sig
        for h in range(o.shape[1] // head_dim):
            sl = slice(h * head_dim, (h + 1) * head_dim)
            oh = o[:, sl]
            oh = oh * lax.rsqrt(jnp.mean(oh * oh, axis=-1, keepdims=True) + EPS) * g_ref[...]
            a_scr[:, sl] = (oh * gate[:, sl]).astype(BF16)

    o_ref[...] = jnp.dot(a_scr[...], w_ref[...], preferred_element_type=F32).astype(o_ref.dtype)


def gated_norm_proj(o_fwd, o_bwd, zz, gate_block, norm_g, w_o, *, silu_gate):
    (fa, fi), (ba, bi) = o_fwd, o_bwd
    m, hd = fa.shape[-2:]
    k, n = w_o.shape
    assert k == hd
    head_dim = norm_g.shape[0]
    tm = _largest_divisor(m, (512, 256, 128))
    tn = _largest_divisor(n, (512, 256, 128))

    def rows(idx):
        if idx is None:
            return pl.BlockSpec((tm, hd), lambda i, j: (i, 0))
        return pl.BlockSpec((None, tm, hd), lambda i, j: (idx, i, 0))

    kern = functools.partial(_gated_norm_proj_kernel, head_dim=head_dim, silu_gate=silu_gate)
    return pl.pallas_call(
        kern,
        out_shape=jax.ShapeDtypeStruct((m, n), BF16),
        grid=(m // tm, n // tn),
        in_specs=[rows(fi), rows(bi),
                  pl.BlockSpec((tm, hd), lambda i, j: (i, gate_block)),
                  pl.BlockSpec((1, head_dim), lambda i, j: (0, 0)),
                  pl.BlockSpec((k, tn), lambda i, j: (0, j))],
        out_specs=pl.BlockSpec((tm, tn), lambda i, j: (i, j)),
        scratch_shapes=[pltpu.VMEM((tm, hd), BF16)],
        compiler_params=_cparams("parallel", "arbitrary"),
        name="gated_norm_proj",
    )(fa, ba, zz, norm_g.reshape(1, head_dim).astype(F32), w_o.astype(BF16))


def _resid_norm_kernel(*refs, n_branch, weighted, modulated, routed):
    refs = list(refs)
    x = refs.pop(0)[...].astype(F32)
    ys = [refs.pop(0)[...].astype(F32) for _ in range(n_branch)]
    if weighted:
        w = refs.pop(0)[...]
        ys = [y * w[:, k:k + 1] for k, y in enumerate(ys)]
    if n_branch:
        x = x + refs.pop(0)[...] * functools.reduce(lambda p, q: p + q, ys)
    g = refs.pop(0)[...]
    out = x * lax.rsqrt(jnp.mean(x * x, axis=-1, keepdims=True) + EPS) * g
    if modulated:
        sc = refs.pop(0)[...]
        sh = refs.pop(0)[...]
        out = out * (1.0 + sc) + sh
    wr = refs.pop(0) if routed else None
    if n_branch:
        refs.pop(0)[...] = x
    o_ref = refs.pop(0)
    o_ref[...] = out.astype(o_ref.dtype)
    if routed:
        logits = jnp.dot(out, wr[...], preferred_element_type=F32, precision=lax.Precision.HIGHEST)
        refs.pop(0)[...] = jax.nn.sigmoid(logits)


def resid_norm(x, n_ctx, gain, *, branches=(), weights=None, gate=None, sc=None, sh=None, w_router=None,
               latent_only=False):
    b, t, d = x.shape
    tm = n_ctx
    assert t % tm == 0
    off = 1 if latent_only else 0
    rows = pl.BlockSpec((None, tm, d), lambda i, j: (i, j + off, 0))
    out_rows = pl.BlockSpec((None, tm, d), lambda i, j: (i, j, 0))
    seg = pl.BlockSpec((None, None, 1, d), lambda i, j: (i, jnp.minimum(j + off, 1), 0, 0))
    args, specs = [x], [rows]
    for y in branches:
        args.append(y)
        specs.append(rows)
    if weights is not None:
        args.append(weights)
        specs.append(pl.BlockSpec((None, tm, weights.shape[-1]), lambda i, j: (i, j + off, 0)))
    if branches:
        args.append(gate)
        specs.append(seg)
    args.append(gain.reshape(1, d).astype(F32))
    specs.append(pl.BlockSpec((1, d), lambda i, j: (0, 0)))
    modulated = sc is not None
    if modulated:
        args += [sc, sh]
        specs += [seg, seg]
    routed = w_router is not None
    if routed:
        args.append(jnp.pad(w_router.astype(F32), ((0, 0), (0, ROUTER_PAD - w_router.shape[1]))))
        specs.append(pl.BlockSpec((d, ROUTER_PAD), lambda i, j: (0, 0)))
    t_out = t - off * tm
    out_shapes, out_specs = [], []
    if branches:
        out_shapes.append(jax.ShapeDtypeStruct((b, t_out, d), F32))
        out_specs.append(out_rows)
    out_shapes.append(jax.ShapeDtypeStruct((b, t_out, d), BF16 if modulated else F32))
    out_specs.append(out_rows)
    if routed:
        out_shapes.append(jax.ShapeDtypeStruct((b, t_out, ROUTER_PAD), F32))
        out_specs.append(pl.BlockSpec((None, tm, ROUTER_PAD), lambda i, j: (i, j, 0)))
    kern = functools.partial(_resid_norm_kernel, n_branch=len(branches), weighted=weights is not None,
                             modulated=modulated, routed=routed)
    return pl.pallas_call(
        kern,
        out_shape=tuple(out_shapes),
        grid=(b, t_out // tm),
        in_specs=specs,
        out_specs=tuple(out_specs),
        compiler_params=_cparams("parallel", "parallel"),
        name="resid_norm",
    )(*args)


def _moe_kernel(te_ref, tv_ref, x_ref, wg_ref, wu_ref, wd_ref, o_ref, wg_s, wu_s, wd_s):
    i = pl.program_id(0)

    @pl.when(tv_ref[i] > 0)
    def _():
        @pl.when((i == 0) | (te_ref[i] != te_ref[jnp.maximum(i - 1, 0)]))
        def _():
            wg_s[...] = wg_ref[0].astype(BF16)
            wu_s[...] = wu_ref[0].astype(BF16)
            wd_s[...] = wd_ref[0].astype(BF16)

        x = x_ref[...]
        g = jnp.dot(x, wg_s[...], preferred_element_type=F32)
        u = jnp.dot(x, wu_s[...], preferred_element_type=F32)
        act = (g * jax.nn.sigmoid(g) * u).astype(BF16)
        o_ref[...] = jnp.dot(act, wd_s[...], preferred_element_type=F32).astype(o_ref.dtype)

    @pl.when(tv_ref[i] == 0)
    def _():
        o_ref[...] = jnp.zeros_like(o_ref)


def _route(aff, b_router):
    assert GROUP_SCORE_TOPK == 2 and TOP_K == 2
    epg = EXPERTS_PER_GROUP
    sel = aff + b_router.astype(F32)
    s = [sel[:, e] for e in range(N_EXPERTS)]
    a = [aff[:, e] for e in range(N_EXPERTS)]

    def first_max(vals):
        idx, best = jnp.zeros_like(vals[0], dtype=jnp.int32), vals[0]
        for e in range(1, len(vals)):
            upd = vals[e] > best
            idx, best = jnp.where(upd, e, idx), jnp.where(upd, vals[e], best)
        return idx, best

    def pick(vals, idx):
        out = vals[0]
        for e in range(1, len(vals)):
            out = jnp.where(idx == e, vals[e], out)
        return out

    def top2_sum(v):
        pairs = [v[i] + v[j] for i in range(len(v)) for j in range(i + 1, len(v))]
        return functools.reduce(jnp.maximum, pairs)

    g_best, _ = first_max([top2_sum(s[g * epg:(g + 1) * epg]) for g in range(N_GROUPS)])
    in_s = [pick([s[g * epg + e] for g in range(N_GROUPS)], g_best) for e in range(epg)]
    in_a = [pick([a[g * epg + e] for g in range(N_GROUPS)], g_best) for e in range(epg)]
    i1, _ = first_max(in_s)
    i2, _ = first_max([jnp.where(i1 == e, NEG_INF, in_s[e]) for e in range(epg)])
    w1, w2 = pick(in_a, i1), pick(in_a, i2)
    tot = w1 + w2
    expert_idx = jnp.stack([g_best * epg + i1, g_best * epg + i2], axis=-1)
    return expert_idx.astype(jnp.int32), jnp.stack([w1 / tot, w2 / tot], axis=-1)


def grouped_moe(h2, aff, b_router, w_gate, w_up, w_down):
    n, d = h2.shape
    expert_idx, w = _route(aff, b_router)
    n_slots = TOP_K * n
    e_flat = expert_idx.reshape(-1)
    onehot = (e_flat[:, None] == jnp.arange(N_EXPERTS, dtype=jnp.int32)[None, :]).astype(jnp.int32)
    csum = jnp.cumsum(onehot, axis=0)
    rank = jnp.take_along_axis(csum, e_flat[:, None], axis=1)[:, 0] - 1
    counts = csum[-1]
    padded = ((counts + MOE_TILE - 1) // MOE_TILE) * MOE_TILE
    pad_end = jnp.cumsum(padded)
    pad_off = pad_end - padded
    dest_flat = pad_off[e_flat] + rank
    dest = dest_flat.reshape(n, TOP_K)

    n_tiles = -(-n_slots // MOE_TILE) + N_EXPERTS
    n_pad_slots = n_tiles * MOE_TILE
    tile_start = jnp.arange(n_tiles, dtype=jnp.int32) * MOE_TILE
    tile_expert = jnp.minimum(jnp.sum((tile_start[:, None] >= pad_end[None, :]).astype(jnp.int32), axis=1),
                              N_EXPERTS - 1)
    tile_valid = (tile_start < pad_end[-1]).astype(jnp.int32)

    slot_token = jnp.zeros((n_pad_slots,), jnp.int32).at[dest_flat].set(
        jnp.arange(n_slots, dtype=jnp.int32) // TOP_K, unique_indices=True, mode="promise_in_bounds")

    xs = h2.at[slot_token].get(mode="promise_in_bounds")
    f = D_EXPERT
    ys = pl.pallas_call(
        _moe_kernel,
        out_shape=jax.ShapeDtypeStruct((n_pad_slots, d), BF16),
        grid_spec=pltpu.PrefetchScalarGridSpec(
            num_scalar_prefetch=2,
            grid=(n_tiles,),
            in_specs=[pl.BlockSpec((MOE_TILE, d), lambda i, te, tv: (i, 0)),
                      pl.BlockSpec((1, d, f), lambda i, te, tv: (te[i], 0, 0), pipeline_mode=pl.Buffered(1)),
                      pl.BlockSpec((1, d, f), lambda i, te, tv: (te[i], 0, 0), pipeline_mode=pl.Buffered(1)),
                      pl.BlockSpec((1, f, d), lambda i, te, tv: (te[i], 0, 0), pipeline_mode=pl.Buffered(1))],
            out_specs=pl.BlockSpec((MOE_TILE, d), lambda i, te, tv: (i, 0)),
            scratch_shapes=[pltpu.VMEM((d, f), BF16), pltpu.VMEM((d, f), BF16), pltpu.VMEM((f, d), BF16)],
        ),
        compiler_params=_cparams("arbitrary"),
        name="moe_experts",
    )(tile_expert, tile_valid, xs, w_gate.astype(F32), w_up.astype(F32), w_down.astype(F32))
    return (ys.at[dest[:, 0]].get(mode="promise_in_bounds"), ys.at[dest[:, 1]].get(mode="promise_in_bounds"), w)


DA_TQ = 512


def _diff_attn_kernel(lam_ref, q_ref, k_ref, v_ref, cos_ref, sin_ref, g_ref, o_ref, k_scr, v_scr,
                      *, n_ctx, lam_init):
    i = pl.program_id(2)
    t = k_ref.shape[0]
    hw = 2 * DA_HALF
    lam = lam_ref[0]
    lane = lax.broadcasted_iota(jnp.int32, (1, hw), 1)
    first_half = (lane % (DA_HALF // 2)) < DA_HALF // 4
    nt = (((1,), (1,)), ((), ()))

    def rope(x, lo, n):
        xf = x.astype(F32)
        partner = jnp.where(first_half, pltpu.roll(xf, hw - DA_HALF // 4, axis=1), pltpu.roll(xf, DA_HALF // 4, axis=1))
        return xf * cos_ref[pl.ds(lo, n), :] + partner * sin_ref[pl.ds(lo, n), :]

    @pl.when(i == 0)
    def _():
        def body(c, carry):
            lo = pl.multiple_of(c * n_ctx, n_ctx)
            k_scr[pl.ds(lo, n_ctx), :] = rope(k_ref[pl.ds(lo, n_ctx), :], lo, n_ctx).astype(BF16)
            return carry

        lax.fori_loop(0, t // n_ctx, body, 0)
        v_scr[:, :hw] = v_ref[...]
        v_scr[:, hw:] = jnp.ones((t, hw), BF16)

    def attend(lo, n, k, v_ext):
        q = rope(q_ref[pl.ds(lo, n), :], lo, n)

        def one_map(in_map):
            qm = jnp.where(in_map, q, 0.0).astype(BF16)
            s = lax.dot_general(qm, k, nt, preferred_element_type=F32)
            p = jnp.exp((s - jnp.max(s, axis=-1, keepdims=True)).astype(BF16))
            oe = jnp.dot(p, v_ext, preferred_element_type=F32)
            return oe[:, :hw] * (1.0 / oe[:, hw:hw + 1])

        o = one_map(lane < DA_HALF) - lam * one_map(lane >= DA_HALF)
        o = o * lax.rsqrt(jnp.mean(o * o, axis=-1, keepdims=True) + 1e-5)
        o_ref[pl.ds(lo, n), :] = ((o * g_ref[...]) * (1.0 - lam_init)).astype(o_ref.dtype)

    @pl.when(i == 0)
    def _():
        attend(0, n_ctx, k_scr[:n_ctx, :], v_scr[:n_ctx, :])

    @pl.when(i > 0)
    def _():
        attend(pl.multiple_of(n_ctx + (i - 1) * DA_TQ, n_ctx), DA_TQ, k_scr[...], v_scr[...])


def diff_attn_core(qkv, cos, sin, lam, subln_g, n_ctx, lam_init):
    b, t, _ = qkv.shape
    assert (t - n_ctx) % DA_TQ == 0 and DA_TQ % n_ctx == 0
    nh = DA_HEADS
    hw = 2 * DA_HALF
    kern = functools.partial(_diff_attn_kernel, n_ctx=n_ctx, lam_init=lam_init)
    table = pl.BlockSpec((t, hw), lambda bi, h, i: (0, 0))

    def col(off):
        return pl.BlockSpec((None, t, hw), lambda bi, h, i: (bi, 0, off + h))

    return pl.pallas_call(
        kern,
        out_shape=jax.ShapeDtypeStruct((b, t, nh * hw), BF16),
        grid=(b, nh, 1 + (t - n_ctx) // DA_TQ),
        in_specs=[pl.BlockSpec(memory_space=pltpu.SMEM), col(0), col(nh), col(2 * nh), table, table,
                  pl.BlockSpec((1, hw), lambda bi, h, i: (0, 0))],
        out_specs=col(0),
        scratch_shapes=[pltpu.VMEM((t, hw), BF16), pltpu.VMEM((t, 2 * hw), BF16)],
        compiler_params=_cparams("parallel", "parallel", "arbitrary"),
        name="diff_attn",
    )(lam.reshape(1).astype(F32), qkv, qkv, qkv, cos, sin, subln_g.reshape(1, hw).astype(F32))


NA_QROWS = 8
NA_KROWS = 2 * NA_QROWS
NA_DR_PAD = 16
NEG_INF = float("-inf")
NA_BIAS_ROWS = 2 * WIN_R - 1
NA_BIAS_COLS = 2 * WIN_C - 1


def _na_kernel(rpb_ref, q_ref, k_ref, v_ref, o_ref, bias_ref, *, n_ctx, rows):
    h = pl.program_id(0)
    b = pl.program_id(1)
    j = pl.program_id(2)
    w = GRID_W
    nq = NA_QROWS * w
    nk = NA_KROWS * w
    lane = lax.broadcasted_iota(jnp.int32, (w, 2 * w), 1)
    nt = (((1,), (1,)), ((), ()))

    @pl.when((b == 0) & (j == 0))
    def _():
        c = lax.broadcasted_iota(jnp.int32, (w, 2 * w), 0)
        kc = lane % w
        c0 = jnp.clip(c - WIN_C // 2, 0, w - WIN_C)
        in_win = (kc >= c0) & (kc < c0 + WIN_C)
        single = []
        for dr in range(-(WIN_R - 1), WIN_R):
            t = jnp.full((w, 2 * w), NEG_INF, F32)
            for dc in range(-(WIN_C - 1), WIN_C):
                t = jnp.where(kc - c == dc, rpb_ref[(h * NA_BIAS_ROWS + dr + WIN_R - 1) * NA_BIAS_COLS + dc + WIN_C - 1], t)
            single.append(jnp.where(in_win, t, NEG_INF))
        neg = jnp.full((w, 2 * w), NEG_INF, F32)

        def at(dr):
            return single[dr + WIN_R - 1] if abs(dr) < WIN_R else neg

        for d in range(2 * NA_DR_PAD + 1):
            bias_ref[d] = jnp.where(lane < w, at(d - NA_DR_PAD), at(d - NA_DR_PAD + 1))

    kctx = k_ref[0, :n_ctx, :]
    vctx = v_ref[0, :n_ctx, :]

    def finish(parts, q_lo, n):
        m = parts[0][0].max(axis=-1, keepdims=True)
        for s, _ in parts[1:]:
            m = jnp.maximum(m, s.max(axis=-1, keepdims=True))
        l = 0.0
        o = 0.0
        for s, vv in parts:
            p = jnp.exp(s - m)
            l = l + p.sum(axis=-1, keepdims=True)
            o = o + jnp.dot(p.astype(BF16), vv, preferred_element_type=F32)
        o_ref[0, pl.ds(q_lo, n), :] = (o * (1.0 / l)).astype(o_ref.dtype)

    @pl.when(j < rows // NA_QROWS)
    def _():
        r_lo = j * NA_QROWS
        ks = jnp.clip(r_lo - WIN_R // 2, 0, rows - NA_KROWS)
        q_lo = pl.multiple_of(n_ctx + r_lo * w, w)
        k_lo = pl.multiple_of(n_ctx + ks * w, w)
        q = q_ref[0, pl.ds(q_lo, nq), :]
        kwin = k_ref[0, pl.ds(k_lo, nk), :]
        vwin = v_ref[0, pl.ds(k_lo, nk), :]
        s_win = lax.dot_general(q, kwin, nt, preferred_element_type=F32)
        half = (lane >= w).astype(jnp.int32)
        row_blocks = []
        for i in range(NA_QROWS):
            r = r_lo + i
            r0 = jnp.clip(r - WIN_R // 2, 0, rows - WIN_R)
            tiles = []
            for jj in range(NA_KROWS // 2):
                kr = ks + 2 * jj
                off = lax.bitcast_convert_type(half + (kr - r0), jnp.uint32)
                t = s_win[i * w:(i + 1) * w, 2 * jj * w:(2 * jj + 2) * w] + bias_ref[kr - r + NA_DR_PAD]
                tiles.append(jnp.where(off < WIN_R, t, NEG_INF))
            row_blocks.append(jnp.concatenate(tiles, axis=1))
        s_win = jnp.concatenate(row_blocks, axis=0)
        s_ctx = lax.dot_general(q, kctx, nt, preferred_element_type=F32)
        finish([(s_win, vwin), (s_ctx, vctx)], q_lo, nq)

    @pl.when(j == rows // NA_QROWS)
    def _():
        q = q_ref[0, :n_ctx, :]
        finish([(lax.dot_general(q, kctx, nt, preferred_element_type=F32), vctx)], 0, n_ctx)


def na_attn_core(qkv, rpb, n_ctx):
    b, t, _ = qkv.shape
    rows = (t - n_ctx) // GRID_W
    assert rows % NA_QROWS == 0 and rows >= NA_KROWS
    nh, hd = NA_HEADS, NA_DIM
    kern = functools.partial(_na_kernel, n_ctx=n_ctx, rows=rows)

    def col(off):
        return pl.BlockSpec((1, t, hd), lambda h, bi, j: (bi, 0, off + h))

    return pl.pallas_call(
        kern,
        out_shape=jax.ShapeDtypeStruct((b, t, nh * hd), BF16),
        grid=(nh, b, rows // NA_QROWS + 1),
        in_specs=[pl.BlockSpec(memory_space=pltpu.SMEM), col(0), col(nh), col(2 * nh)],
        out_specs=col(0),
        scratch_shapes=[pltpu.VMEM((2 * NA_DR_PAD + 1, GRID_W, 2 * GRID_W), F32)],
        compiler_params=_cparams("arbitrary", "arbitrary", "arbitrary"),
        name="na_attn",
    )(rpb.astype(F32).reshape(-1), qkv, qkv, qkv)


def proj(t, w, out_dtype=F32):
    b, tt, k = t.shape
    return mm(t.reshape(b * tt, k), w, out_dtype).reshape(b, tt, w.shape[1])


def axial_rope_tables(length, dim):
    n_freq = dim // 4
    t = jnp.arange(length)
    pos = jnp.stack([t // GRID_W, t % GRID_W], axis=-1).astype(F32)
    inv = ROPE_BASE ** (-jnp.arange(n_freq, dtype=F32) / n_freq)
    ang = pos[:, :, None] * inv
    return jnp.cos(ang), jnp.sin(ang)


def diff_attention(h, n_ctx, w_qkv, w_o, lam_vec, subln_g, layer_idx):
    t = h.shape[1]
    dq = 2 * DA_HEADS * DA_HALF
    lam_init = 0.8 - 0.6 * math.exp(-0.3 * layer_idx)
    lv = lam_vec.astype(F32)
    lam = jnp.exp(jnp.sum(lv[0] * lv[1])) - jnp.exp(jnp.sum(lv[2] * lv[3])) + lam_init
    w = jnp.concatenate([w_qkv[:, :dq] * DA_HALF ** -0.5, w_qkv[:, dq:]], axis=1)
    cos, sin = axial_rope_tables(t - n_ctx, DA_HALF)
    n_freq = cos.shape[-1]
    sign = jnp.array([-1.0, 1.0], F32)[None, None, :, None]
    cos_t = jnp.broadcast_to(cos[:, :, None, :], (t - n_ctx, 2, 2, n_freq)).reshape(t - n_ctx, DA_HALF)
    sin_t = (sin[:, :, None, :] * sign).reshape(t - n_ctx, DA_HALF)
    cos_t = jnp.concatenate([jnp.ones((n_ctx, 2 * DA_HALF), F32), jnp.tile(cos_t, (1, 2))], axis=0)
    sin_t = jnp.concatenate([jnp.zeros((n_ctx, 2 * DA_HALF), F32), jnp.tile(sin_t, (1, 2))], axis=0)
    o = diff_attn_core(proj(h, w, BF16), cos_t, sin_t, lam, subln_g, n_ctx, lam_init)
    return proj(o, w_o, BF16)


GDN_PREP_ROWS = 128
GDN_HALO = 8


def _gdn_prep_kernel(cur_ref, prev_ref, next_ref, w_ref, q_ref, k_ref, v_ref, *, first_latent_tile):
    j = pl.program_id(1)
    nt = pl.num_programs(1)
    tm = GDN_PREP_ROWS
    taps = w_ref.shape[0]
    left_ok = (j != 0) & (j != first_latent_tile)
    right_ok = (j != first_latent_tile - 1) & (j != nt - 1)
    n_ext = tm + 2 * GDN_HALO
    for grp, out_ref in enumerate((q_ref, k_ref, v_ref)):
        cols = slice(grp * out_ref.shape[1], (grp + 1) * out_ref.shape[1])
        prev = jnp.where(left_ok, prev_ref[:, cols], 0.0)
        nxt = jnp.where(right_ok, next_ref[:, cols], 0.0)
        ext = jnp.concatenate([prev, cur_ref[:, cols], nxt], axis=0)
        conv = None
        for tap in range(taps):
            shift = (taps // 2 - tap) % n_ext
            x = (pltpu.roll(ext, shift, axis=0) if shift else ext)[GDN_HALO:GDN_HALO + tm]
            term = x * w_ref[tap:tap + 1, cols]
            conv = term if conv is None else conv + term
        z = conv * jax.nn.sigmoid(conv)
        if out_ref is v_ref:
            out_ref[...] = z.astype(out_ref.dtype)
            continue
        scale = GDN_DK ** -0.5 if out_ref is q_ref else 1.0
        for h in range(GDN_HEADS):
            sl = slice(h * GDN_DK, (h + 1) * GDN_DK)
            zh = z[:, sl]
            out_ref[:, sl] = (zh * (lax.rsqrt(jnp.sum(zh * zh, axis=-1, keepdims=True) + EPS) * scale)
                              ).astype(out_ref.dtype)


def gdn_prep(zz, conv_w, n_ctx):
    b, t, _ = zz.shape
    tm, halo = GDN_PREP_ROWS, GDN_HALO
    hd = GDN_HEADS * GDN_DK
    wid = 3 * hd
    assert n_ctx % tm == 0 and t % tm == 0 and conv_w.shape[0] // 2 <= halo
    per = tm // halo
    out = pl.BlockSpec((None, tm, hd), lambda i, j: (i, j, 0))
    kern = functools.partial(_gdn_prep_kernel, first_latent_tile=n_ctx // tm)
    return pl.pallas_call(
        kern,
        out_shape=tuple(jax.ShapeDtypeStruct((b, t, hd), BF16) for _ in range(3)),
        grid=(b, t // tm),
        in_specs=[pl.BlockSpec((None, tm, wid), lambda i, j: (i, j, 0)),
                  pl.BlockSpec((None, halo, wid), lambda i, j: (i, jnp.maximum(j * per - 1, 0), 0)),
                  pl.BlockSpec((None, halo, wid), lambda i, j: (i, jnp.minimum((j + 1) * per, t // halo - 1), 0)),
                  pl.BlockSpec((conv_w.shape[0], wid), lambda i, j: (0, 0))],
        out_specs=(out, out, out),
        compiler_params=_cparams("parallel", "parallel"),
        name="gdn_prep",
    )(zz, zz, zz, conv_w.astype(F32))


GDN_HEAD_GROUP = 16
GDN_SUB = 16


def _bdot(a, b):
    return jnp.dot(a.astype(BF16), b.astype(BF16), preferred_element_type=F32)


def _scan_chunk(d, s, n, n_ctx_chunks):
    back = jnp.where(s < n_ctx_chunks, n_ctx_chunks - 1 - s, n + n_ctx_chunks - 1 - s)
    return jnp.where(d == 0, s, back)


def _gdn_kernel(qs_ref, k_ref, v_ref, gc_ref, bc_ref, gr_ref, o_ref, s_ref):
    d = pl.program_id(0)
    s = pl.program_id(3)
    c = GDN_CHUNK

    @pl.when(s == 0)
    def _():
        s_ref[...] = jnp.zeros_like(s_ref)

    row = lax.broadcasted_iota(jnp.int32, (c, c), 0)
    col = lax.broadcasted_iota(jnp.int32, (c, c), 1)
    ahead = (row - col) * (1 - 2 * d)
    strict = ahead > 0
    incl = ahead >= 0
    same_blk = (row // GDN_SUB) == (col // GDN_SUB)
    eye = (row == col).astype(F32)
    nt = (((1,), (1,)), ((), ()))
    tn = (((0,), (0,)), ((), ()))
    heads = range(GDN_HEAD_GROUP)
    sls = [slice(g * GDN_DK, (g + 1) * GDN_DK) for g in heads]

    def each(fn, *lists):
        return [fn(*vals) for vals in zip(*lists)]

    gcol = gc_ref[...]
    g_last = jnp.min(gcol, axis=0, keepdims=True)
    e_g, e_rest, e_all = jnp.exp(gcol), jnp.exp(g_last - gcol), jnp.exp(g_last)
    beta = bc_ref[...]
    ks = [k_ref[:, sl] for sl in sls]
    kf = [k.astype(F32) for k in ks]
    kb = [k * beta[:, g:g + 1] for g, k in zip(heads, kf)]
    kbg = [(k * e_g[:, g:g + 1]).astype(BF16) for g, k in zip(heads, kb)]
    kd = [(k * e_rest[:, g:g + 1]).astype(BF16) for g, k in zip(heads, kf)]
    vb = [(v_ref[:, sl].astype(F32) * beta[:, g:g + 1]).astype(BF16) for g, sl in zip(heads, sls)]
    qg = [(qs_ref[:, sl].astype(F32) * e_g[:, g:g + 1]).astype(BF16) for g, sl in zip(heads, sls)]
    decay = [jnp.exp(jnp.where(incl, gcol[:, g:g + 1] - gr_ref[g:g + 1, :], NEG_INF)) for g in heads]
    a = each(lambda m, k, dc: jnp.where(
        strict, lax.dot_general(m.astype(BF16), k, nt, preferred_element_type=F32) * dc, 0.0), kb, ks, decay)
    qk = each(lambda sl, k, dc: jnp.where(
        incl, lax.dot_general(qs_ref[:, sl], k, nt, preferred_element_type=F32) * dc, 0.0), sls, ks, decay)
    dblk = each(lambda m: jnp.where(same_blk, m, 0.0), a)
    d2 = each(lambda m: _bdot(m, m), dblk)
    x = each(lambda m, m2: _bdot(eye - m, eye + m2), dblk, d2)
    d4 = each(lambda m: _bdot(m, m), d2)
    x = each(lambda m, m4: _bdot(m, eye + m4), x, d4)
    d8 = each(lambda m: _bdot(m, m), d4)
    x = each(lambda m, m8: _bdot(m, eye + m8), x, d8)
    nmat = each(lambda m, am, dm: _bdot(m, am - dm), x, a, dblk)
    n2 = each(lambda m: _bdot(m, m), nmat)
    y = each(lambda m, m2: _bdot(eye - m, eye + m2), nmat, n2)
    t = each(_bdot, y, x)
    uw = each(lambda m, p, q: _bdot(m, jnp.concatenate([p, q], axis=1)), t, vb, kbg)
    state = [s_ref[g] for g in heads]
    v_new = each(lambda m, st: m[:, :GDN_DV] - _bdot(m[:, GDN_DV:], st), uw, state)
    o_state = each(_bdot, qg, state)
    o_new = each(_bdot, qk, v_new)
    s_new = each(lambda m, vn: lax.dot_general(m, vn.astype(BF16), tn, preferred_element_type=F32), kd, v_new)
    for g in heads:
        s_ref[g] = state[g] * e_all[:, g:g + 1] + s_new[g]
    o_ref[...] = jnp.concatenate(each(lambda p, q: p + q, o_state, o_new), axis=1)


def gated_delta_core(qs, k, v, g, beta, n_ctx_chunks):
    b, t, hd = k.shape
    c, hh, gg = GDN_CHUNK, GDN_HEADS, GDN_HEAD_GROUP
    n = t // c
    gw = gg * GDN_DK

    def cols(a):
        return a.reshape(2, b, t, hh // gg, gg).transpose(0, 1, 3, 2, 4)

    g_row = g.reshape(2, b, n, c, hh // gg, gg).transpose(0, 1, 2, 4, 5, 3)

    def chunk_of(d, s):
        return _scan_chunk(d, s, n, n_ctx_chunks)

    shared = pl.BlockSpec((None, c, gw), lambda d, bi, hg, s: (bi, chunk_of(d, s), hg))
    col_spec = pl.BlockSpec((None, None, None, c, gg), lambda d, bi, hg, s: (d, bi, hg, chunk_of(d, s), 0))
    return pl.pallas_call(
        _gdn_kernel,
        out_shape=jax.ShapeDtypeStruct((2, b, t, hd), F32),
        grid=(2, b, hh // gg, n),
        in_specs=[shared, shared, shared, col_spec, col_spec,
                  pl.BlockSpec((None, None, None, None, gg, c), lambda d, bi, hg, s: (d, bi, chunk_of(d, s), hg, 0, 0))],
        out_specs=pl.BlockSpec((None, None, c, gw), lambda d, bi, hg, s: (d, bi, chunk_of(d, s), hg)),
        scratch_shapes=[pltpu.VMEM((gg, GDN_DK, GDN_DV), F32)],
        compiler_params=_cparams("parallel", "parallel", "parallel", "arbitrary"),
        name="gated_delta",
    )(qs, k, v, cols(g), cols(beta), g_row)


def gated_deltanet(h, n_ctx, w_in, conv_w, w_ab, dt_bias, a_log, norm_g, w_o):
    B, T, _ = h.shape
    tc = n_ctx
    H, C = GDN_HEADS, GDN_CHUNK
    wq = H * GDN_DK
    hi = 2 * wq + H * GDN_DV

    zz = proj(h, w_in)
    q, k, v = gdn_prep(zz, conv_w, tc)
    ab = proj(h, w_ab).astype(F32).reshape(B, T, 2, 2, H)
    la = -jnp.exp(a_log.astype(F32)) * jax.nn.softplus(ab[:, :, 0] + dt_bias.astype(F32))
    be = jax.nn.sigmoid(ab[:, :, 1])
    n = T // C
    la_c = la.reshape(B, n, C, 2, H)
    g_f = jnp.cumsum(la_c[:, :, :, 0], axis=2)
    g_b = jnp.flip(jnp.cumsum(jnp.flip(la_c[:, :, :, 1], axis=2), axis=2), axis=2)
    g = jnp.stack([g_f, g_b]).reshape(2, B, T, H)
    o = gated_delta_core(q, k, v, g, jnp.moveaxis(be, 2, 0), tc // C)
    o = o.reshape(2, B * T, -1)
    y = gated_norm_proj((o, 0), (o, 1), zz.reshape(B * T, -1), hi // (H * GDN_DV), norm_g, w_o, silu_gate=True)
    return y.reshape(B, T, -1)


def neighbourhood_attention(h, n_ctx, w_qkv, rpb, w_o):
    hd = NA_HEADS * NA_DIM
    w = jnp.concatenate([w_qkv[:, :hd] * NA_DIM ** -0.5, w_qkv[:, hd:]], axis=1)
    return proj(na_attn_core(proj(h, w, BF16), rpb, n_ctx), w_o, BF16)


GLA_CHUNK = 64
GLA_LEVELS = (1, 2, 4, 8, 16, 32)


def _gla_kernel(zq_ref, zv_ref, zf_ref, lb_ref, bf_ref, o_ref, st_ref, *, reverse):
    s = pl.program_id(1)
    c = GLA_CHUNK

    @pl.when(s == 0)
    def _():
        st_ref[...] = jnp.zeros_like(st_ref)

    row = lax.broadcasted_iota(jnp.int32, (c, c), 0)
    col = lax.broadcasted_iota(jnp.int32, (c, c), 1)
    late, early = (col, row) if reverse else (row, col)
    masks = []
    for m in GLA_LEVELS:
        masks.append(((row // (2 * m)) == (col // (2 * m))) & ((late % (2 * m)) >= m) & ((early % (2 * m)) < m))
    diag = row == col
    trow = lax.broadcasted_iota(jnp.int32, (c, HG_DK), 0)
    nt = (((1,), (1,)), ((), ()))
    tn = (((0,), (0,)), ((), ()))
    heads = range(HG_HEADS)
    sls = [slice(g * HG_DK, (g + 1) * HG_DK) for g in heads]

    def ref_rows(b, m):
        p = m if reverse else m - 1
        if 2 * m >= 8:
            blocks = b.reshape(c // (2 * m), 2 * m, HG_DK)
            return jnp.broadcast_to(blocks[:, p:p + 1, :], blocks.shape).reshape(c, HG_DK)
        out = b
        for rho in range(2 * m):
            if rho != p:
                out = jnp.where(trow % (2 * m) == rho, pltpu.roll(b, (rho - p) % c, axis=0), out)
        return out

    zq = zq_ref[...]
    q_all = zq * jax.nn.sigmoid(zq) * HG_DK ** -0.5
    v_all = zv_ref[...].astype(BF16)
    z = zf_ref[...] + bf_ref[...]
    t = jnp.exp(-jnp.abs(z))
    r = 1.0 / (1.0 + t)
    pos = z >= 0
    lb = lb_ref[...]
    log_f = jnp.log(lb + (1.0 - lb) * jnp.where(pos, r, t * r))
    k_all = (1.0 - lb) * jnp.where(pos, t * r, r)
    scan = (col >= row) if reverse else (col <= row)
    b_all = jnp.dot(scan.astype(F32), log_f, preferred_element_type=F32, precision=lax.Precision.HIGHEST)
    qs = [q_all[:, sl] for sl in sls]
    ks = [k_all[:, sl] for sl in sls]
    bs = [b_all[:, sl] for sl in sls]
    vs = [v_all[:, sl] for sl in sls]
    att = [jnp.where(diag, lax.dot_general(q.astype(BF16), k.astype(BF16), nt, preferred_element_type=F32), 0.0)
           for q, k in zip(qs, ks)]
    for m, mask in zip(GLA_LEVELS, masks):
        es = [jnp.exp(-jnp.abs(b - ref_rows(b, m))) for b in bs]
        sc = [lax.dot_general((q * e).astype(BF16), (k * e).astype(BF16), nt, preferred_element_type=F32)
              for q, k, e in zip(qs, ks, es)]
        att = [jnp.where(mask, x, a) for x, a in zip(sc, att)]
    last = c - 1 if not reverse else 0
    b_last = [b[last:last + 1, :] for b in bs]
    state = [st_ref[g] for g in heads]
    o_state = [lax.dot_general((q * jnp.exp(b)).astype(BF16), st.astype(BF16), nt, preferred_element_type=F32)
               for q, b, st in zip(qs, bs, state)]
    o_new = [jnp.dot(a.astype(BF16), v, preferred_element_type=F32) for a, v in zip(att, vs)]
    s_new = [lax.dot_general(v, (k * jnp.exp(bl - b)).astype(BF16), tn, preferred_element_type=F32)
             for v, k, b, bl in zip(vs, ks, bs, b_last)]
    for g in heads:
        st_ref[g] = state[g] * jnp.exp(b_last[g]) + s_new[g]
    o_ref[...] = jnp.concatenate([x + y for x, y in zip(o_state, o_new)], axis=1)


def gla_core(zz, lb, b_f, n_ctx_chunks, reverse):
    bsz, t, _ = zz.shape
    hd = HG_HEADS * HG_DK
    c = GLA_CHUNK
    n = t // c

    def chunk_of(s):
        return _scan_chunk(1, s, n, n_ctx_chunks) if reverse else s

    def cols(j):
        return pl.BlockSpec((None, c, hd), lambda bi, s: (bi, chunk_of(s), j))

    vec = pl.BlockSpec((1, hd), lambda bi, s: (0, 0))
    return pl.pallas_call(
        functools.partial(_gla_kernel, reverse=reverse),
        out_shape=jax.ShapeDtypeStruct((bsz, t, hd), F32),
        grid=(bsz, n),
        in_specs=[cols(0), cols(1), cols(3 if reverse else 2), vec, vec],
        out_specs=cols(0),
        scratch_shapes=[pltpu.VMEM((HG_HEADS, HG_DV, HG_DK), F32)],
        compiler_params=_cparams("parallel", "arbitrary"),
        name="gla_bwd" if reverse else "gla_fwd",
    )(zz, zz, zz, lb.reshape(1, hd).astype(F32), b_f.reshape(1, hd).astype(F32))


def hgrn2(h, n_ctx, w_q, w_i, w_f, b_f, w_g, norm_g, w_o, lb):
    B, T, _ = h.shape
    C = GLA_CHUNK
    hw = HG_HEADS * HG_DK

    zz = proj(h, jnp.concatenate([w_q, w_i, w_f[0], w_f[1], w_g], axis=1))
    o_f = gla_core(zz, lb, b_f[0], n_ctx // C, False).reshape(B * T, hw)
    o_b = gla_core(zz, lb, b_f[1], n_ctx // C, True).reshape(B * T, hw)
    y = gated_norm_proj((o_f, None), (o_b, None), zz.reshape(B * T, -1), 4, norm_g, w_o, silu_gate=False)
    return y.reshape(B, T, -1)


def kernel(x, c, ctx, c_ctx, w_mod, b_mod, norm_mix_g, norm_ffn_g, da_w_qkv, da_w_o, da_lam, da_subln_g, gdn_w_in, gdn_conv, gdn_w_ab, gdn_dt_bias, gdn_a_log, gdn_norm_g, gdn_w_o, na_w_qkv, na_rpb, na_w_o, hg_w_q, hg_w_i, hg_w_f, hg_b_f, hg_w_g, hg_norm_g, hg_w_o, hg_lb_logits, w_router, b_router, e_w_gate, e_w_up, e_w_down, final_norm_g):
    B, L, D = x.shape
    tc = ctx.shape[1]
    T = tc + L
    xa = jnp.concatenate([ctx, x], axis=1)
    p_lb = jax.nn.softmax(hg_lb_logits.astype(F32), axis=0)
    lb_all = jnp.cumsum(p_lb, axis=0) - p_lb[0]
    cond_all = jnp.concatenate([jax.nn.silu(c), jax.nn.silu(c_ctx)[None]], axis=0)
    cond_all = jnp.pad(cond_all, ((0, -(B + 1) % LANES), (0, 0)))

    def mods(i):
        mod = mm(cond_all, w_mod[i])[:B + 1] + b_mod[i]
        both = jnp.stack([jnp.broadcast_to(mod[B], (B, 6 * D)), mod[:B]], axis=1)
        return [both[:, :, None, k * D:(k + 1) * D] for k in range(6)]

    sh1, sc1, g1, sh2, sc2, g2 = mods(0)
    (h,) = resid_norm(xa, tc, norm_mix_g[0], sc=sc1, sh=sh1)
    for i in range(DEPTH):
        kind, j = i % N_MIXERS, i // N_MIXERS
        if kind == 0:
            y = diff_attention(h, tc, da_w_qkv[j], da_w_o[j], da_lam[j], da_subln_g[j], i)
        elif kind == 1:
            y = gated_deltanet(h, tc, gdn_w_in[j], gdn_conv[j], gdn_w_ab[j], gdn_dt_bias[j],
                               gdn_a_log[j], gdn_norm_g[j], gdn_w_o[j])
        elif kind == 2:
            y = neighbourhood_attention(h, tc, na_w_qkv[j], na_rpb[j], na_w_o[j])
        else:
            y = hgrn2(h, tc, hg_w_q[j], hg_w_i[j], hg_w_f[j], hg_b_f[j], hg_w_g[j],
                      hg_norm_g[j], hg_w_o[j], lb_all[i])
        xa, h2, aff = resid_norm(xa, tc, norm_ffn_g[i], branches=(y,), gate=g1, sc=sc2, sh=sh2, w_router=w_router)
        ya, yb, w = grouped_moe(h2.reshape(B * T, D), aff.reshape(B * T, ROUTER_PAD)[:, :N_EXPERTS],
                                b_router, e_w_gate[i], e_w_up[i], e_w_down[i])
        branches = (ya.reshape(B, T, D), yb.reshape(B, T, D))
        w = w.reshape(B, T, TOP_K)
        if i == DEPTH - 1:
            return resid_norm(xa, tc, final_norm_g, branches=branches, weights=w, gate=g2, latent_only=True)[1]
        gate_ffn = g2
        sh1, sc1, g1, sh2, sc2, g2 = mods(i + 1)
        xa, h = resid_norm(xa, tc, norm_mix_g[i + 1], branches=branches, weights=w, gate=gate_ffn, sc=sc1, sh=sh1)
```

```python
import functools
import math

import jax
import jax.numpy as jnp
from jax import lax
from jax.experimental import pallas as pl
from jax.experimental.pallas import tpu as pltpu

F32 = jnp.float32
BF16 = jnp.bfloat16

D_MODEL = 2048
DEPTH = 4
GRID_W = 64
N_MIXERS = 4
EPS = 1e-6
DA_HEADS = 16
DA_HALF = D_MODEL // DA_HEADS // 2
DA_VDIM = 2 * DA_HALF
ROPE_BASE = 10000.0
GDN_HEADS = 16
GDN_DK = D_MODEL // GDN_HEADS
GDN_DV = D_MODEL // GDN_HEADS
GDN_CHUNK = 64
NA_HEADS = 16
NA_DIM = D_MODEL // NA_HEADS
WIN_R = 8
WIN_C = 16
HG_HEADS = 16
HG_DK = D_MODEL // HG_HEADS
HG_DV = D_MODEL // HG_HEADS
HG_CHUNK = 32
N_EXPERTS = 16
N_GROUPS = 4
EXPERTS_PER_GROUP = N_EXPERTS // N_GROUPS
GROUP_SCORE_TOPK = 2
TOP_K = 2
D_EXPERT = D_MODEL // 2

V7X_VMEM_LIMIT_BYTES = 56 * 1024 * 1024
LANES = 128
MOE_TILE = 512
ROUTER_PAD = LANES


def _largest_divisor(n, candidates):
    for c in candidates:
        if n % c == 0:
            return c
    raise ValueError(f"no tile in {candidates} divides {n}")


def _cparams(*sem):
    return pltpu.CompilerParams(dimension_semantics=sem, vmem_limit_bytes=V7X_VMEM_LIMIT_BYTES)


def _mm_kernel(a_ref, w_ref, o_ref):
    o_ref[...] = jnp.dot(a_ref[...], w_ref[...], preferred_element_type=F32).astype(o_ref.dtype)


def mm(a, w, out_dtype=F32):
    m, k = a.shape
    n = w.shape[1]
    n_pad = -n % LANES
    if n_pad:
        w = jnp.pad(w, ((0, 0), (0, n_pad)))
    a = a.astype(BF16)
    w = w.astype(BF16)
    np_ = n + n_pad
    tm = _largest_divisor(m, (1024, 512, 256, 128))
    tn = _largest_divisor(np_, (512, 256, 128))
    out = pl.pallas_call(
        _mm_kernel,
        out_shape=jax.ShapeDtypeStruct((m, np_), out_dtype),
        grid=(m // tm, np_ // tn),
        in_specs=[pl.BlockSpec((tm, k), lambda i, j: (i, 0)),
                  pl.BlockSpec((k, tn), lambda i, j: (0, j))],
        out_specs=pl.BlockSpec((tm, tn), lambda i, j: (i, j)),
        compiler_params=_cparams("parallel", "arbitrary"),
        name="proj_mm",
    )(a, w)
    return out[:, :n] if n_pad else out


def _gated_norm_proj_kernel(of_ref, ob_ref, zg_ref, g_ref, w_ref, o_ref, a_scr, *, head_dim, silu_gate):
    @pl.when(pl.program_id(1) == 0)
    def _():
        o = of_ref[...] + ob_ref[...]
        zg = zg_ref[...].astype(F32)
        sig = jax.nn.sigmoid(zg)
        gate = zg * sig if silu_gate else sig
        for h in range(o.shape[1] // head_dim):
            sl = slice(h * head_dim, (h + 1) * head_dim)
            oh = o[:, sl]
            oh = oh * lax.rsqrt(jnp.mean(oh * oh, axis=-1, keepdims=True) + EPS) * g_ref[...]
            a_scr[:, sl] = (oh * gate[:, sl]).astype(BF16)

    o_ref[...] = jnp.dot(a_scr[...], w_ref[...], preferred_element_type=F32).astype(o_ref.dtype)


def gated_norm_proj(o_fwd, o_bwd, zz, gate_block, norm_g, w_o, *, silu_gate):
    (fa, fi), (ba, bi) = o_fwd, o_bwd
    m, hd = fa.shape[-2:]
    k, n = w_o.shape
    assert k == hd
    head_dim = norm_g.shape[0]
    tm = _largest_divisor(m, (512, 256, 128))
    tn = _largest_divisor(n, (512, 256, 128))

    def rows(idx):
        if idx is None:
            return pl.BlockSpec((tm, hd), lambda i, j: (i, 0))
        return pl.BlockSpec((None, tm, hd), lambda i, j: (idx, i, 0))

    kern = functools.partial(_gated_norm_proj_kernel, head_dim=head_dim, silu_gate=silu_gate)
    return pl.pallas_call(
        kern,
        out_shape=jax.ShapeDtypeStruct((m, n), BF16),
        grid=(m // tm, n // tn),
        in_specs=[rows(fi), rows(bi),
                  pl.BlockSpec((tm, hd), lambda i, j: (i, gate_block)),
                  pl.BlockSpec((1, head_dim), lambda i, j: (0, 0)),
                  pl.BlockSpec((k, tn), lambda i, j: (0, j))],
        out_specs=pl.BlockSpec((tm, tn), lambda i, j: (i, j)),
        scratch_shapes=[pltpu.VMEM((tm, hd), BF16)],
        compiler_params=_cparams("parallel", "arbitrary"),
        name="gated_norm_proj",
    )(fa, ba, zz, norm_g.reshape(1, head_dim).astype(F32), w_o.astype(BF16))


def _resid_norm_kernel(*refs, n_branch, weighted, modulated, routed):
    refs = list(refs)
    x = refs.pop(0)[...].astype(F32)
    ys = [refs.pop(0)[...].astype(F32) for _ in range(n_branch)]
    if weighted:
        w = refs.pop(0)[...]
        ys = [y * w[:, k:k + 1] for k, y in enumerate(ys)]
    if n_branch:
        x = x + refs.pop(0)[...] * functools.reduce(lambda p, q: p + q, ys)
    g = refs.pop(0)[...]
    out = x * lax.rsqrt(jnp.mean(x * x, axis=-1, keepdims=True) + EPS) * g
    if modulated:
        sc = refs.pop(0)[...]
        sh = refs.pop(0)[...]
        out = out * (1.0 + sc) + sh
    wr = refs.pop(0) if routed else None
    if n_branch:
        refs.pop(0)[...] = x
    o_ref = refs.pop(0)
    o_ref[...] = out.astype(o_ref.dtype)
    if routed:
        logits = jnp.dot(out, wr[...], preferred_element_type=F32, precision=lax.Precision.HIGHEST)
        refs.pop(0)[...] = jax.nn.sigmoid(logits)


def resid_norm(x, n_ctx, gain, *, branches=(), weights=None, gate=None, sc=None, sh=None, w_router=None,
               latent_only=False):
    b, t, d = x.shape
    tm = n_ctx
    assert t % tm == 0
    off = 1 if latent_only else 0
    rows = pl.BlockSpec((None, tm, d), lambda i, j: (i, j + off, 0))
    out_rows = pl.BlockSpec((None, tm, d), lambda i, j: (i, j, 0))
    seg = pl.BlockSpec((None, None, 1, d), lambda i, j: (i, jnp.minimum(j + off, 1), 0, 0))
    args, specs = [x], [rows]
    for y in branches:
        args.append(y)
        specs.append(rows)
    if weights is not None:
        args.append(weights)
        specs.append(pl.BlockSpec((None, tm, weights.shape[-1]), lambda i, j: (i, j + off, 0)))
    if branches:
        args.append(gate)
        specs.append(seg)
    args.append(gain.reshape(1, d).astype(F32))
    specs.append(pl.BlockSpec((1, d), lambda i, j: (0, 0)))
    modulated = sc is not None
    if modulated:
        args += [sc, sh]
        specs += [seg, seg]
    routed = w_router is not None
    if routed:
        args.append(jnp.pad(w_router.astype(F32), ((0, 0), (0, ROUTER_PAD - w_router.shape[1]))))
        specs.append(pl.BlockSpec((d, ROUTER_PAD), lambda i, j: (0, 0)))
    t_out = t - off * tm
    out_shapes, out_specs = [], []
    if branches:
        out_shapes.append(jax.ShapeDtypeStruct((b, t_out, d), F32))
        out_specs.append(out_rows)
    out_shapes.append(jax.ShapeDtypeStruct((b, t_out, d), BF16 if modulated else F32))
    out_specs.append(out_rows)
    if routed:
        out_shapes.append(jax.ShapeDtypeStruct((b, t_out, ROUTER_PAD), F32))
        out_specs.append(pl.BlockSpec((None, tm, ROUTER_PAD), lambda i, j: (i, j, 0)))
    kern = functools.partial(_resid_norm_kernel, n_branch=len(branches), weighted=weights is not None,
                             modulated=modulated, routed=routed)
    return pl.pallas_call(
        kern,
        out_shape=tuple(out_shapes),
        grid=(b, t_out // tm),
        in_specs=specs,
        out_specs=tuple(out_specs),
        compiler_params=_cparams("parallel", "parallel"),
        name="resid_norm",
    )(*args)


def _moe_kernel(te_ref, tv_ref, x_ref, wg_ref, wu_ref, wd_ref, o_ref, wg_s, wu_s, wd_s):
    i = pl.program_id(0)

    @pl.when(tv_ref[i] > 0)
    def _():
        @pl.when((i == 0) | (te_ref[i] != te_ref[jnp.maximum(i - 1, 0)]))
        def _():
            wg_s[...] = wg_ref[0].astype(BF16)
            wu_s[...] = wu_ref[0].astype(BF16)
            wd_s[...] = wd_ref[0].astype(BF16)

        x = x_ref[...]
        g = jnp.dot(x, wg_s[...], preferred_element_type=F32)
        u = jnp.dot(x, wu_s[...], preferred_element_type=F32)
        act = (g * jax.nn.sigmoid(g) * u).astype(BF16)
        o_ref[...] = jnp.dot(act, wd_s[...], preferred_element_type=F32).astype(o_ref.dtype)

    @pl.when(tv_ref[i] == 0)
    def _():
        o_ref[...] = jnp.zeros_like(o_ref)


def _route(aff, b_router):
    assert GROUP_SCORE_TOPK == 2 and TOP_K == 2
    epg = EXPERTS_PER_GROUP
    sel = aff + b_router.astype(F32)
    s = [sel[:, e] for e in range(N_EXPERTS)]
    a = [aff[:, e] for e in range(N_EXPERTS)]

    def first_max(vals):
        idx, best = jnp.zeros_like(vals[0], dtype=jnp.int32), vals[0]
        for e in range(1, len(vals)):
            upd = vals[e] > best
            idx, best = jnp.where(upd, e, idx), jnp.where(upd, vals[e], best)
        return idx, best

    def pick(vals, idx):
        out = vals[0]
        for e in range(1, len(vals)):
            out = jnp.where(idx == e, vals[e], out)
        return out

    def top2_sum(v):
        pairs = [v[i] + v[j] for i in range(len(v)) for j in range(i + 1, len(v))]
        return functools.reduce(jnp.maximum, pairs)

    g_best, _ = first_max([top2_sum(s[g * epg:(g + 1) * epg]) for g in range(N_GROUPS)])
    in_s = [pick([s[g * epg + e] for g in range(N_GROUPS)], g_best) for e in range(epg)]
    in_a = [pick([a[g * epg + e] for g in range(N_GROUPS)], g_best) for e in range(epg)]
    i1, _ = first_max(in_s)
    i2, _ = first_max([jnp.where(i1 == e, NEG_INF, in_s[e]) for e in range(epg)])
    w1, w2 = pick(in_a, i1), pick(in_a, i2)
    tot = w1 + w2
    expert_idx = jnp.stack([g_best * epg + i1, g_best * epg + i2], axis=-1)
    return expert_idx.astype(jnp.int32), jnp.stack([w1 / tot, w2 / tot], axis=-1)


def grouped_moe(h2, aff, b_router, w_gate, w_up, w_down, layer):
    n, d = h2.shape
    expert_idx, w = _route(aff, b_router)
    n_slots = TOP_K * n
    e_flat = expert_idx.reshape(-1)
    onehot = (e_flat[:, None] == jnp.arange(N_EXPERTS, dtype=jnp.int32)[None, :]).astype(jnp.int32)
    csum = jnp.cumsum(onehot, axis=0)
    rank = jnp.take_along_axis(csum, e_flat[:, None], axis=1)[:, 0] - 1
    counts = csum[-1]
    padded = ((counts + MOE_TILE - 1) // MOE_TILE) * MOE_TILE
    pad_end = jnp.cumsum(padded)
    pad_off = pad_end - padded
    dest_flat = pad_off[e_flat] + rank
    dest = dest_flat.reshape(n, TOP_K)

    n_tiles = -(-n_slots // MOE_TILE) + N_EXPERTS
    n_pad_slots = n_tiles * MOE_TILE
    tile_start = jnp.arange(n_tiles, dtype=jnp.int32) * MOE_TILE
    tile_expert = jnp.minimum(jnp.sum((tile_start[:, None] >= pad_end[None, :]).astype(jnp.int32), axis=1),
                              N_EXPERTS - 1)
    tile_valid = (tile_start < pad_end[-1]).astype(jnp.int32)

    slot_token = jnp.zeros((n_pad_slots,), jnp.int32).at[dest_flat].set(
        jnp.arange(n_slots, dtype=jnp.int32) // TOP_K, unique_indices=True, mode="promise_in_bounds")

    xs = h2.at[slot_token].get(mode="promise_in_bounds")
    f = D_EXPERT
    ys = pl.pallas_call(
        _moe_kernel,
        out_shape=jax.ShapeDtypeStruct((n_pad_slots, d), BF16),
        grid_spec=pltpu.PrefetchScalarGridSpec(
            num_scalar_prefetch=2,
            grid=(n_tiles,),
            in_specs=[pl.BlockSpec((MOE_TILE, d), lambda i, te, tv: (i, 0)),
                      pl.BlockSpec((None, 1, d, f), lambda i, te, tv: (layer, te[i], 0, 0),
                                   pipeline_mode=pl.Buffered(1)),
                      pl.BlockSpec((None, 1, d, f), lambda i, te, tv: (layer, te[i], 0, 0),
                                   pipeline_mode=pl.Buffered(1)),
                      pl.BlockSpec((None, 1, f, d), lambda i, te, tv: (layer, te[i], 0, 0),
                                   pipeline_mode=pl.Buffered(1))],
            out_specs=pl.BlockSpec((MOE_TILE, d), lambda i, te, tv: (i, 0)),
            scratch_shapes=[pltpu.VMEM((d, f), BF16), pltpu.VMEM((d, f), BF16), pltpu.VMEM((f, d), BF16)],
        ),
        compiler_params=_cparams("arbitrary"),
        name="moe_experts",
    )(tile_expert, tile_valid, xs, w_gate.astype(F32), w_up.astype(F32), w_down.astype(F32))
    return (ys.at[dest[:, 0]].get(mode="promise_in_bounds"), ys.at[dest[:, 1]].get(mode="promise_in_bounds"), w)


DA_TQ = 512


def _diff_attn_kernel(lam_ref, q_ref, k_ref, v_ref, cos_ref, sin_ref, g_ref, o_ref, k_scr, v_scr,
                      *, n_ctx, lam_init):
    i = pl.program_id(2)
    t = k_ref.shape[0]
    hw = 2 * DA_HALF
    lam = lam_ref[0]
    lane = lax.broadcasted_iota(jnp.int32, (1, hw), 1)
    first_half = (lane % (DA_HALF // 2)) < DA_HALF // 4
    nt = (((1,), (1,)), ((), ()))

    def rope(x, lo, n):
        xf = x.astype(F32)
        partner = jnp.where(first_half, pltpu.roll(xf, hw - DA_HALF // 4, axis=1), pltpu.roll(xf, DA_HALF // 4, axis=1))
        return xf * cos_ref[pl.ds(lo, n), :] + partner * sin_ref[pl.ds(lo, n), :]

    @pl.when(i == 0)
    def _():
        def body(c, carry):
            lo = pl.multiple_of(c * n_ctx, n_ctx)
            k_scr[pl.ds(lo, n_ctx), :] = rope(k_ref[pl.ds(lo, n_ctx), :], lo, n_ctx).astype(BF16)
            return carry

        lax.fori_loop(0, t // n_ctx, body, 0)
        v_scr[:, :hw] = v_ref[...]
        v_scr[:, hw:] = jnp.ones((t, hw), BF16)

    def attend(lo, n, k, v_ext):
        q = rope(q_ref[pl.ds(lo, n), :], lo, n)

        def one_map(in_map):
            qm = jnp.where(in_map, q, 0.0).astype(BF16)
            s = lax.dot_general(qm, k, nt, preferred_element_type=F32)
            p = jnp.exp((s - jnp.max(s, axis=-1, keepdims=True)).astype(BF16))
            oe = jnp.dot(p, v_ext, preferred_element_type=F32)
            return oe[:, :hw] * (1.0 / oe[:, hw:hw + 1])

        o = one_map(lane < DA_HALF) - lam * one_map(lane >= DA_HALF)
        o = o * lax.rsqrt(jnp.mean(o * o, axis=-1, keepdims=True) + 1e-5)
        o_ref[pl.ds(lo, n), :] = ((o * g_ref[...]) * (1.0 - lam_init)).astype(o_ref.dtype)

    @pl.when(i == 0)
    def _():
        attend(0, n_ctx, k_scr[:n_ctx, :], v_scr[:n_ctx, :])

    @pl.when(i > 0)
    def _():
        attend(pl.multiple_of(n_ctx + (i - 1) * DA_TQ, n_ctx), DA_TQ, k_scr[...], v_scr[...])


def diff_attn_core(qkv, cos, sin, lam, subln_g, n_ctx, lam_init):
    b, t, _ = qkv.shape
    assert (t - n_ctx) % DA_TQ == 0 and DA_TQ % n_ctx == 0
    nh = DA_HEADS
    hw = 2 * DA_HALF
    kern = functools.partial(_diff_attn_kernel, n_ctx=n_ctx, lam_init=lam_init)
    table = pl.BlockSpec((t, hw), lambda bi, h, i: (0, 0))

    def col(off):
        return pl.BlockSpec((None, t, hw), lambda bi, h, i: (bi, 0, off + h))

    return pl.pallas_call(
        kern,
        out_shape=jax.ShapeDtypeStruct((b, t, nh * hw), BF16),
        grid=(b, nh, 1 + (t - n_ctx) // DA_TQ),
        in_specs=[pl.BlockSpec(memory_space=pltpu.SMEM), col(0), col(nh), col(2 * nh), table, table,
                  pl.BlockSpec((1, hw), lambda bi, h, i: (0, 0))],
        out_specs=col(0),
        scratch_shapes=[pltpu.VMEM((t, hw), BF16), pltpu.VMEM((t, 2 * hw), BF16)],
        compiler_params=_cparams("parallel", "parallel", "arbitrary"),
        name="diff_attn",
    )(lam.reshape(1).astype(F32), qkv, qkv, qkv, cos, sin, subln_g.reshape(1, hw).astype(F32))


NA_QROWS = 8
NA_KROWS = 2 * NA_QROWS
NA_DR_PAD = 16
NEG_INF = float("-inf")
NA_BIAS_ROWS = 2 * WIN_R - 1
NA_BIAS_COLS = 2 * WIN_C - 1


def _na_kernel(rpb_ref, q_ref, k_ref, v_ref, o_ref, bias_ref, *, n_ctx, rows):
    h = pl.program_id(0)
    b = pl.program_id(1)
    j = pl.program_id(2)
    w = GRID_W
    nq = NA_QROWS * w
    nk = NA_KROWS * w
    lane = lax.broadcasted_iota(jnp.int32, (w, 2 * w), 1)
    nt = (((1,), (1,)), ((), ()))

    @pl.when((b == 0) & (j == 0))
    def _():
        c = lax.broadcasted_iota(jnp.int32, (w, 2 * w), 0)
        kc = lane % w
        c0 = jnp.clip(c - WIN_C // 2, 0, w - WIN_C)
        in_win = (kc >= c0) & (kc < c0 + WIN_C)
        single = []
        for dr in range(-(WIN_R - 1), WIN_R):
            t = jnp.full((w, 2 * w), NEG_INF, F32)
            for dc in range(-(WIN_C - 1), WIN_C):
                t = jnp.where(kc - c == dc, rpb_ref[(h * NA_BIAS_ROWS + dr + WIN_R - 1) * NA_BIAS_COLS + dc + WIN_C - 1], t)
            single.append(jnp.where(in_win, t, NEG_INF))
        neg = jnp.full((w, 2 * w), NEG_INF, F32)

        def at(dr):
            return single[dr + WIN_R - 1] if abs(dr) < WIN_R else neg

        for d in range(2 * NA_DR_PAD + 1):
            bias_ref[d] = jnp.where(lane < w, at(d - NA_DR_PAD), at(d - NA_DR_PAD + 1))

    kctx = k_ref[0, :n_ctx, :]
    vctx = v_ref[0, :n_ctx, :]

    def finish(parts, q_lo, n):
        m = parts[0][0].max(axis=-1, keepdims=True)
        for s, _ in parts[1:]:
            m = jnp.maximum(m, s.max(axis=-1, keepdims=True))
        l = 0.0
        o = 0.0
        for s, vv in parts:
            p = jnp.exp(s - m)
            l = l + p.sum(axis=-1, keepdims=True)
            o = o + jnp.dot(p.astype(BF16), vv, preferred_element_type=F32)
        o_ref[0, pl.ds(q_lo, n), :] = (o * (1.0 / l)).astype(o_ref.dtype)

    @pl.when(j < rows // NA_QROWS)
    def _():
        r_lo = j * NA_QROWS
        ks = jnp.clip(r_lo - WIN_R // 2, 0, rows - NA_KROWS)
        q_lo = pl.multiple_of(n_ctx + r_lo * w, w)
        k_lo = pl.multiple_of(n_ctx + ks * w, w)
        q = q_ref[0, pl.ds(q_lo, nq), :]
        kwin = k_ref[0, pl.ds(k_lo, nk), :]
        vwin = v_ref[0, pl.ds(k_lo, nk), :]
        s_win = lax.dot_general(q, kwin, nt, preferred_element_type=F32)
        half = (lane >= w).astype(jnp.int32)
        row_blocks = []
        for i in range(NA_QROWS):
            r = r_lo + i
            r0 = jnp.clip(r - WIN_R // 2, 0, rows - WIN_R)
            tiles = []
            for jj in range(NA_KROWS // 2):
                kr = ks + 2 * jj
                off = lax.bitcast_convert_type(half + (kr - r0), jnp.uint32)
                t = s_win[i * w:(i + 1) * w, 2 * jj * w:(2 * jj + 2) * w] + bias_ref[kr - r + NA_DR_PAD]
                tiles.append(jnp.where(off < WIN_R, t, NEG_INF))
            row_blocks.append(jnp.concatenate(tiles, axis=1))
        s_win = jnp.concatenate(row_blocks, axis=0)
        s_ctx = lax.dot_general(q, kctx, nt, preferred_element_type=F32)
        finish([(s_win, vwin), (s_ctx, vctx)], q_lo, nq)

    @pl.when(j == rows // NA_QROWS)
    def _():
        q = q_ref[0, :n_ctx, :]
        finish([(lax.dot_general(q, kctx, nt, preferred_element_type=F32), vctx)], 0, n_ctx)


def na_attn_core(qkv, rpb, n_ctx):
    b, t, _ = qkv.shape
    rows = (t - n_ctx) // GRID_W
    assert rows % NA_QROWS == 0 and rows >= NA_KROWS
    nh, hd = NA_HEADS, NA_DIM
    kern = functools.partial(_na_kernel, n_ctx=n_ctx, rows=rows)

    def col(off):
        return pl.BlockSpec((1, t, hd), lambda h, bi, j: (bi, 0, off + h))

    return pl.pallas_call(
        kern,
        out_shape=jax.ShapeDtypeStruct((b, t, nh * hd), BF16),
        grid=(nh, b, rows // NA_QROWS + 1),
        in_specs=[pl.BlockSpec(memory_space=pltpu.SMEM), col(0), col(nh), col(2 * nh)],
        out_specs=col(0),
        scratch_shapes=[pltpu.VMEM((2 * NA_DR_PAD + 1, GRID_W, 2 * GRID_W), F32)],
        compiler_params=_cparams("arbitrary", "arbitrary", "arbitrary"),
        name="na_attn",
    )(rpb.astype(F32).reshape(-1), qkv, qkv, qkv)


def proj(t, w, out_dtype=F32):
    b, tt, k = t.shape
    return mm(t.reshape(b * tt, k), w, out_dtype).reshape(b, tt, w.shape[1])


def axial_rope_tables(length, dim):
    n_freq = dim // 4
    t = jnp.arange(length)
    pos = jnp.stack([t // GRID_W, t % GRID_W], axis=-1).astype(F32)
    inv = ROPE_BASE ** (-jnp.arange(n_freq, dtype=F32) / n_freq)
    ang = pos[:, :, None] * inv
    return jnp.cos(ang), jnp.sin(ang)


def diff_attention(h, n_ctx, w_qkv, w_o, lam_vec, subln_g, layer_idx):
    t = h.shape[1]
    dq = 2 * DA_HEADS * DA_HALF
    lam_init = 0.8 - 0.6 * math.exp(-0.3 * layer_idx)
    lv = lam_vec.astype(F32)
    lam = jnp.exp(jnp.sum(lv[0] * lv[1])) - jnp.exp(jnp.sum(lv[2] * lv[3])) + lam_init
    w = jnp.concatenate([w_qkv[:, :dq] * DA_HALF ** -0.5, w_qkv[:, dq:]], axis=1)
    cos, sin = axial_rope_tables(t - n_ctx, DA_HALF)
    n_freq = cos.shape[-1]
    sign = jnp.array([-1.0, 1.0], F32)[None, None, :, None]
    cos_t = jnp.broadcast_to(cos[:, :, None, :], (t - n_ctx, 2, 2, n_freq)).reshape(t - n_ctx, DA_HALF)
    sin_t = (sin[:, :, None, :] * sign).reshape(t - n_ctx, DA_HALF)
    cos_t = jnp.concatenate([jnp.ones((n_ctx, 2 * DA_HALF), F32), jnp.tile(cos_t, (1, 2))], axis=0)
    sin_t = jnp.concatenate([jnp.zeros((n_ctx, 2 * DA_HALF), F32), jnp.tile(sin_t, (1, 2))], axis=0)
    o = diff_attn_core(proj(h, w, BF16), cos_t, sin_t, lam, subln_g, n_ctx, lam_init)
    return proj(o, w_o, BF16)


GDN_PREP_ROWS = 128
GDN_HALO = 8


def _gdn_prep_kernel(cur_ref, prev_ref, next_ref, w_ref, q_ref, k_ref, v_ref, *, first_latent_tile):
    j = pl.program_id(1)
    nt = pl.num_programs(1)
    tm = GDN_PREP_ROWS
    taps = w_ref.shape[0]
    left_ok = (j != 0) & (j != first_latent_tile)
    right_ok = (j != first_latent_tile - 1) & (j != nt - 1)
    n_ext = tm + 2 * GDN_HALO
    for grp, out_ref in enumerate((q_ref, k_ref, v_ref)):
        cols = slice(grp * out_ref.shape[1], (grp + 1) * out_ref.shape[1])
        prev = jnp.where(left_ok, prev_ref[:, cols], 0.0)
        nxt = jnp.where(right_ok, next_ref[:, cols], 0.0)
        ext = jnp.concatenate([prev, cur_ref[:, cols], nxt], axis=0)
        conv = None
        for tap in range(taps):
            shift = (taps // 2 - tap) % n_ext
            x = (pltpu.roll(ext, shift, axis=0) if shift else ext)[GDN_HALO:GDN_HALO + tm]
            term = x * w_ref[tap:tap + 1, cols]
            conv = term if conv is None else conv + term
        z = conv * jax.nn.sigmoid(conv)
        if out_ref is v_ref:
            out_ref[...] = z.astype(out_ref.dtype)
            continue
        scale = GDN_DK ** -0.5 if out_ref is q_ref else 1.0
        for h in range(GDN_HEADS):
            sl = slice(h * GDN_DK, (h + 1) * GDN_DK)
            zh = z[:, sl]
            out_ref[:, sl] = (zh * (lax.rsqrt(jnp.sum(zh * zh, axis=-1, keepdims=True) + EPS) * scale)
                              ).astype(out_ref.dtype)


def gdn_prep(zz, conv_w, n_ctx):
    b, t, _ = zz.shape
    tm, halo = GDN_PREP_ROWS, GDN_HALO
    hd = GDN_HEADS * GDN_DK
    wid = 3 * hd
    assert n_ctx % tm == 0 and t % tm == 0 and conv_w.shape[0] // 2 <= halo
    per = tm // halo
    out = pl.BlockSpec((None, tm, hd), lambda i, j: (i, j, 0))
    kern = functools.partial(_gdn_prep_kernel, first_latent_tile=n_ctx // tm)
    return pl.pallas_call(
        kern,
        out_shape=tuple(jax.ShapeDtypeStruct((b, t, hd), BF16) for _ in range(3)),
        grid=(b, t // tm),
        in_specs=[pl.BlockSpec((None, tm, wid), lambda i, j: (i, j, 0)),
                  pl.BlockSpec((None, halo, wid), lambda i, j: (i, jnp.maximum(j * per - 1, 0), 0)),
                  pl.BlockSpec((None, halo, wid), lambda i, j: (i, jnp.minimum((j + 1) * per, t // halo - 1), 0)),
                  pl.BlockSpec((conv_w.shape[0], wid), lambda i, j: (0, 0))],
        out_specs=(out, out, out),
        compiler_params=_cparams("parallel", "parallel"),
        name="gdn_prep",
    )(zz, zz, zz, conv_w.astype(F32))


GDN_HEAD_GROUP = 16
GDN_SUB = 16


def _bdot(a, b):
    return jnp.dot(a.astype(BF16), b.astype(BF16), preferred_element_type=F32)


def _scan_chunk(d, s, n, n_ctx_chunks):
    back = jnp.where(s < n_ctx_chunks, n_ctx_chunks - 1 - s, n + n_ctx_chunks - 1 - s)
    return jnp.where(d == 0, s, back)


def _gdn_kernel(qs_ref, k_ref, v_ref, gc_ref, bc_ref, gr_ref, o_ref, s_ref):
    d = pl.program_id(0)
    s = pl.program_id(3)
    c = GDN_CHUNK

    @pl.when(s == 0)
    def _():
        s_ref[...] = jnp.zeros_like(s_ref)

    row = lax.broadcasted_iota(jnp.int32, (c, c), 0)
    col = lax.broadcasted_iota(jnp.int32, (c, c), 1)
    ahead = (row - col) * (1 - 2 * d)
    strict = ahead > 0
    incl = ahead >= 0
    same_blk = (row // GDN_SUB) == (col // GDN_SUB)
    eye = (row == col).astype(F32)
    nt = (((1,), (1,)), ((), ()))
    tn = (((0,), (0,)), ((), ()))
    heads = range(GDN_HEAD_GROUP)
    sls = [slice(g * GDN_DK, (g + 1) * GDN_DK) for g in heads]

    def each(fn, *lists):
        return [fn(*vals) for vals in zip(*lists)]

    gcol = gc_ref[...]
    g_last = jnp.min(gcol, axis=0, keepdims=True)
    e_g, e_rest, e_all = jnp.exp(gcol), jnp.exp(g_last - gcol), jnp.exp(g_last)
    beta = bc_ref[...]
    ks = [k_ref[:, sl] for sl in sls]
    kf = [k.astype(F32) for k in ks]
    kb = [k * beta[:, g:g + 1] for g, k in zip(heads, kf)]
    kbg = [(k * e_g[:, g:g + 1]).astype(BF16) for g, k in zip(heads, kb)]
    kd = [(k * e_rest[:, g:g + 1]).astype(BF16) for g, k in zip(heads, kf)]
    vb = [(v_ref[:, sl].astype(F32) * beta[:, g:g + 1]).astype(BF16) for g, sl in zip(heads, sls)]
    qg = [(qs_ref[:, sl].astype(F32) * e_g[:, g:g + 1]).astype(BF16) for g, sl in zip(heads, sls)]
    decay = [jnp.exp(jnp.where(incl, gcol[:, g:g + 1] - gr_ref[g:g + 1, :], NEG_INF)) for g in heads]
    a = each(lambda m, k, dc: jnp.where(
        strict, lax.dot_general(m.astype(BF16), k, nt, preferred_element_type=F32) * dc, 0.0), kb, ks, decay)
    qk = each(lambda sl, k, dc: jnp.where(
        incl, lax.dot_general(qs_ref[:, sl], k, nt, preferred_element_type=F32) * dc, 0.0), sls, ks, decay)
    dblk = each(lambda m: jnp.where(same_blk, m, 0.0), a)
    d2 = each(lambda m: _bdot(m, m), dblk)
    x = each(lambda m, m2: _bdot(eye - m, eye + m2), dblk, d2)
    d4 = each(lambda m: _bdot(m, m), d2)
    x = each(lambda m, m4: _bdot(m, eye + m4), x, d4)
    d8 = each(lambda m: _bdot(m, m), d4)
    x = each(lambda m, m8: _bdot(m, eye + m8), x, d8)
    nmat = each(lambda m, am, dm: _bdot(m, am - dm), x, a, dblk)
    n2 = each(lambda m: _bdot(m, m), nmat)
    y = each(lambda m, m2: _bdot(eye - m, eye + m2), nmat, n2)
    t = each(_bdot, y, x)
    uw = each(lambda m, p, q: _bdot(m, jnp.concatenate([p, q], axis=1)), t, vb, kbg)
    state = [s_ref[g] for g in heads]
    v_new = each(lambda m, st: m[:, :GDN_DV] - _bdot(m[:, GDN_DV:], st), uw, state)
    o_state = each(_bdot, qg, state)
    o_new = each(_bdot, qk, v_new)
    s_new = each(lambda m, vn: lax.dot_general(m, vn.astype(BF16), tn, preferred_element_type=F32), kd, v_new)
    for g in heads:
        s_ref[g] = state[g] * e_all[:, g:g + 1] + s_new[g]
    o_ref[...] = jnp.concatenate(each(lambda p, q: p + q, o_state, o_new), axis=1)


def gated_delta_core(qs, k, v, g, beta, n_ctx_chunks):
    b, t, hd = k.shape
    c, hh, gg = GDN_CHUNK, GDN_HEADS, GDN_HEAD_GROUP
    n = t // c
    gw = gg * GDN_DK

    def cols(a):
        return a.reshape(2, b, t, hh // gg, gg).transpose(0, 1, 3, 2, 4)

    g_row = g.reshape(2, b, n, c, hh // gg, gg).transpose(0, 1, 2, 4, 5, 3)

    def chunk_of(d, s):
        return _scan_chunk(d, s, n, n_ctx_chunks)

    shared = pl.BlockSpec((None, c, gw), lambda d, bi, hg, s: (bi, chunk_of(d, s), hg))
    col_spec = pl.BlockSpec((None, None, None, c, gg), lambda d, bi, hg, s: (d, bi, hg, chunk_of(d, s), 0))
    return pl.pallas_call(
        _gdn_kernel,
        out_shape=jax.ShapeDtypeStruct((2, b, t, hd), F32),
        grid=(2, b, hh // gg, n),
        in_specs=[shared, shared, shared, col_spec, col_spec,
                  pl.BlockSpec((None, None, None, None, gg, c), lambda d, bi, hg, s: (d, bi, chunk_of(d, s), hg, 0, 0))],
        out_specs=pl.BlockSpec((None, None, c, gw), lambda d, bi, hg, s: (d, bi, chunk_of(d, s), hg)),
        scratch_shapes=[pltpu.VMEM((gg, GDN_DK, GDN_DV), F32)],
        compiler_params=_cparams("parallel", "parallel", "parallel", "arbitrary"),
        name="gated_delta",
    )(qs, k, v, cols(g), cols(beta), g_row)


def gated_deltanet(h, n_ctx, w_in, conv_w, w_ab, dt_bias, a_log, norm_g, w_o):
    B, T, _ = h.shape
    tc = n_ctx
    H, C = GDN_HEADS, GDN_CHUNK
    wq = H * GDN_DK
    hi = 2 * wq + H * GDN_DV

    zz = proj(h, w_in)
    q, k, v = gdn_prep(zz, conv_w, tc)
    ab = proj(h, w_ab).astype(F32).reshape(B, T, 2, 2, H)
    la = -jnp.exp(a_log.astype(F32)) * jax.nn.softplus(ab[:, :, 0] + dt_bias.astype(F32))
    be = jax.nn.sigmoid(ab[:, :, 1])
    n = T // C
    la_c = la.reshape(B, n, C, 2, H)
    g_f = jnp.cumsum(la_c[:, :, :, 0], axis=2)
    g_b = jnp.flip(jnp.cumsum(jnp.flip(la_c[:, :, :, 1], axis=2), axis=2), axis=2)
    g = jnp.stack([g_f, g_b]).reshape(2, B, T, H)
    o = gated_delta_core(q, k, v, g, jnp.moveaxis(be, 2, 0), tc // C)
    o = o.reshape(2, B * T, -1)
    y = gated_norm_proj((o, 0), (o, 1), zz.reshape(B * T, -1), hi // (H * GDN_DV), norm_g, w_o, silu_gate=True)
    return y.reshape(B, T, -1)


def neighbourhood_attention(h, n_ctx, w_qkv, rpb, w_o):
    hd = NA_HEADS * NA_DIM
    w = jnp.concatenate([w_qkv[:, :hd] * NA_DIM ** -0.5, w_qkv[:, hd:]], axis=1)
    return proj(na_attn_core(proj(h, w, BF16), rpb, n_ctx), w_o, BF16)


GLA_CHUNK = 64
GLA_LEVELS = (1, 2, 4, 8, 16, 32)


def _gla_kernel(zq_ref, zv_ref, zf_ref, lb_ref, bf_ref, o_ref, st_ref, *, reverse):
    s = pl.program_id(1)
    c = GLA_CHUNK

    @pl.when(s == 0)
    def _():
        st_ref[...] = jnp.zeros_like(st_ref)

    row = lax.broadcasted_iota(jnp.int32, (c, c), 0)
    col = lax.broadcasted_iota(jnp.int32, (c, c), 1)
    late, early = (col, row) if reverse else (row, col)
    masks = []
    for m in GLA_LEVELS:
        masks.append(((row // (2 * m)) == (col // (2 * m))) & ((late % (2 * m)) >= m) & ((early % (2 * m)) < m))
    diag = row == col
    trow = lax.broadcasted_iota(jnp.int32, (c, HG_DK), 0)
    nt = (((1,), (1,)), ((), ()))
    tn = (((0,), (0,)), ((), ()))
    heads = range(HG_HEADS)
    sls = [slice(g * HG_DK, (g + 1) * HG_DK) for g in heads]

    def ref_rows(b, m):
        p = m if reverse else m - 1
        if 2 * m >= 8:
            blocks = b.reshape(c // (2 * m), 2 * m, HG_DK)
            return jnp.broadcast_to(blocks[:, p:p + 1, :], blocks.shape).reshape(c, HG_DK)
        out = b
        for rho in range(2 * m):
            if rho != p:
                out = jnp.where(trow % (2 * m) == rho, pltpu.roll(b, (rho - p) % c, axis=0), out)
        return out

    zq = zq_ref[...]
    q_all = zq * jax.nn.sigmoid(zq) * HG_DK ** -0.5
    v_all = zv_ref[...].astype(BF16)
    z = zf_ref[...] + bf_ref[...]
    t = jnp.exp(-jnp.abs(z))
    r = 1.0 / (1.0 + t)
    pos = z >= 0
    lb = lb_ref[...]
    log_f = jnp.log(lb + (1.0 - lb) * jnp.where(pos, r, t * r))
    k_all = (1.0 - lb) * jnp.where(pos, t * r, r)
    scan = (col >= row) if reverse else (col <= row)
    b_all = jnp.dot(scan.astype(F32), log_f, preferred_element_type=F32, precision=lax.Precision.HIGHEST)
    qs = [q_all[:, sl] for sl in sls]
    ks = [k_all[:, sl] for sl in sls]
    bs = [b_all[:, sl] for sl in sls]
    vs = [v_all[:, sl] for sl in sls]
    att = [jnp.where(diag, lax.dot_general(q.astype(BF16), k.astype(BF16), nt, preferred_element_type=F32), 0.0)
           for q, k in zip(qs, ks)]
    for m, mask in zip(GLA_LEVELS, masks):
        es = [jnp.exp(-jnp.abs(b - ref_rows(b, m))) for b in bs]
        sc = [lax.dot_general((q * e).astype(BF16), (k * e).astype(BF16), nt, preferred_element_type=F32)
              for q, k, e in zip(qs, ks, es)]
        att = [jnp.where(mask, x, a) for x, a in zip(sc, att)]
    last = c - 1 if not reverse else 0
    b_last = [b[last:last + 1, :] for b in bs]
    state = [st_ref[g] for g in heads]
    o_state = [lax.dot_general((q * jnp.exp(b)).astype(BF16), st.astype(BF16), nt, preferred_element_type=F32)
               for q, b, st in zip(qs, bs, state)]
    o_new = [jnp.dot(a.astype(BF16), v, preferred_element_type=F32) for a, v in zip(att, vs)]
    s_new = [lax.dot_general(v, (k * jnp.exp(bl - b)).astype(BF16), tn, preferred_element_type=F32)
             for v, k, b, bl in zip(vs, ks, bs, b_last)]
    for g in heads:
        st_ref[g] = state[g] * jnp.exp(b_last[g]) + s_new[g]
    o_ref[...] = jnp.concatenate([x + y for x, y in zip(o_state, o_new)], axis=1)


def gla_core(zz, lb, b_f, n_ctx_chunks, reverse):
    bsz, t, _ = zz.shape
    hd = HG_HEADS * HG_DK
    c = GLA_CHUNK
    n = t // c

    def chunk_of(s):
        return _scan_chunk(1, s, n, n_ctx_chunks) if reverse else s

    def cols(j):
        return pl.BlockSpec((None, c, hd), lambda bi, s: (bi, chunk_of(s), j))

    vec = pl.BlockSpec((1, hd), lambda bi, s: (0, 0))
    return pl.pallas_call(
        functools.partial(_gla_kernel, reverse=reverse),
        out_shape=jax.ShapeDtypeStruct((bsz, t, hd), F32),
        grid=(bsz, n),
        in_specs=[cols(0), cols(1), cols(3 if reverse else 2), vec, vec],
        out_specs=cols(0),
        scratch_shapes=[pltpu.VMEM((HG_HEADS, HG_DV, HG_DK), F32)],
        compiler_params=_cparams("parallel", "arbitrary"),
        name="gla_bwd" if reverse else "gla_fwd",
    )(zz, zz, zz, lb.reshape(1, hd).astype(F32), b_f.reshape(1, hd).astype(F32))


def hgrn2(h, n_ctx, w_q, w_i, w_f, b_f, w_g, norm_g, w_o, lb):
    B, T, _ = h.shape
    C = GLA_CHUNK
    hw = HG_HEADS * HG_DK

    zz = proj(h, jnp.concatenate([w_q, w_i, w_f[0], w_f[1], w_g], axis=1))
    o_f = gla_core(zz, lb, b_f[0], n_ctx // C, False).reshape(B * T, hw)
    o_b = gla_core(zz, lb, b_f[1], n_ctx // C, True).reshape(B * T, hw)
    y = gated_norm_proj((o_f, None), (o_b, None), zz.reshape(B * T, -1), 4, norm_g, w_o, silu_gate=False)
    return y.reshape(B, T, -1)


def kernel(x, c, ctx, c_ctx, w_mod, b_mod, norm_mix_g, norm_ffn_g, da_w_qkv, da_w_o, da_lam, da_subln_g, gdn_w_in, gdn_conv, gdn_w_ab, gdn_dt_bias, gdn_a_log, gdn_norm_g, gdn_w_o, na_w_qkv, na_rpb, na_w_o, hg_w_q, hg_w_i, hg_w_f, hg_b_f, hg_w_g, hg_norm_g, hg_w_o, hg_lb_logits, w_router, b_router, e_w_gate, e_w_up, e_w_down, final_norm_g):
    B, L, D = x.shape
    tc = ctx.shape[1]
    T = tc + L
    xa = jnp.concatenate([ctx, x], axis=1)
    p_lb = jax.nn.softmax(hg_lb_logits.astype(F32), axis=0)
    lb_all = jnp.cumsum(p_lb, axis=0) - p_lb[0]
    cond_all = jnp.concatenate([jax.nn.silu(c), jax.nn.silu(c_ctx)[None]], axis=0)
    cond_all = jnp.pad(cond_all, ((0, -(B + 1) % LANES), (0, 0)))

    def mods(i):
        mod = mm(cond_all, w_mod[i])[:B + 1] + b_mod[i]
        both = jnp.stack([jnp.broadcast_to(mod[B], (B, 6 * D)), mod[:B]], axis=1)
        return [both[:, :, None, k * D:(k + 1) * D] for k in range(6)]

    sh1, sc1, g1, sh2, sc2, g2 = mods(0)
    (h,) = resid_norm(xa, tc, norm_mix_g[0], sc=sc1, sh=sh1)
    for i in range(DEPTH):
        kind, j = i % N_MIXERS, i // N_MIXERS
        if kind == 0:
            y = diff_attention(h, tc, da_w_qkv[j], da_w_o[j], da_lam[j], da_subln_g[j], i)
        elif kind == 1:
            y = gated_deltanet(h, tc, gdn_w_in[j], gdn_conv[j], gdn_w_ab[j], gdn_dt_bias[j],
                               gdn_a_log[j], gdn_norm_g[j], gdn_w_o[j])
        elif kind == 2:
            y = neighbourhood_attention(h, tc, na_w_qkv[j], na_rpb[j], na_w_o[j])
        else:
            y = hgrn2(h, tc, hg_w_q[j], hg_w_i[j], hg_w_f[j], hg_b_f[j], hg_w_g[j],
                      hg_norm_g[j], hg_w_o[j], lb_all[i])
        xa, h2, aff = resid_norm(xa, tc, norm_ffn_g[i], branches=(y,), gate=g1, sc=sc2, sh=sh2, w_router=w_router)
        ya, yb, w = grouped_moe(h2.reshape(B * T, D), aff.reshape(B * T, ROUTER_PAD)[:, :N_EXPERTS],
                                b_router, e_w_gate, e_w_up, e_w_down, i)
        branches = (ya.reshape(B, T, D), yb.reshape(B, T, D))
        w = w.reshape(B, T, TOP_K)
        if i == DEPTH - 1:
            return resid_norm(xa, tc, final_norm_g, branches=branches, weights=w, gate=g2, latent_only=True)[1]
        gate_ffn = g2
        sh1, sc1, g1, sh2, sc2, g2 = mods(i + 1)
        xa, h = resid_norm(xa, tc, norm_mix_g[i + 1], branches=branches, weights=w, gate=gate_ffn, sc=sc1, sh=sh1)
```

```python
import functools
import math

import jax
import jax.numpy as jnp
from jax import lax
from jax.experimental import pallas as pl
from jax.experimental.pallas import tpu as pltpu

F32 = jnp.float32
BF16 = jnp.bfloat16

D_MODEL = 2048
DEPTH = 4
GRID_W = 64
N_MIXERS = 4
EPS = 1e-6
DA_HEADS = 16
DA_HALF = D_MODEL // DA_HEADS // 2
DA_VDIM = 2 * DA_HALF
ROPE_BASE = 10000.0
GDN_HEADS = 16
GDN_DK = D_MODEL // GDN_HEADS
GDN_DV = D_MODEL // GDN_HEADS
GDN_CHUNK = 64
NA_HEADS = 16
NA_DIM = D_MODEL // NA_HEADS
WIN_R = 8
WIN_C = 16
HG_HEADS = 16
HG_DK = D_MODEL // HG_HEADS
HG_DV = D_MODEL // HG_HEADS
HG_CHUNK = 32
N_EXPERTS = 16
N_GROUPS = 4
EXPERTS_PER_GROUP = N_EXPERTS // N_GROUPS
GROUP_SCORE_TOPK = 2
TOP_K = 2
D_EXPERT = D_MODEL // 2

V7X_VMEM_LIMIT_BYTES = 56 * 1024 * 1024
LANES = 128
MOE_TILE = 512
MOE_CHUNKS = 4
ROUTER_PAD = LANES


def _largest_divisor(n, candidates):
    for c in candidates:
        if n % c == 0:
            return c
    raise ValueError(f"no tile in {candidates} divides {n}")


def _cparams(*sem):
    return pltpu.CompilerParams(dimension_semantics=sem, vmem_limit_bytes=V7X_VMEM_LIMIT_BYTES)


def _mm_kernel(a_ref, w_ref, o_ref):
    o_ref[...] = jnp.dot(a_ref[...], w_ref[...], preferred_element_type=F32).astype(o_ref.dtype)


def mm(a, w, out_dtype=F32):
    m, k = a.shape
    n = w.shape[1]
    n_pad = -n % LANES
    if n_pad:
        w = jnp.pad(w, ((0, 0), (0, n_pad)))
    a = a.astype(BF16)
    w = w.astype(BF16)
    np_ = n + n_pad
    tm = _largest_divisor(m, (1024, 512, 256, 128))
    tn = _largest_divisor(np_, (512, 256, 128))
    out = pl.pallas_call(
        _mm_kernel,
        out_shape=jax.ShapeDtypeStruct((m, np_), out_dtype),
        grid=(m // tm, np_ // tn),
        in_specs=[pl.BlockSpec((tm, k), lambda i, j: (i, 0)),
                  pl.BlockSpec((k, tn), lambda i, j: (0, j))],
        out_specs=pl.BlockSpec((tm, tn), lambda i, j: (i, j)),
        compiler_params=_cparams("parallel", "arbitrary"),
        name="proj_mm",
    )(a, w)
    return out[:, :n] if n_pad else out


def _gated_norm_proj_kernel(of_ref, ob_ref, zg_ref, g_ref, w_ref, o_ref, a_scr, *, head_dim, silu_gate):
    @pl.when(pl.program_id(1) == 0)
    def _():
        o = of_ref[...] + ob_ref[...]
        zg = zg_ref[...].astype(F32)
        sig = jax.nn.sigmoid(zg)
        gate = zg * sig if silu_gate else sig
        for h in range(o.shape[1] // head_dim):
            sl = slice(h * head_dim, (h + 1) * head_dim)
            oh = o[:, sl]
            oh = oh * lax.rsqrt(jnp.mean(oh * oh, axis=-1, keepdims=True) + EPS) * g_ref[...]
            a_scr[:, sl] = (oh * gate[:, sl]).astype(BF16)

    o_ref[...] = jnp.dot(a_scr[...], w_ref[...], preferred_element_type=F32).astype(o_ref.dtype)


def gated_norm_proj(o_fwd, o_bwd, zz, gate_block, norm_g, w_o, *, silu_gate):
    (fa, fi), (ba, bi) = o_fwd, o_bwd
    m, hd = fa.shape[-2:]
    k, n = w_o.shape
    assert k == hd
    head_dim = norm_g.shape[0]
    tm = _largest_divisor(m, (512, 256, 128))
    tn = _largest_divisor(n, (512, 256, 128))

    def rows(idx):
        if idx is None:
            return pl.BlockSpec((tm, hd), lambda i, j: (i, 0))
        return pl.BlockSpec((None, tm, hd), lambda i, j: (idx, i, 0))

    kern = functools.partial(_gated_norm_proj_kernel, head_dim=head_dim, silu_gate=silu_gate)
    return pl.pallas_call(
        kern,
        out_shape=jax.ShapeDtypeStruct((m, n), BF16),
        grid=(m // tm, n // tn),
        in_specs=[rows(fi), rows(bi),
                  pl.BlockSpec((tm, hd), lambda i, j: (i, gate_block)),
                  pl.BlockSpec((1, head_dim), lambda i, j: (0, 0)),
                  pl.BlockSpec((k, tn), lambda i, j: (0, j))],
        out_specs=pl.BlockSpec((tm, tn), lambda i, j: (i, j)),
        scratch_shapes=[pltpu.VMEM((tm, hd), BF16)],
        compiler_params=_cparams("parallel", "arbitrary"),
        name="gated_norm_proj",
    )(fa, ba, zz, norm_g.reshape(1, head_dim).astype(F32), w_o.astype(BF16))


def _resid_norm_kernel(*refs, n_branch, weighted, modulated, routed):
    refs = list(refs)
    x = refs.pop(0)[...].astype(F32)
    ys = [refs.pop(0)[...].astype(F32) for _ in range(n_branch)]
    if weighted:
        w = refs.pop(0)[...]
        ys = [y * w[:, k:k + 1] for k, y in enumerate(ys)]
    if n_branch:
        x = x + refs.pop(0)[...] * functools.reduce(lambda p, q: p + q, ys)
    g = refs.pop(0)[...]
    out = x * lax.rsqrt(jnp.mean(x * x, axis=-1, keepdims=True) + EPS) * g
    if modulated:
        sc = refs.pop(0)[...]
        sh = refs.pop(0)[...]
        out = out * (1.0 + sc) + sh
    wr = refs.pop(0) if routed else None
    if n_branch:
        refs.pop(0)[...] = x
    o_ref = refs.pop(0)
    o_ref[...] = out.astype(o_ref.dtype)
    if routed:
        logits = jnp.dot(out, wr[...], preferred_element_type=F32, precision=lax.Precision.HIGHEST)
        refs.pop(0)[...] = jax.nn.sigmoid(logits)


def resid_norm(x, n_ctx, gain, *, branches=(), weights=None, gate=None, sc=None, sh=None, w_router=None,
               latent_only=False):
    b, t, d = x.shape
    tm = n_ctx
    assert t % tm == 0
    off = 1 if latent_only else 0
    rows = pl.BlockSpec((None, tm, d), lambda i, j: (i, j + off, 0))
    out_rows = pl.BlockSpec((None, tm, d), lambda i, j: (i, j, 0))
    seg = pl.BlockSpec((None, None, 1, d), lambda i, j: (i, jnp.minimum(j + off, 1), 0, 0))
    args, specs = [x], [rows]
    for y in branches:
        args.append(y)
        specs.append(rows)
    if weights is not None:
        args.append(weights)
        specs.append(pl.BlockSpec((None, tm, weights.shape[-1]), lambda i, j: (i, j + off, 0)))
    if branches:
        args.append(gate)
        specs.append(seg)
    args.append(gain.reshape(1, d).astype(F32))
    specs.append(pl.BlockSpec((1, d), lambda i, j: (0, 0)))
    modulated = sc is not None
    if modulated:
        args += [sc, sh]
        specs += [seg, seg]
    routed = w_router is not None
    if routed:
        args.append(jnp.pad(w_router.astype(F32), ((0, 0), (0, ROUTER_PAD - w_router.shape[1]))))
        specs.append(pl.BlockSpec((d, ROUTER_PAD), lambda i, j: (0, 0)))
    t_out = t - off * tm
    out_shapes, out_specs = [], []
    if branches:
        out_shapes.append(jax.ShapeDtypeStruct((b, t_out, d), F32))
        out_specs.append(out_rows)
    out_shapes.append(jax.ShapeDtypeStruct((b, t_out, d), BF16 if modulated else F32))
    out_specs.append(out_rows)
    if routed:
        out_shapes.append(jax.ShapeDtypeStruct((b, t_out, ROUTER_PAD), F32))
        out_specs.append(pl.BlockSpec((None, tm, ROUTER_PAD), lambda i, j: (i, j, 0)))
    kern = functools.partial(_resid_norm_kernel, n_branch=len(branches), weighted=weights is not None,
                             modulated=modulated, routed=routed)
    return pl.pallas_call(
        kern,
        out_shape=tuple(out_shapes),
        grid=(b, t_out // tm),
        in_specs=specs,
        out_specs=tuple(out_specs),
        compiler_params=_cparams("parallel", "parallel"),
        name="resid_norm",
    )(*args)


def _moe_kernel(te_ref, tv_ref, x_ref, wg_ref, wu_ref, wd_ref, *rest, first_tile):
    o_ref, wg_s, wu_s, wd_s = rest[-4:]
    step = pl.program_id(0)
    i = step + first_tile

    @pl.when(tv_ref[i] > 0)
    def _():
        @pl.when((step == 0) | (te_ref[i] != te_ref[jnp.maximum(i - 1, 0)]))
        def _():
            wg_s[...] = wg_ref[0].astype(BF16)
            wu_s[...] = wu_ref[0].astype(BF16)
            wd_s[...] = wd_ref[0].astype(BF16)

        x = x_ref[...]
        g = jnp.dot(x, wg_s[...], preferred_element_type=F32)
        u = jnp.dot(x, wu_s[...], preferred_element_type=F32)
        act = (g * jax.nn.sigmoid(g) * u).astype(BF16)
        o_ref[...] = jnp.dot(act, wd_s[...], preferred_element_type=F32).astype(o_ref.dtype)

    @pl.when(tv_ref[i] == 0)
    def _():
        o_ref[...] = jnp.zeros_like(o_ref)


def _route(aff, b_router):
    assert GROUP_SCORE_TOPK == 2 and TOP_K == 2
    epg = EXPERTS_PER_GROUP
    sel = aff + b_router.astype(F32)
    s = [sel[:, e] for e in range(N_EXPERTS)]
    a = [aff[:, e] for e in range(N_EXPERTS)]

    def first_max(vals):
        idx, best = jnp.zeros_like(vals[0], dtype=jnp.int32), vals[0]
        for e in range(1, len(vals)):
            upd = vals[e] > best
            idx, best = jnp.where(upd, e, idx), jnp.where(upd, vals[e], best)
        return idx, best

    def pick(vals, idx):
        out = vals[0]
        for e in range(1, len(vals)):
            out = jnp.where(idx == e, vals[e], out)
        return out

    def top2_sum(v):
        pairs = [v[i] + v[j] for i in range(len(v)) for j in range(i + 1, len(v))]
        return functools.reduce(jnp.maximum, pairs)

    g_best, _ = first_max([top2_sum(s[g * epg:(g + 1) * epg]) for g in range(N_GROUPS)])
    in_s = [pick([s[g * epg + e] for g in range(N_GROUPS)], g_best) for e in range(epg)]
    in_a = [pick([a[g * epg + e] for g in range(N_GROUPS)], g_best) for e in range(epg)]
    i1, _ = first_max(in_s)
    i2, _ = first_max([jnp.where(i1 == e, NEG_INF, in_s[e]) for e in range(epg)])
    w1, w2 = pick(in_a, i1), pick(in_a, i2)
    tot = w1 + w2
    expert_idx = jnp.stack([g_best * epg + i1, g_best * epg + i2], axis=-1)
    return expert_idx.astype(jnp.int32), jnp.stack([w1 / tot, w2 / tot], axis=-1)


def grouped_moe(h2, aff, b_router, w_gate, w_up, w_down, layer):
    n, d = h2.shape
    expert_idx, w = _route(aff, b_router)
    n_slots = TOP_K * n
    e_flat = expert_idx.reshape(-1)
    onehot = (e_flat[:, None] == jnp.arange(N_EXPERTS, dtype=jnp.int32)[None, :]).astype(jnp.int32)
    csum = jnp.cumsum(onehot, axis=0)
    rank = jnp.take_along_axis(csum, e_flat[:, None], axis=1)[:, 0] - 1
    counts = csum[-1]
    padded = ((counts + MOE_TILE - 1) // MOE_TILE) * MOE_TILE
    pad_end = jnp.cumsum(padded)
    pad_off = pad_end - padded
    dest_flat = pad_off[e_flat] + rank
    dest = dest_flat.reshape(n, TOP_K)

    n_tiles = -(-n_slots // MOE_TILE) + N_EXPERTS
    n_pad_slots = n_tiles * MOE_TILE
    tile_start = jnp.arange(n_tiles, dtype=jnp.int32) * MOE_TILE
    tile_expert = jnp.minimum(jnp.sum((tile_start[:, None] >= pad_end[None, :]).astype(jnp.int32), axis=1),
                              N_EXPERTS - 1)
    tile_valid = (tile_start < pad_end[-1]).astype(jnp.int32)

    slot_token = jnp.zeros((n_pad_slots,), jnp.int32).at[dest_flat].set(
        jnp.arange(n_slots, dtype=jnp.int32) // TOP_K, unique_indices=True, mode="promise_in_bounds")

    f = D_EXPERT
    weights = (w_gate.astype(F32), w_up.astype(F32), w_down.astype(F32))
    bounds = [n_tiles * c // MOE_CHUNKS for c in range(MOE_CHUNKS + 1)]
    ys = None
    for lo, hi in zip(bounds[:-1], bounds[1:]):
        xs = h2.at[slot_token[lo * MOE_TILE:hi * MOE_TILE]].get(mode="promise_in_bounds")

        def w_spec(shape):
            return pl.BlockSpec((None, 1) + shape, lambda i, te, tv: (layer, te[i + lo], 0, 0),
                                pipeline_mode=pl.Buffered(1))

        in_specs = [pl.BlockSpec((MOE_TILE, d), lambda i, te, tv: (i, 0)), w_spec((d, f)), w_spec((d, f)), w_spec((f, d))]
        args = [tile_expert, tile_valid, xs, *weights]
        aliases = {}
        if ys is not None:
            in_specs.append(pl.BlockSpec(memory_space=pl.ANY))
            args.append(ys)
            aliases = {len(args) - 1: 0}
        ys = pl.pallas_call(
            functools.partial(_moe_kernel, first_tile=lo),
            out_shape=jax.ShapeDtypeStruct((n_pad_slots, d), BF16),
            grid_spec=pltpu.PrefetchScalarGridSpec(
                num_scalar_prefetch=2,
                grid=(hi - lo,),
                in_specs=in_specs,
                out_specs=pl.BlockSpec((MOE_TILE, d), lambda i, te, tv: (i + lo, 0)),
                scratch_shapes=[pltpu.VMEM((d, f), BF16), pltpu.VMEM((d, f), BF16), pltpu.VMEM((f, d), BF16)],
            ),
            input_output_aliases=aliases,
            compiler_params=_cparams("arbitrary"),
            name="moe_experts",
        )(*args)
    return (ys.at[dest[:, 0]].get(mode="promise_in_bounds"), ys.at[dest[:, 1]].get(mode="promise_in_bounds"), w)


DA_TQ = 512


def _diff_attn_kernel(lam_ref, q_ref, k_ref, v_ref, cos_ref, sin_ref, g_ref, o_ref, k_scr, v_scr,
                      *, n_ctx, lam_init):
    i = pl.program_id(2)
    t = k_ref.shape[0]
    hw = 2 * DA_HALF
    lam = lam_ref[0]
    lane = lax.broadcasted_iota(jnp.int32, (1, hw), 1)
    first_half = (lane % (DA_HALF // 2)) < DA_HALF // 4
    nt = (((1,), (1,)), ((), ()))

    def rope(x, lo, n):
        xf = x.astype(F32)
        partner = jnp.where(first_half, pltpu.roll(xf, hw - DA_HALF // 4, axis=1), pltpu.roll(xf, DA_HALF // 4, axis=1))
        return xf * cos_ref[pl.ds(lo, n), :] + partner * sin_ref[pl.ds(lo, n), :]

    @pl.when(i == 0)
    def _():
        def body(c, carry):
            lo = pl.multiple_of(c * n_ctx, n_ctx)
            k_scr[pl.ds(lo, n_ctx), :] = rope(k_ref[pl.ds(lo, n_ctx), :], lo, n_ctx).astype(BF16)
            return carry

        lax.fori_loop(0, t // n_ctx, body, 0)
        v_scr[:, :hw] = v_ref[...]
        v_scr[:, hw:] = jnp.ones((t, hw), BF16)

    def attend(lo, n, k, v_ext):
        q = rope(q_ref[pl.ds(lo, n), :], lo, n)

        def one_map(in_map):
            qm = jnp.where(in_map, q, 0.0).astype(BF16)
            s = lax.dot_general(qm, k, nt, preferred_element_type=F32)
            p = jnp.exp((s - jnp.max(s, axis=-1, keepdims=True)).astype(BF16))
            oe = jnp.dot(p, v_ext, preferred_element_type=F32)
            return oe[:, :hw] * (1.0 / oe[:, hw:hw + 1])

        o = one_map(lane < DA_HALF) - lam * one_map(lane >= DA_HALF)
        o = o * lax.rsqrt(jnp.mean(o * o, axis=-1, keepdims=True) + 1e-5)
        o_ref[pl.ds(lo, n), :] = ((o * g_ref[...]) * (1.0 - lam_init)).astype(o_ref.dtype)

    @pl.when(i == 0)
    def _():
        attend(0, n_ctx, k_scr[:n_ctx, :], v_scr[:n_ctx, :])

    @pl.when(i > 0)
    def _():
        attend(pl.multiple_of(n_ctx + (i - 1) * DA_TQ, n_ctx), DA_TQ, k_scr[...], v_scr[...])


def diff_attn_core(qkv, cos, sin, lam, subln_g, n_ctx, lam_init):
    b, t, _ = qkv.shape
    assert (t - n_ctx) % DA_TQ == 0 and DA_TQ % n_ctx == 0
    nh = DA_HEADS
    hw = 2 * DA_HALF
    kern = functools.partial(_diff_attn_kernel, n_ctx=n_ctx, lam_init=lam_init)
    table = pl.BlockSpec((t, hw), lambda bi, h, i: (0, 0))

    def col(off):
        return pl.BlockSpec((None, t, hw), lambda bi, h, i: (bi, 0, off + h))

    return pl.pallas_call(
        kern,
        out_shape=jax.ShapeDtypeStruct((b, t, nh * hw), BF16),
        grid=(b, nh, 1 + (t - n_ctx) // DA_TQ),
        in_specs=[pl.BlockSpec(memory_space=pltpu.SMEM), col(0), col(nh), col(2 * nh), table, table,
                  pl.BlockSpec((1, hw), lambda bi, h, i: (0, 0))],
        out_specs=col(0),
        scratch_shapes=[pltpu.VMEM((t, hw), BF16), pltpu.VMEM((t, 2 * hw), BF16)],
        compiler_params=_cparams("parallel", "parallel", "arbitrary"),
        name="diff_attn",
    )(lam.reshape(1).astype(F32), qkv, qkv, qkv, cos, sin, subln_g.reshape(1, hw).astype(F32))


NA_QROWS = 8
NA_KROWS = 2 * NA_QROWS
NA_DR_PAD = 16
NEG_INF = float("-inf")
NA_BIAS_ROWS = 2 * WIN_R - 1
NA_BIAS_COLS = 2 * WIN_C - 1


def _na_kernel(rpb_ref, q_ref, k_ref, v_ref, o_ref, bias_ref, *, n_ctx, rows):
    h = pl.program_id(0)
    b = pl.program_id(1)
    j = pl.program_id(2)
    w = GRID_W
    nq = NA_QROWS * w
    nk = NA_KROWS * w
    lane = lax.broadcasted_iota(jnp.int32, (w, 2 * w), 1)
    nt = (((1,), (1,)), ((), ()))

    @pl.when((b == 0) & (j == 0))
    def _():
        c = lax.broadcasted_iota(jnp.int32, (w, 2 * w), 0)
        kc = lane % w
        c0 = jnp.clip(c - WIN_C // 2, 0, w - WIN_C)
        in_win = (kc >= c0) & (kc < c0 + WIN_C)
        single = []
        for dr in range(-(WIN_R - 1), WIN_R):
            t = jnp.full((w, 2 * w), NEG_INF, F32)
            for dc in range(-(WIN_C - 1), WIN_C):
                t = jnp.where(kc - c == dc, rpb_ref[(h * NA_BIAS_ROWS + dr + WIN_R - 1) * NA_BIAS_COLS + dc + WIN_C - 1], t)
            single.append(jnp.where(in_win, t, NEG_INF))
        neg = jnp.full((w, 2 * w), NEG_INF, F32)

        def at(dr):
            return single[dr + WIN_R - 1] if abs(dr) < WIN_R else neg

        for d in range(2 * NA_DR_PAD + 1):
            bias_ref[d] = jnp.where(lane < w, at(d - NA_DR_PAD), at(d - NA_DR_PAD + 1))

    kctx = k_ref[0, :n_ctx, :]
    vctx = v_ref[0, :n_ctx, :]

    def finish(parts, q_lo, n):
        m = parts[0][0].max(axis=-1, keepdims=True)
        for s, _ in parts[1:]:
            m = jnp.maximum(m, s.max(axis=-1, keepdims=True))
        l = 0.0
        o = 0.0
        for s, vv in parts:
            p = jnp.exp(s - m)
            l = l + p.sum(axis=-1, keepdims=True)
            o = o + jnp.dot(p.astype(BF16), vv, preferred_element_type=F32)
        o_ref[0, pl.ds(q_lo, n), :] = (o * (1.0 / l)).astype(o_ref.dtype)

    @pl.when(j < rows // NA_QROWS)
    def _():
        r_lo = j * NA_QROWS
        ks = jnp.clip(r_lo - WIN_R // 2, 0, rows - NA_KROWS)
        q_lo = pl.multiple_of(n_ctx + r_lo * w, w)
        k_lo = pl.multiple_of(n_ctx + ks * w, w)
        q = q_ref[0, pl.ds(q_lo, nq), :]
        kwin = k_ref[0, pl.ds(k_lo, nk), :]
        vwin = v_ref[0, pl.ds(k_lo, nk), :]
        s_win = lax.dot_general(q, kwin, nt, preferred_element_type=F32)
        half = (lane >= w).astype(jnp.int32)
        row_blocks = []
        for i in range(NA_QROWS):
            r = r_lo + i
            r0 = jnp.clip(r - WIN_R // 2, 0, rows - WIN_R)
            tiles = []
            for jj in range(NA_KROWS // 2):
                kr = ks + 2 * jj
                off = lax.bitcast_convert_type(half + (kr - r0), jnp.uint32)
                t = s_win[i * w:(i + 1) * w, 2 * jj * w:(2 * jj + 2) * w] + bias_ref[kr - r + NA_DR_PAD]
                tiles.append(jnp.where(off < WIN_R, t, NEG_INF))
            row_blocks.append(jnp.concatenate(tiles, axis=1))
        s_win = jnp.concatenate(row_blocks, axis=0)
        s_ctx = lax.dot_general(q, kctx, nt, preferred_element_type=F32)
        finish([(s_win, vwin), (s_ctx, vctx)], q_lo, nq)

    @pl.when(j == rows // NA_QROWS)
    def _():
        q = q_ref[0, :n_ctx, :]
        finish([(lax.dot_general(q, kctx, nt, preferred_element_type=F32), vctx)], 0, n_ctx)


def na_attn_core(qkv, rpb, n_ctx):
    b, t, _ = qkv.shape
    rows = (t - n_ctx) // GRID_W
    assert rows % NA_QROWS == 0 and rows >= NA_KROWS
    nh, hd = NA_HEADS, NA_DIM
    kern = functools.partial(_na_kernel, n_ctx=n_ctx, rows=rows)

    def col(off):
        return pl.BlockSpec((1, t, hd), lambda h, bi, j: (bi, 0, off + h))

    return pl.pallas_call(
        kern,
        out_shape=jax.ShapeDtypeStruct((b, t, nh * hd), BF16),
        grid=(nh, b, rows // NA_QROWS + 1),
        in_specs=[pl.BlockSpec(memory_space=pltpu.SMEM), col(0), col(nh), col(2 * nh)],
        out_specs=col(0),
        scratch_shapes=[pltpu.VMEM((2 * NA_DR_PAD + 1, GRID_W, 2 * GRID_W), F32)],
        compiler_params=_cparams("arbitrary", "arbitrary", "arbitrary"),
        name="na_attn",
    )(rpb.astype(F32).reshape(-1), qkv, qkv, qkv)


def proj(t, w, out_dtype=F32):
    b, tt, k = t.shape
    return mm(t.reshape(b * tt, k), w, out_dtype).reshape(b, tt, w.shape[1])


def axial_rope_tables(length, dim):
    n_freq = dim // 4
    t = jnp.arange(length)
    pos = jnp.stack([t // GRID_W, t % GRID_W], axis=-1).astype(F32)
    inv = ROPE_BASE ** (-jnp.arange(n_freq, dtype=F32) / n_freq)
    ang = pos[:, :, None] * inv
    return jnp.cos(ang), jnp.sin(ang)


def diff_attention(h, n_ctx, w_qkv, w_o, lam_vec, subln_g, layer_idx):
    t = h.shape[1]
    dq = 2 * DA_HEADS * DA_HALF
    lam_init = 0.8 - 0.6 * math.exp(-0.3 * layer_idx)
    lv = lam_vec.astype(F32)
    lam = jnp.exp(jnp.sum(lv[0] * lv[1])) - jnp.exp(jnp.sum(lv[2] * lv[3])) + lam_init
    w = jnp.concatenate([w_qkv[:, :dq] * DA_HALF ** -0.5, w_qkv[:, dq:]], axis=1)
    cos, sin = axial_rope_tables(t - n_ctx, DA_HALF)
    n_freq = cos.shape[-1]
    sign = jnp.array([-1.0, 1.0], F32)[None, None, :, None]
    cos_t = jnp.broadcast_to(cos[:, :, None, :], (t - n_ctx, 2, 2, n_freq)).reshape(t - n_ctx, DA_HALF)
    sin_t = (sin[:, :, None, :] * sign).reshape(t - n_ctx, DA_HALF)
    cos_t = jnp.concatenate([jnp.ones((n_ctx, 2 * DA_HALF), F32), jnp.tile(cos_t, (1, 2))], axis=0)
    sin_t = jnp.concatenate([jnp.zeros((n_ctx, 2 * DA_HALF), F32), jnp.tile(sin_t, (1, 2))], axis=0)
    o = diff_attn_core(proj(h, w, BF16), cos_t, sin_t, lam, subln_g, n_ctx, lam_init)
    return proj(o, w_o, BF16)


GDN_PREP_ROWS = 128
GDN_HALO = 8


def _gdn_prep_kernel(cur_ref, prev_ref, next_ref, w_ref, q_ref, k_ref, v_ref, *, first_latent_tile):
    j = pl.program_id(1)
    nt = pl.num_programs(1)
    tm = GDN_PREP_ROWS
    taps = w_ref.shape[0]
    left_ok = (j != 0) & (j != first_latent_tile)
    right_ok = (j != first_latent_tile - 1) & (j != nt - 1)
    n_ext = tm + 2 * GDN_HALO
    for grp, out_ref in enumerate((q_ref, k_ref, v_ref)):
        cols = slice(grp * out_ref.shape[1], (grp + 1) * out_ref.shape[1])
        prev = jnp.where(left_ok, prev_ref[:, cols], 0.0)
        nxt = jnp.where(right_ok, next_ref[:, cols], 0.0)
        ext = jnp.concatenate([prev, cur_ref[:, cols], nxt], axis=0)
        conv = None
        for tap in range(taps):
            shift = (taps // 2 - tap) % n_ext
            x = (pltpu.roll(ext, shift, axis=0) if shift else ext)[GDN_HALO:GDN_HALO + tm]
            term = x * w_ref[tap:tap + 1, cols]
            conv = term if conv is None else conv + term
        z = conv * jax.nn.sigmoid(conv)
        if out_ref is v_ref:
            out_ref[...] = z.astype(out_ref.dtype)
            continue
        scale = GDN_DK ** -0.5 if out_ref is q_ref else 1.0
        for h in range(GDN_HEADS):
            sl = slice(h * GDN_DK, (h + 1) * GDN_DK)
            zh = z[:, sl]
            out_ref[:, sl] = (zh * (lax.rsqrt(jnp.sum(zh * zh, axis=-1, keepdims=True) + EPS) * scale)
                              ).astype(out_ref.dtype)


def gdn_prep(zz, conv_w, n_ctx):
    b, t, _ = zz.shape
    tm, halo = GDN_PREP_ROWS, GDN_HALO
    hd = GDN_HEADS * GDN_DK
    wid = 3 * hd
    assert n_ctx % tm == 0 and t % tm == 0 and conv_w.shape[0] // 2 <= halo
    per = tm // halo
    out = pl.BlockSpec((None, tm, hd), lambda i, j: (i, j, 0))
    kern = functools.partial(_gdn_prep_kernel, first_latent_tile=n_ctx // tm)
    return pl.pallas_call(
        kern,
        out_shape=tuple(jax.ShapeDtypeStruct((b, t, hd), BF16) for _ in range(3)),
        grid=(b, t // tm),
        in_specs=[pl.BlockSpec((None, tm, wid), lambda i, j: (i, j, 0)),
                  pl.BlockSpec((None, halo, wid), lambda i, j: (i, jnp.maximum(j * per - 1, 0), 0)),
                  pl.BlockSpec((None, halo, wid), lambda i, j: (i, jnp.minimum((j + 1) * per, t // halo - 1), 0)),
                  pl.BlockSpec((conv_w.shape[0], wid), lambda i, j: (0, 0))],
        out_specs=(out, out, out),
        compiler_params=_cparams("parallel", "parallel"),
        name="gdn_prep",
    )(zz, zz, zz, conv_w.astype(F32))


GDN_HEAD_GROUP = 16
GDN_SUB = 16


def _bdot(a, b):
    return jnp.dot(a.astype(BF16), b.astype(BF16), preferred_element_type=F32)


def _scan_chunk(d, s, n, n_ctx_chunks):
    back = jnp.where(s < n_ctx_chunks, n_ctx_chunks - 1 - s, n + n_ctx_chunks - 1 - s)
    return jnp.where(d == 0, s, back)


def _gdn_kernel(qs_ref, k_ref, v_ref, gc_ref, bc_ref, gr_ref, o_ref, s_ref):
    d = pl.program_id(0)
    s = pl.program_id(3)
    c = GDN_CHUNK

    @pl.when(s == 0)
    def _():
        s_ref[...] = jnp.zeros_like(s_ref)

    row = lax.broadcasted_iota(jnp.int32, (c, c), 0)
    col = lax.broadcasted_iota(jnp.int32, (c, c), 1)
    ahead = (row - col) * (1 - 2 * d)
    strict = ahead > 0
    incl = ahead >= 0
    same_blk = (row // GDN_SUB) == (col // GDN_SUB)
    eye = (row == col).astype(F32)
    nt = (((1,), (1,)), ((), ()))
    tn = (((0,), (0,)), ((), ()))
    heads = range(GDN_HEAD_GROUP)
    sls = [slice(g * GDN_DK, (g + 1) * GDN_DK) for g in heads]

    def each(fn, *lists):
        return [fn(*vals) for vals in zip(*lists)]

    gcol = gc_ref[...]
    g_last = jnp.min(gcol, axis=0, keepdims=True)
    e_g, e_rest, e_all = jnp.exp(gcol), jnp.exp(g_last - gcol), jnp.exp(g_last)
    beta = bc_ref[...]
    ks = [k_ref[:, sl] for sl in sls]
    kf = [k.astype(F32) for k in ks]
    kb = [k * beta[:, g:g + 1] for g, k in zip(heads, kf)]
    kbg = [(k * e_g[:, g:g + 1]).astype(BF16) for g, k in zip(heads, kb)]
    kd = [(k * e_rest[:, g:g + 1]).astype(BF16) for g, k in zip(heads, kf)]
    vb = [(v_ref[:, sl].astype(F32) * beta[:, g:g + 1]).astype(BF16) for g, sl in zip(heads, sls)]
    qg = [(qs_ref[:, sl].astype(F32) * e_g[:, g:g + 1]).astype(BF16) for g, sl in zip(heads, sls)]
    decay = [jnp.exp(jnp.where(incl, gcol[:, g:g + 1] - gr_ref[g:g + 1, :], NEG_INF)) for g in heads]
    a = each(lambda m, k, dc: jnp.where(
        strict, lax.dot_general(m.astype(BF16), k, nt, preferred_element_type=F32) * dc, 0.0), kb, ks, decay)
    qk = each(lambda sl, k, dc: jnp.where(
        incl, lax.dot_general(qs_ref[:, sl], k, nt, preferred_element_type=F32) * dc, 0.0), sls, ks, decay)
    dblk = each(lambda m: jnp.where(same_blk, m, 0.0), a)
    d2 = each(lambda m: _bdot(m, m), dblk)
    x = each(lambda m, m2: _bdot(eye - m, eye + m2), dblk, d2)
    d4 = each(lambda m: _bdot(m, m), d2)
    x = each(lambda m, m4: _bdot(m, eye + m4), x, d4)
    d8 = each(lambda m: _bdot(m, m), d4)
    x = each(lambda m, m8: _bdot(m, eye + m8), x, d8)
    nmat = each(lambda m, am, dm: _bdot(m, am - dm), x, a, dblk)
    n2 = each(lambda m: _bdot(m, m), nmat)
    y = each(lambda m, m2: _bdot(eye - m, eye + m2), nmat, n2)
    t = each(_bdot, y, x)
    uw = each(lambda m, p, q: _bdot(m, jnp.concatenate([p, q], axis=1)), t, vb, kbg)
    state = [s_ref[g] for g in heads]
    v_new = each(lambda m, st: m[:, :GDN_DV] - _bdot(m[:, GDN_DV:], st), uw, state)
    o_state = each(_bdot, qg, state)
    o_new = each(_bdot, qk, v_new)
    s_new = each(lambda m, vn: lax.dot_general(m, vn.astype(BF16), tn, preferred_element_type=F32), kd, v_new)
    for g in heads:
        s_ref[g] = state[g] * e_all[:, g:g + 1] + s_new[g]
    o_ref[...] = jnp.concatenate(each(lambda p, q: p + q, o_state, o_new), axis=1)


def gated_delta_core(qs, k, v, g, beta, n_ctx_chunks):
    b, t, hd = k.shape
    c, hh, gg = GDN_CHUNK, GDN_HEADS, GDN_HEAD_GROUP
    n = t // c
    gw = gg * GDN_DK

    def cols(a):
        return a.reshape(2, b, t, hh // gg, gg).transpose(0, 1, 3, 2, 4)

    g_row = g.reshape(2, b, n, c, hh // gg, gg).transpose(0, 1, 2, 4, 5, 3)

    def chunk_of(d, s):
        return _scan_chunk(d, s, n, n_ctx_chunks)

    shared = pl.BlockSpec((None, c, gw), lambda d, bi, hg, s: (bi, chunk_of(d, s), hg))
    col_spec = pl.BlockSpec((None, None, None, c, gg), lambda d, bi, hg, s: (d, bi, hg, chunk_of(d, s), 0))
    return pl.pallas_call(
        _gdn_kernel,
        out_shape=jax.ShapeDtypeStruct((2, b, t, hd), F32),
        grid=(2, b, hh // gg, n),
        in_specs=[shared, shared, shared, col_spec, col_spec,
                  pl.BlockSpec((None, None, None, None, gg, c), lambda d, bi, hg, s: (d, bi, chunk_of(d, s), hg, 0, 0))],
        out_specs=pl.BlockSpec((None, None, c, gw), lambda d, bi, hg, s: (d, bi, chunk_of(d, s), hg)),
        scratch_shapes=[pltpu.VMEM((gg, GDN_DK, GDN_DV), F32)],
        compiler_params=_cparams("parallel", "parallel", "parallel", "arbitrary"),
        name="gated_delta",
    )(qs, k, v, cols(g), cols(beta), g_row)


def gated_deltanet(h, n_ctx, w_in, conv_w, w_ab, dt_bias, a_log, norm_g, w_o):
    B, T, _ = h.shape
    tc = n_ctx
    H, C = GDN_HEADS, GDN_CHUNK
    wq = H * GDN_DK
    hi = 2 * wq + H * GDN_DV

    zz = proj(h, w_in)
    q, k, v = gdn_prep(zz, conv_w, tc)
    ab = proj(h, w_ab).astype(F32).reshape(B, T, 2, 2, H)
    la = -jnp.exp(a_log.astype(F32)) * jax.nn.softplus(ab[:, :, 0] + dt_bias.astype(F32))
    be = jax.nn.sigmoid(ab[:, :, 1])
    n = T // C
    la_c = la.reshape(B, n, C, 2, H)
    g_f = jnp.cumsum(la_c[:, :, :, 0], axis=2)
    g_b = jnp.flip(jnp.cumsum(jnp.flip(la_c[:, :, :, 1], axis=2), axis=2), axis=2)
    g = jnp.stack([g_f, g_b]).reshape(2, B, T, H)
    o = gated_delta_core(q, k, v, g, jnp.moveaxis(be, 2, 0), tc // C)
    o = o.reshape(2, B * T, -1)
    y = gated_norm_proj((o, 0), (o, 1), zz.reshape(B * T, -1), hi // (H * GDN_DV), norm_g, w_o, silu_gate=True)
    return y.reshape(B, T, -1)


def neighbourhood_attention(h, n_ctx, w_qkv, rpb, w_o):
    hd = NA_HEADS * NA_DIM
    w = jnp.concatenate([w_qkv[:, :hd] * NA_DIM ** -0.5, w_qkv[:, hd:]], axis=1)
    return proj(na_attn_core(proj(h, w, BF16), rpb, n_ctx), w_o, BF16)


GLA_CHUNK = 64
GLA_LEVELS = (1, 2, 4, 8, 16, 32)


def _gla_kernel(zq_ref, zv_ref, zf_ref, lb_ref, bf_ref, o_ref, st_ref, *, reverse):
    s = pl.program_id(1)
    c = GLA_CHUNK

    @pl.when(s == 0)
    def _():
        st_ref[...] = jnp.zeros_like(st_ref)

    row = lax.broadcasted_iota(jnp.int32, (c, c), 0)
    col = lax.broadcasted_iota(jnp.int32, (c, c), 1)
    late, early = (col, row) if reverse else (row, col)
    masks = []
    for m in GLA_LEVELS:
        masks.append(((row // (2 * m)) == (col // (2 * m))) & ((late % (2 * m)) >= m) & ((early % (2 * m)) < m))
    diag = row == col
    trow = lax.broadcasted_iota(jnp.int32, (c, HG_DK), 0)
    nt = (((1,), (1,)), ((), ()))
    tn = (((0,), (0,)), ((), ()))
    heads = range(HG_HEADS)
    sls = [slice(g * HG_DK, (g + 1) * HG_DK) for g in heads]

    def ref_rows(b, m):
        p = m if reverse else m - 1
        if 2 * m >= 8:
            blocks = b.reshape(c // (2 * m), 2 * m, HG_DK)
            return jnp.broadcast_to(blocks[:, p:p + 1, :], blocks.shape).reshape(c, HG_DK)
        out = b
        for rho in range(2 * m):
            if rho != p:
                out = jnp.where(trow % (2 * m) == rho, pltpu.roll(b, (rho - p) % c, axis=0), out)
        return out

    zq = zq_ref[...]
    q_all = zq * jax.nn.sigmoid(zq) * HG_DK ** -0.5
    v_all = zv_ref[...].astype(BF16)
    z = zf_ref[...] + bf_ref[...]
    t = jnp.exp(-jnp.abs(z))
    r = 1.0 / (1.0 + t)
    pos = z >= 0
    lb = lb_ref[...]
    log_f = jnp.log(lb + (1.0 - lb) * jnp.where(pos, r, t * r))
    k_all = (1.0 - lb) * jnp.where(pos, t * r, r)
    scan = (col >= row) if reverse else (col <= row)
    b_all = jnp.dot(scan.astype(F32), log_f, preferred_element_type=F32, precision=lax.Precision.HIGHEST)
    qs = [q_all[:, sl] for sl in sls]
    ks = [k_all[:, sl] for sl in sls]
    bs = [b_all[:, sl] for sl in sls]
    vs = [v_all[:, sl] for sl in sls]
    att = [jnp.where(diag, lax.dot_general(q.astype(BF16), k.astype(BF16), nt, preferred_element_type=F32), 0.0)
           for q, k in zip(qs, ks)]
    for m, mask in zip(GLA_LEVELS, masks):
        es = [jnp.exp(-jnp.abs(b - ref_rows(b, m))) for b in bs]
        sc = [lax.dot_general((q * e).astype(BF16), (k * e).astype(BF16), nt, preferred_element_type=F32)
              for q, k, e in zip(qs, ks, es)]
        att = [jnp.where(mask, x, a) for x, a in zip(sc, att)]
    last = c - 1 if not reverse else 0
    b_last = [b[last:last + 1, :] for b in bs]
    state = [st_ref[g] for g in heads]
    o_state = [lax.dot_general((q * jnp.exp(b)).astype(BF16), st.astype(BF16), nt, preferred_element_type=F32)
               for q, b, st in zip(qs, bs, state)]
    o_new = [jnp.dot(a.astype(BF16), v, preferred_element_type=F32) for a, v in zip(att, vs)]
    s_new = [lax.dot_general(v, (k * jnp.exp(bl - b)).astype(BF16), tn, preferred_element_type=F32)
             for v, k, b, bl in zip(vs, ks, bs, b_last)]
    for g in heads:
        st_ref[g] = state[g] * jnp.exp(b_last[g]) + s_new[g]
    o_ref[...] = jnp.concatenate([x + y for x, y in zip(o_state, o_new)], axis=1)


def gla_core(zz, lb, b_f, n_ctx_chunks, reverse):
    bsz, t, _ = zz.shape
    hd = HG_HEADS * HG_DK
    c = GLA_CHUNK
    n = t // c

    def chunk_of(s):
        return _scan_chunk(1, s, n, n_ctx_chunks) if reverse else s

    def cols(j):
        return pl.BlockSpec((None, c, hd), lambda bi, s: (bi, chunk_of(s), j))

    vec = pl.BlockSpec((1, hd), lambda bi, s: (0, 0))
    return pl.pallas_call(
        functools.partial(_gla_kernel, reverse=reverse),
        out_shape=jax.ShapeDtypeStruct((bsz, t, hd), F32),
        grid=(bsz, n),
        in_specs=[cols(0), cols(1), cols(3 if reverse else 2), vec, vec],
        out_specs=cols(0),
        scratch_shapes=[pltpu.VMEM((HG_HEADS, HG_DV, HG_DK), F32)],
        compiler_params=_cparams("parallel", "arbitrary"),
        name="gla_bwd" if reverse else "gla_fwd",
    )(zz, zz, zz, lb.reshape(1, hd).astype(F32), b_f.reshape(1, hd).astype(F32))


def hgrn2(h, n_ctx, w_q, w_i, w_f, b_f, w_g, norm_g, w_o, lb):
    B, T, _ = h.shape
    C = GLA_CHUNK
    hw = HG_HEADS * HG_DK

    zz = proj(h, jnp.concatenate([w_q, w_i, w_f[0], w_f[1], w_g], axis=1))
    o_f = gla_core(zz, lb, b_f[0], n_ctx // C, False).reshape(B * T, hw)
    o_b = gla_core(zz, lb, b_f[1], n_ctx // C, True).reshape(B * T, hw)
    y = gated_norm_proj((o_f, None), (o_b, None), zz.reshape(B * T, -1), 4, norm_g, w_o, silu_gate=False)
    return y.reshape(B, T, -1)


def kernel(x, c, ctx, c_ctx, w_mod, b_mod, norm_mix_g, norm_ffn_g, da_w_qkv, da_w_o, da_lam, da_subln_g, gdn_w_in, gdn_conv, gdn_w_ab, gdn_dt_bias, gdn_a_log, gdn_norm_g, gdn_w_o, na_w_qkv, na_rpb, na_w_o, hg_w_q, hg_w_i, hg_w_f, hg_b_f, hg_w_g, hg_norm_g, hg_w_o, hg_lb_logits, w_router, b_router, e_w_gate, e_w_up, e_w_down, final_norm_g):
    B, L, D = x.shape
    tc = ctx.shape[1]
    T = tc + L
    xa = jnp.concatenate([ctx, x], axis=1)
    p_lb = jax.nn.softmax(hg_lb_logits.astype(F32), axis=0)
    lb_all = jnp.cumsum(p_lb, axis=0) - p_lb[0]
    cond_all = jnp.concatenate([jax.nn.silu(c), jax.nn.silu(c_ctx)[None]], axis=0)
    cond_all = jnp.pad(cond_all, ((0, -(B + 1) % LANES), (0, 0)))

    def mods(i):
        mod = mm(cond_all, w_mod[i])[:B + 1] + b_mod[i]
        both = jnp.stack([jnp.broadcast_to(mod[B], (B, 6 * D)), mod[:B]], axis=1)
        return [both[:, :, None, k * D:(k + 1) * D] for k in range(6)]

    sh1, sc1, g1, sh2, sc2, g2 = mods(0)
    (h,) = resid_norm(xa, tc, norm_mix_g[0], sc=sc1, sh=sh1)
    for i in range(DEPTH):
        kind, j = i % N_MIXERS, i // N_MIXERS
        if kind == 0:
            y = diff_attention(h, tc, da_w_qkv[j], da_w_o[j], da_lam[j], da_subln_g[j], i)
        elif kind == 1:
            y = gated_deltanet(h, tc, gdn_w_in[j], gdn_conv[j], gdn_w_ab[j], gdn_dt_bias[j],
                               gdn_a_log[j], gdn_norm_g[j], gdn_w_o[j])
        elif kind == 2:
            y = neighbourhood_attention(h, tc, na_w_qkv[j], na_rpb[j], na_w_o[j])
        else:
            y = hgrn2(h, tc, hg_w_q[j], hg_w_i[j], hg_w_f[j], hg_b_f[j], hg_w_g[j],
                      hg_norm_g[j], hg_w_o[j], lb_all[i])
        xa, h2, aff = resid_norm(xa, tc, norm_ffn_g[i], branches=(y,), gate=g1, sc=sc2, sh=sh2, w_router=w_router)
        ya, yb, w = grouped_moe(h2.reshape(B * T, D), aff.reshape(B * T, ROUTER_PAD)[:, :N_EXPERTS],
                                b_router, e_w_gate, e_w_up, e_w_down, i)
        branches = (ya.reshape(B, T, D), yb.reshape(B, T, D))
        w = w.reshape(B, T, TOP_K)
        if i == DEPTH - 1:
            return resid_norm(xa, tc, final_norm_g, branches=branches, weights=w, gate=g2, latent_only=True)[1]
        gate_ffn = g2
        sh1, sc1, g1, sh2, sc2, g2 = mods(i + 1)
        xa, h = resid_norm(xa, tc, norm_mix_g[i + 1], branches=branches, weights=w, gate=gate_ffn, sc=sc1, sh=sh1)
```

```python
import functools
import math

import jax
import jax.numpy as jnp
from jax import lax
from jax.experimental import pallas as pl
from jax.experimental.pallas import tpu as pltpu

F32 = jnp.float32
BF16 = jnp.bfloat16

D_MODEL = 2048
DEPTH = 4
GRID_W = 64
N_MIXERS = 4
EPS = 1e-6
DA_HEADS = 16
DA_HALF = D_MODEL // DA_HEADS // 2
DA_VDIM = 2 * DA_HALF
ROPE_BASE = 10000.0
GDN_HEADS = 16
GDN_DK = D_MODEL // GDN_HEADS
GDN_DV = D_MODEL // GDN_HEADS
GDN_CHUNK = 64
NA_HEADS = 16
NA_DIM = D_MODEL // NA_HEADS
WIN_R = 8
WIN_C = 16
HG_HEADS = 16
HG_DK = D_MODEL // HG_HEADS
HG_DV = D_MODEL // HG_HEADS
HG_CHUNK = 32
N_EXPERTS = 16
N_GROUPS = 4
EXPERTS_PER_GROUP = N_EXPERTS // N_GROUPS
GROUP_SCORE_TOPK = 2
TOP_K = 2
D_EXPERT = D_MODEL // 2

V7X_VMEM_LIMIT_BYTES = 56 * 1024 * 1024
LANES = 128
MOE_TILE = 512
MOE_CHUNKS = 8
ROUTER_PAD = LANES


def _largest_divisor(n, candidates):
    for c in candidates:
        if n % c == 0:
            return c
    raise ValueError(f"no tile in {candidates} divides {n}")


def _cparams(*sem):
    return pltpu.CompilerParams(dimension_semantics=sem, vmem_limit_bytes=V7X_VMEM_LIMIT_BYTES)


def _mm_kernel(a_ref, w_ref, o_ref):
    o_ref[...] = jnp.dot(a_ref[...], w_ref[...], preferred_element_type=F32).astype(o_ref.dtype)


def mm(a, w, out_dtype=F32):
    m, k = a.shape
    n = w.shape[1]
    n_pad = -n % LANES
    if n_pad:
        w = jnp.pad(w, ((0, 0), (0, n_pad)))
    a = a.astype(BF16)
    w = w.astype(BF16)
    np_ = n + n_pad
    tm = _largest_divisor(m, (1024, 512, 256, 128))
    tn = _largest_divisor(np_, (512, 256, 128))
    out = pl.pallas_call(
        _mm_kernel,
        out_shape=jax.ShapeDtypeStruct((m, np_), out_dtype),
        grid=(m // tm, np_ // tn),
        in_specs=[pl.BlockSpec((tm, k), lambda i, j: (i, 0)),
                  pl.BlockSpec((k, tn), lambda i, j: (0, j))],
        out_specs=pl.BlockSpec((tm, tn), lambda i, j: (i, j)),
        compiler_params=_cparams("parallel", "arbitrary"),
        name="proj_mm",
    )(a, w)
    return out[:, :n] if n_pad else out


def _gated_norm_proj_kernel(of_ref, ob_ref, zg_ref, g_ref, w_ref, o_ref, a_scr, *, head_dim, silu_gate):
    @pl.when(pl.program_id(1) == 0)
    def _():
        o = of_ref[...] + ob_ref[...]
        zg = zg_ref[...].astype(F32)
        sig = jax.nn.sigmoid(zg)
        gate = zg * sig if silu_gate else sig
        for h in range(o.shape[1] // head_dim):
            sl = slice(h * head_dim, (h + 1) * head_dim)
            oh = o[:, sl]
            oh = oh * lax.rsqrt(jnp.mean(oh * oh, axis=-1, keepdims=True) + EPS) * g_ref[...]
            a_scr[:, sl] = (oh * gate[:, sl]).astype(BF16)

    o_ref[...] = jnp.dot(a_scr[...], w_ref[...], preferred_element_type=F32).astype(o_ref.dtype)


def gated_norm_proj(o_fwd, o_bwd, zz, gate_block, norm_g, w_o, *, silu_gate):
    (fa, fi), (ba, bi) = o_fwd, o_bwd
    m, hd = fa.shape[-2:]
    k, n = w_o.shape
    assert k == hd
    head_dim = norm_g.shape[0]
    tm = _largest_divisor(m, (512, 256, 128))
    tn = _largest_divisor(n, (512, 256, 128))

    def rows(idx):
        if idx is None:
            return pl.BlockSpec((tm, hd), lambda i, j: (i, 0))
        return pl.BlockSpec((None, tm, hd), lambda i, j: (idx, i, 0))

    kern = functools.partial(_gated_norm_proj_kernel, head_dim=head_dim, silu_gate=silu_gate)
    return pl.pallas_call(
        kern,
        out_shape=jax.ShapeDtypeStruct((m, n), BF16),
        grid=(m // tm, n // tn),
        in_specs=[rows(fi), rows(bi),
                  pl.BlockSpec((tm, hd), lambda i, j: (i, gate_block)),
                  pl.BlockSpec((1, head_dim), lambda i, j: (0, 0)),
                  pl.BlockSpec((k, tn), lambda i, j: (0, j))],
        out_specs=pl.BlockSpec((tm, tn), lambda i, j: (i, j)),
        scratch_shapes=[pltpu.VMEM((tm, hd), BF16)],
        compiler_params=_cparams("parallel", "arbitrary"),
        name="gated_norm_proj",
    )(fa, ba, zz, norm_g.reshape(1, head_dim).astype(F32), w_o.astype(BF16))


def _resid_norm_kernel(*refs, n_branch, weighted, modulated, routed):
    refs = list(refs)
    x = refs.pop(0)[...].astype(F32)
    ys = [refs.pop(0)[...].astype(F32) for _ in range(n_branch)]
    if weighted:
        w = refs.pop(0)[...]
        ys = [y * w[:, k:k + 1] for k, y in enumerate(ys)]
    if n_branch:
        x = x + refs.pop(0)[...] * functools.reduce(lambda p, q: p + q, ys)
    g = refs.pop(0)[...]
    out = x * lax.rsqrt(jnp.mean(x * x, axis=-1, keepdims=True) + EPS) * g
    if modulated:
        sc = refs.pop(0)[...]
        sh = refs.pop(0)[...]
        out = out * (1.0 + sc) + sh
    wr = refs.pop(0) if routed else None
    if n_branch:
        refs.pop(0)[...] = x
    o_ref = refs.pop(0)
    o_ref[...] = out.astype(o_ref.dtype)
    if routed:
        logits = jnp.dot(out, wr[...], preferred_element_type=F32, precision=lax.Precision.HIGHEST)
        refs.pop(0)[...] = jax.nn.sigmoid(logits)


def resid_norm(x, n_ctx, gain, *, branches=(), weights=None, gate=None, sc=None, sh=None, w_router=None,
               latent_only=False):
    b, t, d = x.shape
    tm = n_ctx
    assert t % tm == 0
    off = 1 if latent_only else 0
    rows = pl.BlockSpec((None, tm, d), lambda i, j: (i, j + off, 0))
    out_rows = pl.BlockSpec((None, tm, d), lambda i, j: (i, j, 0))
    seg = pl.BlockSpec((None, None, 1, d), lambda i, j: (i, jnp.minimum(j + off, 1), 0, 0))
    args, specs = [x], [rows]
    for y in branches:
        args.append(y)
        specs.append(rows)
    if weights is not None:
        args.append(weights)
        specs.append(pl.BlockSpec((None, tm, weights.shape[-1]), lambda i, j: (i, j + off, 0)))
    if branches:
        args.append(gate)
        specs.append(seg)
    args.append(gain.reshape(1, d).astype(F32))
    specs.append(pl.BlockSpec((1, d), lambda i, j: (0, 0)))
    modulated = sc is not None
    if modulated:
        args += [sc, sh]
        specs += [seg, seg]
    routed = w_router is not None
    if routed:
        args.append(jnp.pad(w_router.astype(F32), ((0, 0), (0, ROUTER_PAD - w_router.shape[1]))))
        specs.append(pl.BlockSpec((d, ROUTER_PAD), lambda i, j: (0, 0)))
    t_out = t - off * tm
    out_shapes, out_specs = [], []
    if branches:
        out_shapes.append(jax.ShapeDtypeStruct((b, t_out, d), F32))
        out_specs.append(out_rows)
    out_shapes.append(jax.ShapeDtypeStruct((b, t_out, d), BF16 if modulated else F32))
    out_specs.append(out_rows)
    if routed:
        out_shapes.append(jax.ShapeDtypeStruct((b, t_out, ROUTER_PAD), F32))
        out_specs.append(pl.BlockSpec((None, tm, ROUTER_PAD), lambda i, j: (i, j, 0)))
    kern = functools.partial(_resid_norm_kernel, n_branch=len(branches), weighted=weights is not None,
                             modulated=modulated, routed=routed)
    return pl.pallas_call(
        kern,
        out_shape=tuple(out_shapes),
        grid=(b, t_out // tm),
        in_specs=specs,
        out_specs=tuple(out_specs),
        compiler_params=_cparams("parallel", "parallel"),
        name="resid_norm",
    )(*args)


def _moe_kernel(te_ref, tv_ref, x_ref, wg_ref, wu_ref, wd_ref, *rest, first_tile):
    o_ref, wg_s, wu_s, wd_s = rest[-4:]
    step = pl.program_id(0)
    i = step + first_tile

    @pl.when(tv_ref[i] > 0)
    def _():
        @pl.when((step == 0) | (te_ref[i] != te_ref[jnp.maximum(i - 1, 0)]))
        def _():
            wg_s[...] = wg_ref[0].astype(BF16)
            wu_s[...] = wu_ref[0].astype(BF16)
            wd_s[...] = wd_ref[0].astype(BF16)

        x = x_ref[...]
        g = jnp.dot(x, wg_s[...], preferred_element_type=F32)
        u = jnp.dot(x, wu_s[...], preferred_element_type=F32)
        act = (g * jax.nn.sigmoid(g) * u).astype(BF16)
        o_ref[...] = jnp.dot(act, wd_s[...], preferred_element_type=F32).astype(o_ref.dtype)

    @pl.when(tv_ref[i] == 0)
    def _():
        o_ref[...] = jnp.zeros_like(o_ref)


def _route(aff, b_router):
    assert GROUP_SCORE_TOPK == 2 and TOP_K == 2
    epg = EXPERTS_PER_GROUP
    sel = aff + b_router.astype(F32)
    s = [sel[:, e] for e in range(N_EXPERTS)]
    a = [aff[:, e] for e in range(N_EXPERTS)]

    def first_max(vals):
        idx, best = jnp.zeros_like(vals[0], dtype=jnp.int32), vals[0]
        for e in range(1, len(vals)):
            upd = vals[e] > best
            idx, best = jnp.where(upd, e, idx), jnp.where(upd, vals[e], best)
        return idx, best

    def pick(vals, idx):
        out = vals[0]
        for e in range(1, len(vals)):
            out = jnp.where(idx == e, vals[e], out)
        return out

    def top2_sum(v):
        pairs = [v[i] + v[j] for i in range(len(v)) for j in range(i + 1, len(v))]
        return functools.reduce(jnp.maximum, pairs)

    g_best, _ = first_max([top2_sum(s[g * epg:(g + 1) * epg]) for g in range(N_GROUPS)])
    in_s = [pick([s[g * epg + e] for g in range(N_GROUPS)], g_best) for e in range(epg)]
    in_a = [pick([a[g * epg + e] for g in range(N_GROUPS)], g_best) for e in range(epg)]
    i1, _ = first_max(in_s)
    i2, _ = first_max([jnp.where(i1 == e, NEG_INF, in_s[e]) for e in range(epg)])
    w1, w2 = pick(in_a, i1), pick(in_a, i2)
    tot = w1 + w2
    expert_idx = jnp.stack([g_best * epg + i1, g_best * epg + i2], axis=-1)
    return expert_idx.astype(jnp.int32), jnp.stack([w1 / tot, w2 / tot], axis=-1)


def grouped_moe(h2, aff, b_router, w_gate, w_up, w_down, layer):
    n, d = h2.shape
    expert_idx, w = _route(aff, b_router)
    n_slots = TOP_K * n
    e_flat = expert_idx.reshape(-1)
    onehot = (e_flat[:, None] == jnp.arange(N_EXPERTS, dtype=jnp.int32)[None, :]).astype(jnp.int32)
    csum = jnp.cumsum(onehot, axis=0)
    rank = jnp.take_along_axis(csum, e_flat[:, None], axis=1)[:, 0] - 1
    counts = csum[-1]
    padded = ((counts + MOE_TILE - 1) // MOE_TILE) * MOE_TILE
    pad_end = jnp.cumsum(padded)
    pad_off = pad_end - padded
    dest_flat = pad_off[e_flat] + rank
    dest = dest_flat.reshape(n, TOP_K)

    n_tiles = -(-n_slots // MOE_TILE) + N_EXPERTS
    n_pad_slots = n_tiles * MOE_TILE
    tile_start = jnp.arange(n_tiles, dtype=jnp.int32) * MOE_TILE
    tile_expert = jnp.minimum(jnp.sum((tile_start[:, None] >= pad_end[None, :]).astype(jnp.int32), axis=1),
                              N_EXPERTS - 1)
    tile_valid = (tile_start < pad_end[-1]).astype(jnp.int32)

    slot_token = jnp.zeros((n_pad_slots,), jnp.int32).at[dest_flat].set(
        jnp.arange(n_slots, dtype=jnp.int32) // TOP_K, unique_indices=True, mode="promise_in_bounds")

    f = D_EXPERT
    weights = (w_gate.astype(F32), w_up.astype(F32), w_down.astype(F32))
    bounds = [n_tiles * c // MOE_CHUNKS for c in range(MOE_CHUNKS + 1)]
    ys = None
    for lo, hi in zip(bounds[:-1], bounds[1:]):
        xs = h2.at[slot_token[lo * MOE_TILE:hi * MOE_TILE]].get(mode="promise_in_bounds")

        def w_spec(shape):
            return pl.BlockSpec((None, 1) + shape, lambda i, te, tv: (layer, te[i + lo], 0, 0),
                                pipeline_mode=pl.Buffered(1))

        in_specs = [pl.BlockSpec((MOE_TILE, d), lambda i, te, tv: (i, 0)), w_spec((d, f)), w_spec((d, f)), w_spec((f, d))]
        args = [tile_expert, tile_valid, xs, *weights]
        aliases = {}
        if ys is not None:
            in_specs.append(pl.BlockSpec(memory_space=pl.ANY))
            args.append(ys)
            aliases = {len(args) - 1: 0}
        ys = pl.pallas_call(
            functools.partial(_moe_kernel, first_tile=lo),
            out_shape=jax.ShapeDtypeStruct((n_pad_slots, d), BF16),
            grid_spec=pltpu.PrefetchScalarGridSpec(
                num_scalar_prefetch=2,
                grid=(hi - lo,),
                in_specs=in_specs,
                out_specs=pl.BlockSpec((MOE_TILE, d), lambda i, te, tv: (i + lo, 0)),
                scratch_shapes=[pltpu.VMEM((d, f), BF16), pltpu.VMEM((d, f), BF16), pltpu.VMEM((f, d), BF16)],
            ),
            input_output_aliases=aliases,
            compiler_params=_cparams("arbitrary"),
            name="moe_experts",
        )(*args)
    return (ys.at[dest[:, 0]].get(mode="promise_in_bounds"), ys.at[dest[:, 1]].get(mode="promise_in_bounds"), w)


DA_TQ = 512


def _diff_attn_kernel(lam_ref, q_ref, k_ref, v_ref, cos_ref, sin_ref, g_ref, o_ref, k_scr, v_scr,
                      *, n_ctx, lam_init):
    i = pl.program_id(2)
    t = k_ref.shape[0]
    hw = 2 * DA_HALF
    lam = lam_ref[0]
    lane = lax.broadcasted_iota(jnp.int32, (1, hw), 1)
    first_half = (lane % (DA_HALF // 2)) < DA_HALF // 4
    nt = (((1,), (1,)), ((), ()))

    def rope(x, lo, n):
        xf = x.astype(F32)
        partner = jnp.where(first_half, pltpu.roll(xf, hw - DA_HALF // 4, axis=1), pltpu.roll(xf, DA_HALF // 4, axis=1))
        return xf * cos_ref[pl.ds(lo, n), :] + partner * sin_ref[pl.ds(lo, n), :]

    @pl.when(i == 0)
    def _():
        def body(c, carry):
            lo = pl.multiple_of(c * n_ctx, n_ctx)
            k_scr[pl.ds(lo, n_ctx), :] = rope(k_ref[pl.ds(lo, n_ctx), :], lo, n_ctx).astype(BF16)
            return carry

        lax.fori_loop(0, t // n_ctx, body, 0)
        v_scr[:, :hw] = v_ref[...]
        v_scr[:, hw:] = jnp.ones((t, hw), BF16)

    def attend(lo, n, k, v_ext):
        q = rope(q_ref[pl.ds(lo, n), :], lo, n)

        def one_map(in_map):
            qm = jnp.where(in_map, q, 0.0).astype(BF16)
            s = lax.dot_general(qm, k, nt, preferred_element_type=F32)
            p = jnp.exp((s - jnp.max(s, axis=-1, keepdims=True)).astype(BF16))
            oe = jnp.dot(p, v_ext, preferred_element_type=F32)
            return oe[:, :hw] * (1.0 / oe[:, hw:hw + 1])

        o = one_map(lane < DA_HALF) - lam * one_map(lane >= DA_HALF)
        o = o * lax.rsqrt(jnp.mean(o * o, axis=-1, keepdims=True) + 1e-5)
        o_ref[pl.ds(lo, n), :] = ((o * g_ref[...]) * (1.0 - lam_init)).astype(o_ref.dtype)

    @pl.when(i == 0)
    def _():
        attend(0, n_ctx, k_scr[:n_ctx, :], v_scr[:n_ctx, :])

    @pl.when(i > 0)
    def _():
        attend(pl.multiple_of(n_ctx + (i - 1) * DA_TQ, n_ctx), DA_TQ, k_scr[...], v_scr[...])


def diff_attn_core(qkv, cos, sin, lam, subln_g, n_ctx, lam_init):
    b, t, _ = qkv.shape
    assert (t - n_ctx) % DA_TQ == 0 and DA_TQ % n_ctx == 0
    nh = DA_HEADS
    hw = 2 * DA_HALF
    kern = functools.partial(_diff_attn_kernel, n_ctx=n_ctx, lam_init=lam_init)
    table = pl.BlockSpec((t, hw), lambda bi, h, i: (0, 0))

    def col(off):
        return pl.BlockSpec((None, t, hw), lambda bi, h, i: (bi, 0, off + h))

    return pl.pallas_call(
        kern,
        out_shape=jax.ShapeDtypeStruct((b, t, nh * hw), BF16),
        grid=(b, nh, 1 + (t - n_ctx) // DA_TQ),
        in_specs=[pl.BlockSpec(memory_space=pltpu.SMEM), col(0), col(nh), col(2 * nh), table, table,
                  pl.BlockSpec((1, hw), lambda bi, h, i: (0, 0))],
        out_specs=col(0),
        scratch_shapes=[pltpu.VMEM((t, hw), BF16), pltpu.VMEM((t, 2 * hw), BF16)],
        compiler_params=_cparams("parallel", "parallel", "arbitrary"),
        name="diff_attn",
    )(lam.reshape(1).astype(F32), qkv, qkv, qkv, cos, sin, subln_g.reshape(1, hw).astype(F32))


NA_QROWS = 8
NA_KROWS = 2 * NA_QROWS
NA_DR_PAD = 16
NEG_INF = float("-inf")
NA_BIAS_ROWS = 2 * WIN_R - 1
NA_BIAS_COLS = 2 * WIN_C - 1


def _na_kernel(rpb_ref, q_ref, k_ref, v_ref, o_ref, bias_ref, *, n_ctx, rows):
    h = pl.program_id(0)
    b = pl.program_id(1)
    j = pl.program_id(2)
    w = GRID_W
    nq = NA_QROWS * w
    nk = NA_KROWS * w
    lane = lax.broadcasted_iota(jnp.int32, (w, 2 * w), 1)
    nt = (((1,), (1,)), ((), ()))

    @pl.when((b == 0) & (j == 0))
    def _():
        c = lax.broadcasted_iota(jnp.int32, (w, 2 * w), 0)
        kc = lane % w
        c0 = jnp.clip(c - WIN_C // 2, 0, w - WIN_C)
        in_win = (kc >= c0) & (kc < c0 + WIN_C)
        single = []
        for dr in range(-(WIN_R - 1), WIN_R):
            t = jnp.full((w, 2 * w), NEG_INF, F32)
            for dc in range(-(WIN_C - 1), WIN_C):
                t = jnp.where(kc - c == dc, rpb_ref[(h * NA_BIAS_ROWS + dr + WIN_R - 1) * NA_BIAS_COLS + dc + WIN_C - 1], t)
            single.append(jnp.where(in_win, t, NEG_INF))
        neg = jnp.full((w, 2 * w), NEG_INF, F32)

        def at(dr):
            return single[dr + WIN_R - 1] if abs(dr) < WIN_R else neg

        for d in range(2 * NA_DR_PAD + 1):
            bias_ref[d] = jnp.where(lane < w, at(d - NA_DR_PAD), at(d - NA_DR_PAD + 1))

    kctx = k_ref[0, :n_ctx, :]
    vctx = v_ref[0, :n_ctx, :]

    def finish(parts, q_lo, n):
        m = parts[0][0].max(axis=-1, keepdims=True)
        for s, _ in parts[1:]:
            m = jnp.maximum(m, s.max(axis=-1, keepdims=True))
        l = 0.0
        o = 0.0
        for s, vv in parts:
            p = jnp.exp(s - m)
            l = l + p.sum(axis=-1, keepdims=True)
            o = o + jnp.dot(p.astype(BF16), vv, preferred_element_type=F32)
        o_ref[0, pl.ds(q_lo, n), :] = (o * (1.0 / l)).astype(o_ref.dtype)

    @pl.when(j < rows // NA_QROWS)
    def _():
        r_lo = j * NA_QROWS
        ks = jnp.clip(r_lo - WIN_R // 2, 0, rows - NA_KROWS)
        q_lo = pl.multiple_of(n_ctx + r_lo * w, w)
        k_lo = pl.multiple_of(n_ctx + ks * w, w)
        q = q_ref[0, pl.ds(q_lo, nq), :]
        kwin = k_ref[0, pl.ds(k_lo, nk), :]
        vwin = v_ref[0, pl.ds(k_lo, nk), :]
        s_win = lax.dot_general(q, kwin, nt, preferred_element_type=F32)
        half = (lane >= w).astype(jnp.int32)
        row_blocks = []
        for i in range(NA_QROWS):
            r = r_lo + i
            r0 = jnp.clip(r - WIN_R // 2, 0, rows - WIN_R)
            tiles = []
            for jj in range(NA_KROWS // 2):
                kr = ks + 2 * jj
                off = lax.bitcast_convert_type(half + (kr - r0), jnp.uint32)
                t = s_win[i * w:(i + 1) * w, 2 * jj * w:(2 * jj + 2) * w] + bias_ref[kr - r + NA_DR_PAD]
                tiles.append(jnp.where(off < WIN_R, t, NEG_INF))
            row_blocks.append(jnp.concatenate(tiles, axis=1))
        s_win = jnp.concatenate(row_blocks, axis=0)
        s_ctx = lax.dot_general(q, kctx, nt, preferred_element_type=F32)
        finish([(s_win, vwin), (s_ctx, vctx)], q_lo, nq)

    @pl.when(j == rows // NA_QROWS)
    def _():
        q = q_ref[0, :n_ctx, :]
        finish([(lax.dot_general(q, kctx, nt, preferred_element_type=F32), vctx)], 0, n_ctx)


def na_attn_core(qkv, rpb, n_ctx):
    b, t, _ = qkv.shape
    rows = (t - n_ctx) // GRID_W
    assert rows % NA_QROWS == 0 and rows >= NA_KROWS
    nh, hd = NA_HEADS, NA_DIM
    kern = functools.partial(_na_kernel, n_ctx=n_ctx, rows=rows)

    def col(off):
        return pl.BlockSpec((1, t, hd), lambda h, bi, j: (bi, 0, off + h))

    return pl.pallas_call(
        kern,
        out_shape=jax.ShapeDtypeStruct((b, t, nh * hd), BF16),
        grid=(nh, b, rows // NA_QROWS + 1),
        in_specs=[pl.BlockSpec(memory_space=pltpu.SMEM), col(0), col(nh), col(2 * nh)],
        out_specs=col(0),
        scratch_shapes=[pltpu.VMEM((2 * NA_DR_PAD + 1, GRID_W, 2 * GRID_W), F32)],
        compiler_params=_cparams("arbitrary", "arbitrary", "arbitrary"),
        name="na_attn",
    )(rpb.astype(F32).reshape(-1), qkv, qkv, qkv)


def proj(t, w, out_dtype=F32):
    b, tt, k = t.shape
    return mm(t.reshape(b * tt, k), w, out_dtype).reshape(b, tt, w.shape[1])


def axial_rope_tables(length, dim):
    n_freq = dim // 4
    t = jnp.arange(length)
    pos = jnp.stack([t // GRID_W, t % GRID_W], axis=-1).astype(F32)
    inv = ROPE_BASE ** (-jnp.arange(n_freq, dtype=F32) / n_freq)
    ang = pos[:, :, None] * inv
    return jnp.cos(ang), jnp.sin(ang)


def diff_attention(h, n_ctx, w_qkv, w_o, lam_vec, subln_g, layer_idx):
    t = h.shape[1]
    dq = 2 * DA_HEADS * DA_HALF
    lam_init = 0.8 - 0.6 * math.exp(-0.3 * layer_idx)
    lv = lam_vec.astype(F32)
    lam = jnp.exp(jnp.sum(lv[0] * lv[1])) - jnp.exp(jnp.sum(lv[2] * lv[3])) + lam_init
    w = jnp.concatenate([w_qkv[:, :dq] * DA_HALF ** -0.5, w_qkv[:, dq:]], axis=1)
    cos, sin = axial_rope_tables(t - n_ctx, DA_HALF)
    n_freq = cos.shape[-1]
    sign = jnp.array([-1.0, 1.0], F32)[None, None, :, None]
    cos_t = jnp.broadcast_to(cos[:, :, None, :], (t - n_ctx, 2, 2, n_freq)).reshape(t - n_ctx, DA_HALF)
    sin_t = (sin[:, :, None, :] * sign).reshape(t - n_ctx, DA_HALF)
    cos_t = jnp.concatenate([jnp.ones((n_ctx, 2 * DA_HALF), F32), jnp.tile(cos_t, (1, 2))], axis=0)
    sin_t = jnp.concatenate([jnp.zeros((n_ctx, 2 * DA_HALF), F32), jnp.tile(sin_t, (1, 2))], axis=0)
    o = diff_attn_core(proj(h, w, BF16), cos_t, sin_t, lam, subln_g, n_ctx, lam_init)
    return proj(o, w_o, BF16)


GDN_PREP_ROWS = 128
GDN_HALO = 8


def _gdn_prep_kernel(cur_ref, prev_ref, next_ref, w_ref, q_ref, k_ref, v_ref, *, first_latent_tile):
    j = pl.program_id(1)
    nt = pl.num_programs(1)
    tm = GDN_PREP_ROWS
    taps = w_ref.shape[0]
    left_ok = (j != 0) & (j != first_latent_tile)
    right_ok = (j != first_latent_tile - 1) & (j != nt - 1)
    n_ext = tm + 2 * GDN_HALO
    for grp, out_ref in enumerate((q_ref, k_ref, v_ref)):
        cols = slice(grp * out_ref.shape[1], (grp + 1) * out_ref.shape[1])
        prev = jnp.where(left_ok, prev_ref[:, cols], 0.0)
        nxt = jnp.where(right_ok, next_ref[:, cols], 0.0)
        ext = jnp.concatenate([prev, cur_ref[:, cols], nxt], axis=0)
        conv = None
        for tap in range(taps):
            shift = (taps // 2 - tap) % n_ext
            x = (pltpu.roll(ext, shift, axis=0) if shift else ext)[GDN_HALO:GDN_HALO + tm]
            term = x * w_ref[tap:tap + 1, cols]
            conv = term if conv is None else conv + term
        z = conv * jax.nn.sigmoid(conv)
        if out_ref is v_ref:
            out_ref[...] = z.astype(out_ref.dtype)
            continue
        scale = GDN_DK ** -0.5 if out_ref is q_ref else 1.0
        for h in range(GDN_HEADS):
            sl = slice(h * GDN_DK, (h + 1) * GDN_DK)
            zh = z[:, sl]
            out_ref[:, sl] = (zh * (lax.rsqrt(jnp.sum(zh * zh, axis=-1, keepdims=True) + EPS) * scale)
                              ).astype(out_ref.dtype)


def gdn_prep(zz, conv_w, n_ctx):
    b, t, _ = zz.shape
    tm, halo = GDN_PREP_ROWS, GDN_HALO
    hd = GDN_HEADS * GDN_DK
    wid = 3 * hd
    assert n_ctx % tm == 0 and t % tm == 0 and conv_w.shape[0] // 2 <= halo
    per = tm // halo
    out = pl.BlockSpec((None, tm, hd), lambda i, j: (i, j, 0))
    kern = functools.partial(_gdn_prep_kernel, first_latent_tile=n_ctx // tm)
    return pl.pallas_call(
        kern,
        out_shape=tuple(jax.ShapeDtypeStruct((b, t, hd), BF16) for _ in range(3)),
        grid=(b, t // tm),
        in_specs=[pl.BlockSpec((None, tm, wid), lambda i, j: (i, j, 0)),
                  pl.BlockSpec((None, halo, wid), lambda i, j: (i, jnp.maximum(j * per - 1, 0), 0)),
                  pl.BlockSpec((None, halo, wid), lambda i, j: (i, jnp.minimum((j + 1) * per, t // halo - 1), 0)),
                  pl.BlockSpec((conv_w.shape[0], wid), lambda i, j: (0, 0))],
        out_specs=(out, out, out),
        compiler_params=_cparams("parallel", "parallel"),
        name="gdn_prep",
    )(zz, zz, zz, conv_w.astype(F32))


GDN_HEAD_GROUP = 16
GDN_SUB = 16


def _bdot(a, b):
    return jnp.dot(a.astype(BF16), b.astype(BF16), preferred_element_type=F32)


def _scan_chunk(d, s, n, n_ctx_chunks):
    back = jnp.where(s < n_ctx_chunks, n_ctx_chunks - 1 - s, n + n_ctx_chunks - 1 - s)
    return jnp.where(d == 0, s, back)


def _gdn_kernel(qs_ref, k_ref, v_ref, gc_ref, bc_ref, gr_ref, o_ref, s_ref):
    d = pl.program_id(0)
    s = pl.program_id(3)
    c = GDN_CHUNK

    @pl.when(s == 0)
    def _():
        s_ref[...] = jnp.zeros_like(s_ref)

    row = lax.broadcasted_iota(jnp.int32, (c, c), 0)
    col = lax.broadcasted_iota(jnp.int32, (c, c), 1)
    ahead = (row - col) * (1 - 2 * d)
    strict = ahead > 0
    incl = ahead >= 0
    same_blk = (row // GDN_SUB) == (col // GDN_SUB)
    eye = (row == col).astype(F32)
    nt = (((1,), (1,)), ((), ()))
    tn = (((0,), (0,)), ((), ()))
    heads = range(GDN_HEAD_GROUP)
    sls = [slice(g * GDN_DK, (g + 1) * GDN_DK) for g in heads]

    def each(fn, *lists):
        return [fn(*vals) for vals in zip(*lists)]

    gcol = gc_ref[...]
    g_last = jnp.min(gcol, axis=0, keepdims=True)
    e_g, e_rest, e_all = jnp.exp(gcol), jnp.exp(g_last - gcol), jnp.exp(g_last)
    beta = bc_ref[...]
    ks = [k_ref[:, sl] for sl in sls]
    kf = [k.astype(F32) for k in ks]
    kb = [k * beta[:, g:g + 1] for g, k in zip(heads, kf)]
    kbg = [(k * e_g[:, g:g + 1]).astype(BF16) for g, k in zip(heads, kb)]
    kd = [(k * e_rest[:, g:g + 1]).astype(BF16) for g, k in zip(heads, kf)]
    vb = [(v_ref[:, sl].astype(F32) * beta[:, g:g + 1]).astype(BF16) for g, sl in zip(heads, sls)]
    qg = [(qs_ref[:, sl].astype(F32) * e_g[:, g:g + 1]).astype(BF16) for g, sl in zip(heads, sls)]
    decay = [jnp.exp(jnp.where(incl, gcol[:, g:g + 1] - gr_ref[g:g + 1, :], NEG_INF)) for g in heads]
    a = each(lambda m, k, dc: jnp.where(
        strict, lax.dot_general(m.astype(BF16), k, nt, preferred_element_type=F32) * dc, 0.0), kb, ks, decay)
    qk = each(lambda sl, k, dc: jnp.where(
        incl, lax.dot_general(qs_ref[:, sl], k, nt, preferred_element_type=F32) * dc, 0.0), sls, ks, decay)
    dblk = each(lambda m: jnp.where(same_blk, m, 0.0), a)
    d2 = each(lambda m: _bdot(m, m), dblk)
    x = each(lambda m, m2: _bdot(eye - m, eye + m2), dblk, d2)
    d4 = each(lambda m: _bdot(m, m), d2)
    x = each(lambda m, m4: _bdot(m, eye + m4), x, d4)
    d8 = each(lambda m: _bdot(m, m), d4)
    x = each(lambda m, m8: _bdot(m, eye + m8), x, d8)
    nmat = each(lambda m, am, dm: _bdot(m, am - dm), x, a, dblk)
    n2 = each(lambda m: _bdot(m, m), nmat)
    y = each(lambda m, m2: _bdot(eye - m, eye + m2), nmat, n2)
    t = each(_bdot, y, x)
    uw = each(lambda m, p, q: _bdot(m, jnp.concatenate([p, q], axis=1)), t, vb, kbg)
    state = [s_ref[g] for g in heads]
    v_new = each(lambda m, st: m[:, :GDN_DV] - _bdot(m[:, GDN_DV:], st), uw, state)
    o_state = each(_bdot, qg, state)
    o_new = each(_bdot, qk, v_new)
    s_new = each(lambda m, vn: lax.dot_general(m, vn.astype(BF16), tn, preferred_element_type=F32), kd, v_new)
    for g in heads:
        s_ref[g] = state[g] * e_all[:, g:g + 1] + s_new[g]
    o_ref[...] = jnp.concatenate(each(lambda p, q: p + q, o_state, o_new), axis=1)


def gated_delta_core(qs, k, v, g, beta, n_ctx_chunks):
    b, t, hd = k.shape
    c, hh, gg = GDN_CHUNK, GDN_HEADS, GDN_HEAD_GROUP
    n = t // c
    gw = gg * GDN_DK

    def cols(a):
        return a.reshape(2, b, t, hh // gg, gg).transpose(0, 1, 3, 2, 4)

    g_row = g.reshape(2, b, n, c, hh // gg, gg).transpose(0, 1, 2, 4, 5, 3)

    def chunk_of(d, s):
        return _scan_chunk(d, s, n, n_ctx_chunks)

    shared = pl.BlockSpec((None, c, gw), lambda d, bi, hg, s: (bi, chunk_of(d, s), hg))
    col_spec = pl.BlockSpec((None, None, None, c, gg), lambda d, bi, hg, s: (d, bi, hg, chunk_of(d, s), 0))
    return pl.pallas_call(
        _gdn_kernel,
        out_shape=jax.ShapeDtypeStruct((2, b, t, hd), F32),
        grid=(2, b, hh // gg, n),
        in_specs=[shared, shared, shared, col_spec, col_spec,
                  pl.BlockSpec((None, None, None, None, gg, c), lambda d, bi, hg, s: (d, bi, chunk_of(d, s), hg, 0, 0))],
        out_specs=pl.BlockSpec((None, None, c, gw), lambda d, bi, hg, s: (d, bi, chunk_of(d, s), hg)),
        scratch_shapes=[pltpu.VMEM((gg, GDN_DK, GDN_DV), F32)],
        compiler_params=_cparams("parallel", "parallel", "parallel", "arbitrary"),
        name="gated_delta",
    )(qs, k, v, cols(g), cols(beta), g_row)


def gated_deltanet(h, n_ctx, w_in, conv_w, w_ab, dt_bias, a_log, norm_g, w_o):
    B, T, _ = h.shape
    tc = n_ctx
    H, C = GDN_HEADS, GDN_CHUNK
    wq = H * GDN_DK
    hi = 2 * wq + H * GDN_DV

    zz = proj(h, w_in)
    q, k, v = gdn_prep(zz, conv_w, tc)
    ab = proj(h, w_ab).astype(F32).reshape(B, T, 2, 2, H)
    la = -jnp.exp(a_log.astype(F32)) * jax.nn.softplus(ab[:, :, 0] + dt_bias.astype(F32))
    be = jax.nn.sigmoid(ab[:, :, 1])
    n = T // C
    la_c = la.reshape(B, n, C, 2, H)
    g_f = jnp.cumsum(la_c[:, :, :, 0], axis=2)
    g_b = jnp.flip(jnp.cumsum(jnp.flip(la_c[:, :, :, 1], axis=2), axis=2), axis=2)
    g = jnp.stack([g_f, g_b]).reshape(2, B, T, H)
    o = gated_delta_core(q, k, v, g, jnp.moveaxis(be, 2, 0), tc // C)
    o = o.reshape(2, B * T, -1)
    y = gated_norm_proj((o, 0), (o, 1), zz.reshape(B * T, -1), hi // (H * GDN_DV), norm_g, w_o, silu_gate=True)
    return y.reshape(B, T, -1)


def neighbourhood_attention(h, n_ctx, w_qkv, rpb, w_o):
    hd = NA_HEADS * NA_DIM
    w = jnp.concatenate([w_qkv[:, :hd] * NA_DIM ** -0.5, w_qkv[:, hd:]], axis=1)
    return proj(na_attn_core(proj(h, w, BF16), rpb, n_ctx), w_o, BF16)


GLA_CHUNK = 64
GLA_LEVELS = (1, 2, 4, 8, 16, 32)


def _gla_kernel(zq_ref, zv_ref, zf_ref, lb_ref, bf_ref, o_ref, st_ref, *, reverse):
    s = pl.program_id(1)
    c = GLA_CHUNK

    @pl.when(s == 0)
    def _():
        st_ref[...] = jnp.zeros_like(st_ref)

    row = lax.broadcasted_iota(jnp.int32, (c, c), 0)
    col = lax.broadcasted_iota(jnp.int32, (c, c), 1)
    late, early = (col, row) if reverse else (row, col)
    masks = []
    for m in GLA_LEVELS:
        masks.append(((row // (2 * m)) == (col // (2 * m))) & ((late % (2 * m)) >= m) & ((early % (2 * m)) < m))
    diag = row == col
    trow = lax.broadcasted_iota(jnp.int32, (c, HG_DK), 0)
    nt = (((1,), (1,)), ((), ()))
    tn = (((0,), (0,)), ((), ()))
    heads = range(HG_HEADS)
    sls = [slice(g * HG_DK, (g + 1) * HG_DK) for g in heads]

    def ref_rows(b, m):
        p = m if reverse else m - 1
        if 2 * m >= 8:
            blocks = b.reshape(c // (2 * m), 2 * m, HG_DK)
            return jnp.broadcast_to(blocks[:, p:p + 1, :], blocks.shape).reshape(c, HG_DK)
        out = b
        for rho in range(2 * m):
            if rho != p:
                out = jnp.where(trow % (2 * m) == rho, pltpu.roll(b, (rho - p) % c, axis=0), out)
        return out

    zq = zq_ref[...]
    q_all = zq * jax.nn.sigmoid(zq) * HG_DK ** -0.5
    v_all = zv_ref[...].astype(BF16)
    z = zf_ref[...] + bf_ref[...]
    t = jnp.exp(-jnp.abs(z))
    r = 1.0 / (1.0 + t)
    pos = z >= 0
    lb = lb_ref[...]
    log_f = jnp.log(lb + (1.0 - lb) * jnp.where(pos, r, t * r))
    k_all = (1.0 - lb) * jnp.where(pos, t * r, r)
    scan = (col >= row) if reverse else (col <= row)
    b_all = jnp.dot(scan.astype(F32), log_f, preferred_element_type=F32, precision=lax.Precision.HIGHEST)
    qs = [q_all[:, sl] for sl in sls]
    ks = [k_all[:, sl] for sl in sls]
    bs = [b_all[:, sl] for sl in sls]
    vs = [v_all[:, sl] for sl in sls]
    att = [jnp.where(diag, lax.dot_general(q.astype(BF16), k.astype(BF16), nt, preferred_element_type=F32), 0.0)
           for q, k in zip(qs, ks)]
    for m, mask in zip(GLA_LEVELS, masks):
        es = [jnp.exp(-jnp.abs(b - ref_rows(b, m))) for b in bs]
        sc = [lax.dot_general((q * e).astype(BF16), (k * e).astype(BF16), nt, preferred_element_type=F32)
              for q, k, e in zip(qs, ks, es)]
        att = [jnp.where(mask, x, a) for x, a in zip(sc, att)]
    last = c - 1 if not reverse else 0
    b_last = [b[last:last + 1, :] for b in bs]
    state = [st_ref[g] for g in heads]
    o_state = [lax.dot_general((q * jnp.exp(b)).astype(BF16), st.astype(BF16), nt, preferred_element_type=F32)
               for q, b, st in zip(qs, bs, state)]
    o_new = [jnp.dot(a.astype(BF16), v, preferred_element_type=F32) for a, v in zip(att, vs)]
    s_new = [lax.dot_general(v, (k * jnp.exp(bl - b)).astype(BF16), tn, preferred_element_type=F32)
             for v, k, b, bl in zip(vs, ks, bs, b_last)]
    for g in heads:
        st_ref[g] = state[g] * jnp.exp(b_last[g]) + s_new[g]
    o_ref[...] = jnp.concatenate([x + y for x, y in zip(o_state, o_new)], axis=1)


def gla_core(zz, lb, b_f, n_ctx_chunks, reverse):
    bsz, t, _ = zz.shape
    hd = HG_HEADS * HG_DK
    c = GLA_CHUNK
    n = t // c

    def chunk_of(s):
        return _scan_chunk(1, s, n, n_ctx_chunks) if reverse else s

    def cols(j):
        return pl.BlockSpec((None, c, hd), lambda bi, s: (bi, chunk_of(s), j))

    vec = pl.BlockSpec((1, hd), lambda bi, s: (0, 0))
    return pl.pallas_call(
        functools.partial(_gla_kernel, reverse=reverse),
        out_shape=jax.ShapeDtypeStruct((bsz, t, hd), F32),
        grid=(bsz, n),
        in_specs=[cols(0), cols(1), cols(3 if reverse else 2), vec, vec],
        out_specs=cols(0),
        scratch_shapes=[pltpu.VMEM((HG_HEADS, HG_DV, HG_DK), F32)],
        compiler_params=_cparams("parallel", "arbitrary"),
        name="gla_bwd" if reverse else "gla_fwd",
    )(zz, zz, zz, lb.reshape(1, hd).astype(F32), b_f.reshape(1, hd).astype(F32))


def hgrn2(h, n_ctx, w_q, w_i, w_f, b_f, w_g, norm_g, w_o, lb):
    B, T, _ = h.shape
    C = GLA_CHUNK
    hw = HG_HEADS * HG_DK

    zz = proj(h, jnp.concatenate([w_q, w_i, w_f[0], w_f[1], w_g], axis=1))
    o_f = gla_core(zz, lb, b_f[0], n_ctx // C, False).reshape(B * T, hw)
    o_b = gla_core(zz, lb, b_f[1], n_ctx // C, True).reshape(B * T, hw)
    y = gated_norm_proj((o_f, None), (o_b, None), zz.reshape(B * T, -1), 4, norm_g, w_o, silu_gate=False)
    return y.reshape(B, T, -1)


def kernel(x, c, ctx, c_ctx, w_mod, b_mod, norm_mix_g, norm_ffn_g, da_w_qkv, da_w_o, da_lam, da_subln_g, gdn_w_in, gdn_conv, gdn_w_ab, gdn_dt_bias, gdn_a_log, gdn_norm_g, gdn_w_o, na_w_qkv, na_rpb, na_w_o, hg_w_q, hg_w_i, hg_w_f, hg_b_f, hg_w_g, hg_norm_g, hg_w_o, hg_lb_logits, w_router, b_router, e_w_gate, e_w_up, e_w_down, final_norm_g):
    B, L, D = x.shape
    tc = ctx.shape[1]
    T = tc + L
    xa = jnp.concatenate([ctx, x], axis=1)
    p_lb = jax.nn.softmax(hg_lb_logits.astype(F32), axis=0)
    lb_all = jnp.cumsum(p_lb, axis=0) - p_lb[0]
    cond_all = jnp.concatenate([jax.nn.silu(c), jax.nn.silu(c_ctx)[None]], axis=0)
    cond_all = jnp.pad(cond_all, ((0, -(B + 1) % LANES), (0, 0)))

    def mods(i):
        mod = mm(cond_all, w_mod[i])[:B + 1] + b_mod[i]
        both = jnp.stack([jnp.broadcast_to(mod[B], (B, 6 * D)), mod[:B]], axis=1)
        return [both[:, :, None, k * D:(k + 1) * D] for k in range(6)]

    sh1, sc1, g1, sh2, sc2, g2 = mods(0)
    (h,) = resid_norm(xa, tc, norm_mix_g[0], sc=sc1, sh=sh1)
    for i in range(DEPTH):
        kind, j = i % N_MIXERS, i // N_MIXERS
        if kind == 0:
            y = diff_attention(h, tc, da_w_qkv[j], da_w_o[j], da_lam[j], da_subln_g[j], i)
        elif kind == 1:
            y = gated_deltanet(h, tc, gdn_w_in[j], gdn_conv[j], gdn_w_ab[j], gdn_dt_bias[j],
                               gdn_a_log[j], gdn_norm_g[j], gdn_w_o[j])
        elif kind == 2:
            y = neighbourhood_attention(h, tc, na_w_qkv[j], na_rpb[j], na_w_o[j])
        else:
            y = hgrn2(h, tc, hg_w_q[j], hg_w_i[j], hg_w_f[j], hg_b_f[j], hg_w_g[j],
                      hg_norm_g[j], hg_w_o[j], lb_all[i])
        xa, h2, aff = resid_norm(xa, tc, norm_ffn_g[i], branches=(y,), gate=g1, sc=sc2, sh=sh2, w_router=w_router)
        ya, yb, w = grouped_moe(h2.reshape(B * T, D), aff.reshape(B * T, ROUTER_PAD)[:, :N_EXPERTS],
                                b_router, e_w_gate, e_w_up, e_w_down, i)
        branches = (ya.reshape(B, T, D), yb.reshape(B, T, D))
        w = w.reshape(B, T, TOP_K)
        if i == DEPTH - 1:
            return resid_norm(xa, tc, final_norm_g, branches=branches, weights=w, gate=g2, latent_only=True)[1]
        gate_ffn = g2
        sh1, sc1, g1, sh2, sc2, g2 = mods(i + 1)
        xa, h = resid_norm(xa, tc, norm_mix_g[i + 1], branches=branches, weights=w, gate=gate_ffn, sc=sc1, sh=sh1)
```
